```python
import jax, jax.numpy as jnp
from jax import lax
import numpy as np

D_MODEL = 1024
BATCH = 8
SEQ = 4096
DEPTH = 1
DEC_BATCH = 128
DEC_SEQ = 8
PAST_LEN = 16384
PAGE_SIZE = 128

N_HEADS = 8
N_KV_HEADS = 2
HEAD_DIM = 64
Q_GROUP = N_HEADS // N_KV_HEADS
WINDOW = 128
ATT_WIDTH = N_HEADS * HEAD_DIM
KV_WIDTH = N_KV_HEADS * HEAD_DIM
CHUNK = 128
SGU_WIDTH = D_MODEL // 2
SGU_GROUPS = 4
SGU_GROUP_DIM = SGU_WIDTH // SGU_GROUPS
N_BRANCH = 2
SPLITS = (ATT_WIDTH,
          ATT_WIDTH + KV_WIDTH,
          ATT_WIDTH + 2 * KV_WIDTH,
          ATT_WIDTH + 2 * KV_WIDTH + SGU_WIDTH,
          ATT_WIDTH + 2 * KV_WIDTH + 2 * SGU_WIDTH,
          ATT_WIDTH + 2 * KV_WIDTH + 2 * SGU_WIDTH + D_MODEL)
IN_WIDTH = ATT_WIDTH + 2 * KV_WIDTH + 2 * SGU_WIDTH + N_BRANCH * D_MODEL
PEER_HEADS = 8
PEER_N_KEYS = 128
PEER_N_EXPERTS = PEER_N_KEYS * PEER_N_KEYS
PEER_TOPK = 16
PEER_QDIM = 256
PEER_HALF = PEER_QDIM // 2
PEER_BLOCK = 256
PLE_DIM = 256
EPS = 1e-6
NEG_INF = -1e30

kernel_name = "hybrid_swa_sgu_peer_decoder_step"


def _rms_norm(x, g):
    x32 = x.astype(jnp.float32)
    y = x32 * lax.rsqrt(jnp.mean(x32 * x32, axis=-1, keepdims=True) + EPS)
    return (y * g.astype(jnp.float32)).astype(x.dtype)


def _layer_norm(x, g, b):
    x32 = x.astype(jnp.float32)
    mu = jnp.mean(x32, axis=-1, keepdims=True)
    xc = x32 - mu
    y = xc * lax.rsqrt(jnp.mean(xc * xc, axis=-1, keepdims=True) + EPS)
    return (y * g.astype(jnp.float32) + b.astype(jnp.float32)).astype(x.dtype)


def _alibi_slopes():
    h = jnp.arange(1, N_HEADS + 1, dtype=jnp.float32)
    return jnp.exp2(-8.0 * h / N_HEADS)


def _sink_attend(q, k, v, dist, valid, sinks):
    scores = jnp.einsum('...qkgd,...skd->...kgqs', q, k).astype(jnp.float32) * (HEAD_DIM ** -0.5)
    slopes = _alibi_slopes().reshape(N_KV_HEADS, Q_GROUP, 1, 1)
    scores = scores - slopes * dist.astype(jnp.float32)
    scores = jnp.where(valid, scores, NEG_INF)
    sink = jnp.broadcast_to(sinks.astype(jnp.float32).reshape(N_KV_HEADS, Q_GROUP, 1, 1),
                            scores.shape[:-1] + (1,))
    probs = jax.nn.softmax(jnp.concatenate([scores, sink], axis=-1), axis=-1)[..., :-1]
    return jnp.einsum('...kgqs,...skd->...qkgd', probs.astype(v.dtype), v)


def _window_attention_prompt(q, k, v, sinks):
    B, S = q.shape[:2]
    nb = S // WINDOW
    qb = q.reshape(B, nb, WINDOW, N_KV_HEADS, Q_GROUP, HEAD_DIM)
    kb = k.reshape(B, nb, WINDOW, N_KV_HEADS, HEAD_DIM)
    vb = v.reshape(B, nb, WINDOW, N_KV_HEADS, HEAD_DIM)
    pad = ((0, 0), (1, 0), (0, 0), (0, 0), (0, 0))
    kk = jnp.concatenate([jnp.pad(kb[:, :-1], pad), kb], axis=2)
    vv = jnp.concatenate([jnp.pad(vb[:, :-1], pad), vb], axis=2)
    i = jnp.arange(WINDOW)[:, None]
    r = jnp.arange(2 * WINDOW)[None, :]
    dist = i - r + WINDOW
    has_prev = (jnp.arange(nb) > 0)[:, None, None]
    valid = (dist >= 0) & (dist < WINDOW) & (has_prev | (r >= WINDOW))
    out = _sink_attend(qb, kk, vv, dist, valid[:, None, None], sinks)
    return out.reshape(B, S, ATT_WIDTH)


def _window_attention_sample(q, k_new, v_new, k_past, v_past, sinks):
    Bd, L = q.shape[:2]
    W = k_past.shape[1]
    kk = jnp.concatenate([k_past.astype(k_new.dtype), k_new], axis=1)
    vv = jnp.concatenate([v_past.astype(v_new.dtype), v_new], axis=1)
    key_off = jnp.concatenate([jnp.arange(W) - W, jnp.arange(L)])
    dist = jnp.arange(L)[:, None] - key_off[None, :]
    valid = (dist >= 0) & (dist < WINDOW)
    out = _sink_attend(q, kk, vv, dist, valid, sinks)
    return out.reshape(Bd, L, ATT_WIDTH)


def _sgu_prompt(v, w_s, b_s):
    B, S = v.shape[:2]
    nc = S // CHUNK
    vb = v.reshape(B, nc, CHUNK, SGU_GROUPS, SGU_GROUP_DIM)
    wm = w_s * jnp.tril(jnp.ones((CHUNK, CHUNK), w_s.dtype))
    s = jnp.einsum('gts,bnsgd->bntgd', wm, vb) + b_s.T[:, :, None]
    return s.reshape(B, S, SGU_WIDTH)


def _sgu_sample(v, w_s, b_s):
    Bd, L = v.shape[:2]
    vb = v.reshape(Bd, L, SGU_GROUPS, SGU_GROUP_DIM)
    wm = w_s[:, :L, :L] * jnp.tril(jnp.ones((L, L), w_s.dtype))
    s = jnp.einsum('gts,bsgd->btgd', wm, vb) + b_s[:, :L].T[:, :, None]
    return s.reshape(Bd, L, SGU_WIDTH)


def _peer(xn, w_q, sub_keys, expert_u, expert_v):
    shape = xn.shape
    t = xn.reshape(-1, D_MODEL)
    n = t.shape[0]
    blk = min(PEER_BLOCK, n)
    nb = -(-n // blk)
    t = jnp.pad(t, ((0, nb * blk - n), (0, 0)))

    def one_block(xb):
        q = (xb @ w_q).reshape(blk, PEER_HEADS, 2, PEER_HALF)
        sc = jnp.einsum('thcd,hcnd->thcn', q, sub_keys).astype(jnp.float32)
        s_top, i_top = lax.top_k(sc, PEER_TOPK)
        cand = (s_top[:, :, 0, :, None] + s_top[:, :, 1, None, :]).reshape(blk, PEER_HEADS, -1)
        cand_idx = (i_top[:, :, 0, :, None] * PEER_N_KEYS + i_top[:, :, 1, None, :]).reshape(blk, PEER_HEADS, -1)
        best, pos = lax.top_k(cand, PEER_TOPK)
        idx = jnp.take_along_axis(cand_idx, pos, axis=-1)
        gate = jax.nn.softmax(best, axis=-1)
        u = expert_u[idx]
        act = jax.nn.gelu(jnp.einsum('thkd,td->thk', u, xb).astype(jnp.float32))
        w = (gate * act).astype(xb.dtype)
        return jnp.einsum('thk,thkd->td', w, expert_v[idx])

    out = lax.map(one_block, t.reshape(nb, blk, D_MODEL))
    return out.reshape(-1, D_MODEL)[:n].reshape(shape)


def _layer(x, p_i, lp, past_k, past_v):
    prompt = past_k is None
    B, S = x.shape[:2]
    xn = _rms_norm(x, lp['attn_norm_g'])
    z = xn @ lp['w_in']
    q, k, v, su, sv, g_a, g_b = jnp.split(z, SPLITS, axis=-1)
    q = _rms_norm(q.reshape(B, S, N_HEADS, HEAD_DIM), lp['q_norm_g'])
    q = q.reshape(B, S, N_KV_HEADS, Q_GROUP, HEAD_DIM)
    k = _rms_norm(k.reshape(B, S, N_KV_HEADS, HEAD_DIM), lp['k_norm_g'])
    v = v.reshape(B, S, N_KV_HEADS, HEAD_DIM)
    if prompt:
        a = _window_attention_prompt(q, k, v, lp['attn_sinks'])
    else:
        a = _window_attention_sample(q, k, v, past_k, past_v, lp['attn_sinks'])
    u = jax.nn.gelu(su)
    vn = _layer_norm(jax.nn.gelu(sv), lp['sgu_norm_g'], lp['sgu_norm_b'])
    if prompt:
        s = _sgu_prompt(vn, lp['sgu_w'], lp['sgu_b'])
    else:
        s = _sgu_sample(vn, lp['sgu_w'], lp['sgu_b'])
    m = u * s
    h = jax.nn.sigmoid(g_a) * (a @ lp['w_branch_a']) + jax.nn.sigmoid(g_b) * (m @ lp['w_branch_b'])
    x1 = x + h @ lp['w_out']
    x2 = x1 + _peer(_rms_norm(x1, lp['ffn_norm_g']), lp['peer_w_q'], lp['peer_sub_keys'],
                    lp['peer_u'], lp['peer_v'])
    gate = jax.nn.sigmoid(_rms_norm(x2, lp['ple_norm_g']) @ lp['w_ple_gate'])
    x3 = x2 + gate * (p_i @ lp['w_ple'])
    if prompt:
        wp = min(WINDOW, S)
        return x3, k[:, S - wp:], v[:, S - wp:], vn[:, S - CHUNK:]
    return x3, k, v, vn


def setup_inputs(seed: int = 0) -> dict:
    key = jax.random.key(seed)
    ks = jax.random.split(key, 32)
    f32 = jnp.float32

    def nrm(k, shape, scale):
        return jax.random.normal(k, shape, f32) * scale

    w_buf = min(WINDOW, PAST_LEN)
    return {
        'x_prompt': nrm(ks[0], (BATCH, SEQ, D_MODEL), 1.0),
        'x_sample': nrm(ks[1], (DEC_BATCH, DEC_SEQ, D_MODEL), 1.0),
        'cache_k': nrm(ks[2], (DEPTH, DEC_BATCH, w_buf, N_KV_HEADS, HEAD_DIM), 1.0),
        'cache_v': nrm(ks[3], (DEPTH, DEC_BATCH, w_buf, N_KV_HEADS, HEAD_DIM), 1.0),
        'p_prompt': nrm(ks[4], (DEPTH, BATCH, SEQ, PLE_DIM), 1.0),
        'p_sample': nrm(ks[5], (DEPTH, DEC_BATCH, DEC_SEQ, PLE_DIM), 1.0),
        'attn_norm_g': 1.0 + nrm(ks[6], (DEPTH, D_MODEL), 0.02),
        'w_in': nrm(ks[7], (DEPTH, D_MODEL, IN_WIDTH), D_MODEL ** -0.5),
        'q_norm_g': 1.0 + nrm(ks[8], (DEPTH, HEAD_DIM), 0.02),
        'k_norm_g': 1.0 + nrm(ks[9], (DEPTH, HEAD_DIM), 0.02),
        'attn_sinks': nrm(ks[10], (DEPTH, N_HEADS), 0.5),
        'sgu_norm_g': 1.0 + nrm(ks[11], (DEPTH, SGU_WIDTH), 0.02),
        'sgu_norm_b': nrm(ks[12], (DEPTH, SGU_WIDTH), 0.02),
        'sgu_w': nrm(ks[13], (DEPTH, SGU_GROUPS, CHUNK, CHUNK), CHUNK ** -0.5),
        'sgu_b': 1.0 + nrm(ks[14], (DEPTH, SGU_GROUPS, CHUNK), 0.02),
        'w_branch_a': nrm(ks[15], (DEPTH, ATT_WIDTH, D_MODEL), ATT_WIDTH ** -0.5),
        'w_branch_b': nrm(ks[16], (DEPTH, SGU_WIDTH, D_MODEL), SGU_WIDTH ** -0.5),
        'w_out': nrm(ks[17], (DEPTH, D_MODEL, D_MODEL), D_MODEL ** -0.5),
        'ffn_norm_g': 1.0 + nrm(ks[18], (DEPTH, D_MODEL), 0.02),
        'peer_w_q': nrm(ks[19], (DEPTH, D_MODEL, PEER_HEADS * PEER_QDIM), D_MODEL ** -0.5),
        'peer_sub_keys': nrm(ks[20], (DEPTH, PEER_HEADS, 2, PEER_N_KEYS, PEER_HALF), PEER_HALF ** -0.5),
        'peer_u': nrm(ks[21], (DEPTH, PEER_N_EXPERTS, D_MODEL), D_MODEL ** -0.5),
        'peer_v': nrm(ks[22], (DEPTH, PEER_N_EXPERTS, D_MODEL), (PEER_HEADS * PEER_TOPK) ** -0.5),
        'ple_norm_g': 1.0 + nrm(ks[23], (DEPTH, D_MODEL), 0.02),
        'w_ple': nrm(ks[24], (DEPTH, PLE_DIM, D_MODEL), PLE_DIM ** -0.5),
        'w_ple_gate': nrm(ks[25], (DEPTH, D_MODEL, D_MODEL), D_MODEL ** -0.5),
    }


def reference(x_prompt, x_sample, cache_k, cache_v, p_prompt, p_sample,
              attn_norm_g, w_in, q_norm_g, k_norm_g, attn_sinks,
              sgu_norm_g, sgu_norm_b, sgu_w, sgu_b,
              w_branch_a, w_branch_b, w_out,
              ffn_norm_g, peer_w_q, peer_sub_keys, peer_u, peer_v,
              ple_norm_g, w_ple, w_ple_gate):
    hp, hs = x_prompt, x_sample
    nkp, nvp, nks, nvs, nsp, nss = [], [], [], [], [], []
    for i in range(DEPTH):
        lp = dict(attn_norm_g=attn_norm_g[i], w_in=w_in[i], q_norm_g=q_norm_g[i],
                  k_norm_g=k_norm_g[i], attn_sinks=attn_sinks[i],
                  sgu_norm_g=sgu_norm_g[i], sgu_norm_b=sgu_norm_b[i], sgu_w=sgu_w[i], sgu_b=sgu_b[i],
                  w_branch_a=w_branch_a[i], w_branch_b=w_branch_b[i], w_out=w_out[i],
                  ffn_norm_g=ffn_norm_g[i], peer_w_q=peer_w_q[i], peer_sub_keys=peer_sub_keys[i],
                  peer_u=peer_u[i], peer_v=peer_v[i],
                  ple_norm_g=ple_norm_g[i], w_ple=w_ple[i], w_ple_gate=w_ple_gate[i])
        hp, kp, vp, sp = _layer(hp, p_prompt[i], lp, None, None)
        hs, ks_, vs_, ss = _layer(hs, p_sample[i], lp, cache_k[i], cache_v[i])
        nkp.append(kp); nvp.append(vp); nsp.append(sp)
        nks.append(ks_); nvs.append(vs_); nss.append(ss)
    new_k_prompt = jnp.stack(nkp)
    new_v_prompt = jnp.stack(nvp)
    new_k_sample = jnp.stack(nks)
    new_v_sample = jnp.stack(nvs)
    new_sgu_v_prompt = jnp.stack(nsp)
    new_sgu_v_sample = jnp.stack(nss)
    return (hp, hs, new_k_prompt, new_v_prompt, new_k_sample, new_v_sample, new_sgu_v_prompt, new_sgu_v_sample)
```

```python
import functools
import math

import jax
import jax.numpy as jnp
from jax import lax
from jax.experimental import pallas as pl
from jax.experimental.pallas import tpu as pltpu

F32 = jnp.float32
BF16 = jnp.bfloat16

N_HEADS = 8
N_KV_HEADS = 2
HEAD_DIM = 64
Q_GROUP = N_HEADS // N_KV_HEADS
WINDOW = 128
CHUNK = 128
SGU_GROUPS = 4
PEER_HEADS = 8
PEER_N_KEYS = 128
PEER_TOPK = 16
EPS = 1e-6
NEG_INF = -1e30
ALIBI_SLOPES = tuple(2.0 ** (-8.0 * h / N_HEADS) for h in range(1, N_HEADS + 1))

LANES = 128
VMEM_LIMIT = 56 * 1024 * 1024


def _params(*semantics):
    return pltpu.CompilerParams(dimension_semantics=semantics, vmem_limit_bytes=VMEM_LIMIT)


def _token_block(n):
    for tb in (512, 256, 128):
        if n % tb == 0:
            return tb
    raise ValueError(f"token count {n} must be a multiple of 128")


def _rms(x, g):
    return x * lax.rsqrt(jnp.mean(x * x, axis=-1, keepdims=True) + EPS) * g


def _group_rms(t, ones_blk, g):
    t2 = t * t
    hi = t2.astype(BF16)
    lo = (t2 - hi.astype(F32)).astype(BF16)
    ss = (jnp.dot(hi, ones_blk, preferred_element_type=F32)
          + jnp.dot(lo, ones_blk, preferred_element_type=F32))
    return t * lax.rsqrt(ss * (1.0 / HEAD_DIM) + EPS) * g


def _inproj_kernel(x_ref, g_ref, w_ref, qg_ref, kg_ref, lg_ref, lb_ref, bq_ref, bk_ref,
                   q_ref, k_ref, v_ref, u_ref, vn_ref, ga_ref, gb_ref):
    x = x_ref[...]
    xn = _rms(x, g_ref[...])
    z = jnp.dot(xn.astype(BF16), w_ref[...], preferred_element_type=F32)
    att = N_HEADS * HEAD_DIM
    kvw = N_KV_HEADS * HEAD_DIM
    sgw = (z.shape[1] - att - 2 * kvw) // 6
    o = 0
    q = z[:, o:o + att]; o += att
    k = z[:, o:o + kvw]; o += kvw
    v = z[:, o:o + kvw]; o += kvw
    su = z[:, o:o + sgw]; o += sgw
    sv = z[:, o:o + sgw]; o += sgw
    g_a = z[:, o:o + 2 * sgw]; o += 2 * sgw
    g_b = z[:, o:o + 2 * sgw]
    qn = _group_rms(q, bq_ref[...], qg_ref[...])
    q_ref[...] = (qn * (HEAD_DIM ** -0.5)).astype(BF16)
    k_ref[...] = _group_rms(k, bk_ref[...], kg_ref[...])
    v_ref[...] = v
    u_ref[...] = jax.nn.gelu(su).astype(BF16)
    gv = jax.nn.gelu(sv)
    mu = jnp.mean(gv, axis=-1, keepdims=True)
    gc = gv - mu
    vn_ref[...] = gc * lax.rsqrt(jnp.mean(gc * gc, axis=-1, keepdims=True) + EPS) * lg_ref[...] + lb_ref[...]
    ga_ref[...] = jax.nn.sigmoid(g_a).astype(BF16)
    gb_ref[...] = jax.nn.sigmoid(g_b).astype(BF16)


def _inproj(x, g, w_in, qg, kg, lg, lb):
    n, d = x.shape
    tb = _token_block(n)
    att = N_HEADS * HEAD_DIM
    kvw = N_KV_HEADS * HEAD_DIM
    sgw = d // 2
    hid = jnp.arange(att) // HEAD_DIM
    bq = (hid[:, None] == hid[None, :]).astype(BF16)
    bk = bq[:kvw, :kvw]
    const = lambda i: (0, 0)
    row = lambda i: (i, 0)
    outs = [
        jax.ShapeDtypeStruct((n, att), BF16),
        jax.ShapeDtypeStruct((n, kvw), F32),
        jax.ShapeDtypeStruct((n, kvw), F32),
        jax.ShapeDtypeStruct((n, sgw), BF16),
        jax.ShapeDtypeStruct((n, sgw), F32),
        jax.ShapeDtypeStruct((n, d), BF16),
        jax.ShapeDtypeStruct((n, d), BF16),
    ]
    return pl.pallas_call(
        _inproj_kernel,
        grid=(n // tb,),
        in_specs=[
            pl.BlockSpec((tb, d), row),
            pl.BlockSpec((1, d), const),
            pl.BlockSpec(w_in.shape, const),
            pl.BlockSpec((1, att), const),
            pl.BlockSpec((1, kvw), const),
            pl.BlockSpec((1, sgw), const),
            pl.BlockSpec((1, sgw), const),
            pl.BlockSpec((att, att), const),
            pl.BlockSpec((kvw, kvw), const),
        ],
        out_specs=[pl.BlockSpec((tb, s.shape[1]), row) for s in outs],
        out_shape=outs,
        compiler_params=_params("parallel"),
        name="inproj",
    )(x, g, w_in, qg, kg, lg, lb, bq, bk)


def _sink_softmax(s, sink):
    mx = jnp.maximum(jnp.max(s, axis=-1, keepdims=True), sink)
    p = jnp.exp(s - mx)
    den = jnp.sum(p, axis=-1, keepdims=True) + jnp.exp(sink - mx)
    return p / den


def _prompt_kernel(sinks_ref, q_ref, kc_ref, kp_ref, vc_ref, vp_ref, vn_ref, u_ref, w_ref, bias_ref,
                   a_ref, m_ref):
    i = pl.program_id(1)
    tq = q_ref.shape[1]
    nblk = tq // WINDOW
    q = q_ref[0]
    kc = kc_ref[0].astype(BF16)
    vc = vc_ref[0].astype(BF16)
    kp = kp_ref[0].astype(BF16)
    vp = vp_ref[0].astype(BF16)
    row = lax.broadcasted_iota(jnp.int32, (WINDOW, 2 * WINDOW), 0)
    col = lax.broadcasted_iota(jnp.int32, (WINDOW, 2 * WINDOW), 1)
    dist = row - col + WINDOW
    in_window = (dist >= 0) & (dist < WINDOW)
    distf = dist.astype(F32)
    for jq in range(nblk):
        r0 = jq * WINDOW
        if jq == 0:
            kprev, vprev = kp, vp
            valid = in_window & (col >= jnp.where(i > 0, 0, WINDOW))
        else:
            kprev, vprev = kc[r0 - WINDOW:r0], vc[r0 - WINDOW:r0]
            valid = in_window
        kcat = jnp.concatenate([kprev, kc[r0:r0 + WINDOW]], axis=0)
        vcat = jnp.concatenate([vprev, vc[r0:r0 + WINDOW]], axis=0)
        for g in range(N_KV_HEADS):
            heads = range(g * Q_GROUP, (g + 1) * Q_GROUP)
            qg = jnp.concatenate([q[r0:r0 + WINDOW, h * HEAD_DIM:(h + 1) * HEAD_DIM] for h in heads], axis=0)
            s_all = lax.dot_general(qg, kcat[:, g * HEAD_DIM:(g + 1) * HEAD_DIM],
                                    (((1,), (1,)), ((), ())), preferred_element_type=F32)
            probs = []
            for hl, h in enumerate(heads):
                s = s_all[hl * WINDOW:(hl + 1) * WINDOW] - ALIBI_SLOPES[h] * distf
                s = jnp.where(valid, s, NEG_INF)
                probs.append(_sink_softmax(s, sinks_ref[h]).astype(BF16))
            o_all = jnp.dot(jnp.concatenate(probs, axis=0), vcat[:, g * HEAD_DIM:(g + 1) * HEAD_DIM],
                            preferred_element_type=F32)
            for hl, h in enumerate(heads):
                a_ref[0, r0:r0 + WINDOW, h * HEAD_DIM:(h + 1) * HEAD_DIM] = (
                    o_all[hl * WINDOW:(hl + 1) * WINDOW].astype(BF16))
    tr = lax.broadcasted_iota(jnp.int32, (CHUNK, CHUNK), 0)
    tc = lax.broadcasted_iota(jnp.int32, (CHUNK, CHUNK), 1)
    gd = vn_ref.shape[2] // SGU_GROUPS
    wm = [jnp.where(tr >= tc, w_ref[g], 0.0).astype(BF16) for g in range(SGU_GROUPS)]
    for c in range(tq // CHUNK):
        r0 = c * CHUNK
        vnc = vn_ref[0, r0:r0 + CHUNK, :].astype(BF16)
        for g in range(SGU_GROUPS):
            s = jnp.dot(wm[g], vnc[:, g * gd:(g + 1) * gd], preferred_element_type=F32)
            s = s + bias_ref[:, g * gd:(g + 1) * gd]
            m_ref[0, r0:r0 + CHUNK, g * gd:(g + 1) * gd] = (
                u_ref[0, r0:r0 + CHUNK, g * gd:(g + 1) * gd].astype(F32) * s).astype(BF16)


def _prompt_mix(sinks, q, k, v, vn, u, sgu_w, sgu_bias):
    b, s, att = q.shape
    kvw = k.shape[2]
    sgw = vn.shape[2]
    tq = 512 if s % 512 == 0 else WINDOW
    assert s % tq == 0 and tq % WINDOW == 0 and WINDOW == CHUNK
    r = tq // WINDOW
    cur = lambda bi, i: (bi, i, 0)
    prev = lambda bi, i: (bi, jnp.maximum(i * r - 1, 0), 0)
    outs = [jax.ShapeDtypeStruct((b, s, att), BF16), jax.ShapeDtypeStruct((b, s, sgw), BF16)]
    return pl.pallas_call(
        _prompt_kernel,
        grid=(b, s // tq),
        in_specs=[
            pl.BlockSpec(memory_space=pltpu.SMEM),
            pl.BlockSpec((1, tq, att), cur),
            pl.BlockSpec((1, tq, kvw), cur),
            pl.BlockSpec((1, WINDOW, kvw), prev),
            pl.BlockSpec((1, tq, kvw), cur),
            pl.BlockSpec((1, WINDOW, kvw), prev),
            pl.BlockSpec((1, tq, sgw), cur),
            pl.BlockSpec((1, tq, sgw), cur),
            pl.BlockSpec(sgu_w.shape, lambda bi, i: (0, 0, 0)),
            pl.BlockSpec(sgu_bias.shape, lambda bi, i: (0, 0)),
        ],
        out_specs=[pl.BlockSpec((1, tq, att), cur), pl.BlockSpec((1, tq, sgw), cur)],
        out_shape=outs,
        compiler_params=_params("parallel", "parallel"),
        name="prompt_mix",
    )(sinks, q, k, k, v, v, vn, u, sgu_w, sgu_bias)


def _sample_kernel(sinks_ref, q_ref, kn_ref, vn_new_ref, ck_ref, cv_ref, vn_ref, u_ref, wexp_ref, bias_ref,
                   a_ref, m_ref):
    bb, w, _ = ck_ref.shape
    l = q_ref.shape[0] // bb
    per_seq = lambda ref: ref[...].astype(F32).reshape(bb, l, ref.shape[1])
    q = per_seq(q_ref)
    kcat = jnp.concatenate([ck_ref[...], per_seq(kn_ref)], axis=1).astype(BF16)
    vcat = jnp.concatenate([cv_ref[...], per_seq(vn_new_ref)], axis=1).astype(BF16)
    rows = Q_GROUP * l
    t = lax.broadcasted_iota(jnp.int32, (rows, w + l), 0) % l
    key = lax.broadcasted_iota(jnp.int32, (rows, w + l), 1)
    dist = t - (key - w)
    valid = (dist >= 0) & (dist < WINDOW)
    distf = dist.astype(F32)
    hl_of_row = lax.broadcasted_iota(jnp.int32, (rows, 1), 0) // l
    for g in range(N_KV_HEADS):
        heads = range(g * Q_GROUP, (g + 1) * Q_GROUP)
        qg = jnp.concatenate([q[:, :, h * HEAD_DIM:(h + 1) * HEAD_DIM] for h in heads], axis=1)
        s = jnp.einsum('bqd,bkd->bqk', qg.astype(BF16), kcat[:, :, g * HEAD_DIM:(g + 1) * HEAD_DIM],
                       preferred_element_type=F32)
        slope = jnp.zeros((rows, 1), F32)
        sink = jnp.zeros((rows, 1), F32)
        for hl, h in enumerate(heads):
            slope = jnp.where(hl_of_row == hl, ALIBI_SLOPES[h], slope)
            sink = jnp.where(hl_of_row == hl, sinks_ref[h], sink)
        s = jnp.where(valid[None], s - (slope * distf)[None], NEG_INF)
        p = _sink_softmax(s, sink[None]).astype(BF16)
        o = jnp.einsum('bqk,bkd->bqd', p, vcat[:, :, g * HEAD_DIM:(g + 1) * HEAD_DIM],
                       preferred_element_type=F32)
        for hl, h in enumerate(heads):
            a_ref[:, h * HEAD_DIM:(h + 1) * HEAD_DIM] = (
                o[:, hl * l:(hl + 1) * l, :].reshape(bb * l, HEAD_DIM).astype(BF16))
    vn = per_seq(vn_ref)
    tt = lax.broadcasted_iota(jnp.int32, (l, vn.shape[2]), 0)
    s = jnp.broadcast_to(bias_ref[...][None], vn.shape)
    for sp in range(l):
        wm = jnp.where(tt >= sp, wexp_ref[sp], 0.0)
        s = s + wm[None] * vn[:, sp:sp + 1, :]
    m_ref[...] = (per_seq(u_ref) * s).reshape(bb * l, vn.shape[2]).astype(BF16)


def _sample_mix(sinks, q, k, v, cache_k, cache_v, vn, u, wexp, bias):
    b, w, kvw = cache_k.shape
    l = q.shape[0] // b
    att = q.shape[1]
    sgw = vn.shape[1]
    bb = 16 if b % 16 == 0 else b
    tok = lambda width: pl.BlockSpec((bb * l, width), lambda i: (i, 0))
    past = pl.BlockSpec((bb, w, kvw), lambda i: (i, 0, 0))
    outs = [jax.ShapeDtypeStruct((b * l, att), BF16), jax.ShapeDtypeStruct((b * l, sgw), BF16)]
    return pl.pallas_call(
        _sample_kernel,
        grid=(b // bb,),
        in_specs=[
            pl.BlockSpec(memory_space=pltpu.SMEM),
            tok(att), tok(kvw), tok(kvw), past, past, tok(sgw), tok(sgw),
            pl.BlockSpec(wexp.shape, lambda i: (0, 0, 0)),
            pl.BlockSpec(bias.shape, lambda i: (0, 0)),
        ],
        out_specs=[tok(att), tok(sgw)],
        out_shape=outs,
        compiler_params=_params("parallel"),
        name="sample_mix",
    )(sinks, q, k, v, cache_k, cache_v, vn, u, wexp, bias)


def _merge_kernel(x_ref, a_ref, m_ref, ga_ref, gb_ref, wa_ref, wb_ref, wo_ref, fg_ref, wq_ref, keys_ref,
                  x1_ref, xn_ref, st_ref):
    ha = jnp.dot(a_ref[...], wa_ref[...], preferred_element_type=F32)
    hb = jnp.dot(m_ref[...], wb_ref[...], preferred_element_type=F32)
    h = ga_ref[...].astype(F32) * ha + gb_ref[...].astype(F32) * hb
    x1 = x_ref[...] + jnp.dot(h.astype(BF16), wo_ref[...], preferred_element_type=F32)
    x1_ref[...] = x1
    xn = _rms(x1, fg_ref[...]).astype(BF16)
    xn_ref[...] = xn
    qp = jnp.dot(xn, wq_ref[...], preferred_element_type=F32).astype(BF16)
    half = keys_ref.shape[2]
    nsub = st_ref.shape[0]
    for hc in range(keys_ref.shape[0]):
        st = lax.dot_general(keys_ref[hc], qp[:, hc * half:(hc + 1) * half],
                             (((1,), (1,)), ((), ())), preferred_element_type=F32)
        for tl in range(nsub):
            st_ref[tl, hc] = st[:, tl * LANES:(tl + 1) * LANES]


def _merge(x, a, m, ga, gb, wa, wb, wo, fg, wq, keys):
    n, d = x.shape
    tb = _token_block(n)
    nsub = tb // LANES
    hc, nk, half = keys.shape
    row = lambda i: (i, 0)
    const2 = lambda i: (0, 0)
    outs = [
        jax.ShapeDtypeStruct((n, d), F32),
        jax.ShapeDtypeStruct((n, d), BF16),
        jax.ShapeDtypeStruct((n // LANES, hc, nk, LANES), F32),
    ]
    return pl.pallas_call(
        _merge_kernel,
        grid=(n // tb,),
        in_specs=[
            pl.BlockSpec((tb, d), row),
            pl.BlockSpec((tb, a.shape[1]), row),
            pl.BlockSpec((tb, m.shape[1]), row),
            pl.BlockSpec((tb, d), row),
            pl.BlockSpec((tb, d), row),
            pl.BlockSpec(wa.shape, const2),
            pl.BlockSpec(wb.shape, const2),
            pl.BlockSpec(wo.shape, const2),
            pl.BlockSpec((1, d), const2),
            pl.BlockSpec(wq.shape, const2),
            pl.BlockSpec(keys.shape, lambda i: (0, 0, 0)),
        ],
        out_specs=[
            pl.BlockSpec((tb, d), row),
            pl.BlockSpec((tb, d), row),
            pl.BlockSpec((nsub, hc, nk, LANES), lambda i: (i, 0, 0, 0)),
        ],
        out_shape=outs,
        compiler_params=_params("parallel"),
        name="merge",
    )(x, a, m, ga, gb, wa, wb, wo, fg, wq, keys)


def _top_values(s, k):
    rows = lax.broadcasted_iota(jnp.int32, s.shape, 0)
    vals = []
    cur = s
    for _ in range(k):
        mx = jnp.max(cur, axis=0, keepdims=True)
        vals.append(mx)
        first = jnp.min(jnp.where(cur == mx, rows, s.shape[0]), axis=0, keepdims=True)
        cur = jnp.where(rows == first, -jnp.inf, cur)
    return vals


def _thresh_kernel(st_ref, tau_ref, e1_ref, e2_ref):
    for h in range(PEER_HEADS):
        s1 = st_ref[0, 2 * h]
        s2 = st_ref[0, 2 * h + 1]
        a = jnp.concatenate(_top_values(s1, PEER_TOPK), axis=0)
        b = jnp.concatenate(_top_values(s2, PEER_TOPK), axis=0)
        cand = (a[:, None, :] + b[None, :, :]).reshape(PEER_TOPK * PEER_TOPK, a.shape[1])
        best = _top_values(cand, PEER_TOPK)
        z = jnp.ones_like(best[0])
        for r in range(1, PEER_TOPK):
            z = z + jnp.exp(best[r] - best[0])
        tau_ref[0, h:h + 1, :] = best[PEER_TOPK - 1]
        e1_ref[0, h] = jnp.exp(s1 - a[0:1]) / z
        e2_ref[0, h] = jnp.exp(s2 - b[0:1])


def _thresholds(st):
    nsub, hc, nk, lanes = st.shape
    heads = hc // 2
    outs = [
        jax.ShapeDtypeStruct((nsub, heads, lanes), F32),
        jax.ShapeDtypeStruct((nsub, heads, nk, lanes), F32),
        jax.ShapeDtypeStruct((nsub, heads, nk, lanes), F32),
    ]
    return pl.pallas_call(
        _thresh_kernel,
        grid=(nsub,),
        in_specs=[pl.BlockSpec((1, hc, nk, lanes), lambda i: (i, 0, 0, 0))],
        out_specs=[
            pl.BlockSpec((1, heads, lanes), lambda i: (i, 0, 0)),
            pl.BlockSpec((1, heads, nk, lanes), lambda i: (i, 0, 0, 0)),
            pl.BlockSpec((1, heads, nk, lanes), lambda i: (i, 0, 0, 0)),
        ],
        out_shape=outs,
        compiler_params=_params("parallel"),
        name="peer_thresholds",
    )(st)


def _peer_kernel(xn_ref, u_ref, vt_ref, st_ref, tau_ref, e1_ref, e2_ref, out_ref, at_ref, wt_ref, acc_ref):
    c = pl.program_id(1)
    nsub = st_ref.shape[0]
    ec = u_ref.shape[0]
    nk = st_ref.shape[2]

    @pl.when(c == 0)
    def _():
        acc_ref[...] = jnp.zeros_like(acc_ref)

    at_ref[...] = lax.dot_general(u_ref[...], xn_ref[...], (((1,), (1,)), ((), ())),
                                  preferred_element_type=F32)
    for j in range(ec // nk):
        i1 = c * (ec // nk) + j
        for tl in range(nsub):
            gate = jnp.zeros((nk, LANES), F32)
            for h in range(PEER_HEADS):
                s1row = st_ref[tl, 2 * h, pl.ds(i1, 1), :]
                g1row = e1_ref[tl, h, pl.ds(i1, 1), :]
                ssum = s1row + st_ref[tl, 2 * h + 1]
                gate = gate + jnp.where(ssum >= tau_ref[tl, h:h + 1, :], g1row * e2_ref[tl, h], 0.0)
            act = at_ref[j * nk:(j + 1) * nk, tl * LANES:(tl + 1) * LANES]
            wt_ref[j * nk:(j + 1) * nk, tl * LANES:(tl + 1) * LANES] = (jax.nn.gelu(act) * gate).astype(BF16)
    acc_ref[...] += jnp.dot(vt_ref[...], wt_ref[...], preferred_element_type=F32)

    @pl.when(c == pl.num_programs(1) - 1)
    def _():
        out_ref[...] = acc_ref[...].T


def _peer(xn, u, vt, st, tau, e1, e2):
    n, d = xn.shape
    ne = u.shape[0]
    tb = _token_block(n)
    nsub = tb // LANES
    _, hc, nk, _ = st.shape
    heads = hc // 2
    ec = 512
    assert ne % ec == 0 and ec % nk == 0 and ne == nk * nk
    return pl.pallas_call(
        _peer_kernel,
        grid=(n // tb, ne // ec),
        in_specs=[
            pl.BlockSpec((tb, d), lambda i, c: (i, 0)),
            pl.BlockSpec((ec, d), lambda i, c: (c, 0)),
            pl.BlockSpec((d, ec), lambda i, c: (0, c)),
            pl.BlockSpec((nsub, hc, nk, LANES), lambda i, c: (i, 0, 0, 0)),
            pl.BlockSpec((nsub, heads, LANES), lambda i, c: (i, 0, 0)),
            pl.BlockSpec((nsub, heads, nk, LANES), lambda i, c: (i, 0, 0, 0)),
            pl.BlockSpec((nsub, heads, nk, LANES), lambda i, c: (i, 0, 0, 0)),
        ],
        out_specs=pl.BlockSpec((tb, d), lambda i, c: (i, 0)),
        out_shape=jax.ShapeDtypeStruct((n, d), F32),
        scratch_shapes=[
            pltpu.VMEM((ec, tb), F32),
            pltpu.VMEM((ec, tb), BF16),
            pltpu.VMEM((d, tb), F32),
        ],
        compiler_params=_params("parallel", "arbitrary"),
        name="peer_mix",
    )(xn, u, vt, st, tau, e1, e2)


def _final_kernel(x1_ref, pe_ref, p_ref, g_ref, wg_ref, wp_ref, y_ref):
    x2 = x1_ref[...] + pe_ref[...]
    xn = _rms(x2, g_ref[...]).astype(BF16)
    gate = jax.nn.sigmoid(jnp.dot(xn, wg_ref[...], preferred_element_type=F32))
    y_ref[...] = x2 + gate * jnp.dot(p_ref[...].astype(BF16), wp_ref[...], preferred_element_type=F32)


def _final(x1, pe, p, g, wg, wp):
    n, d = x1.shape
    tb = _token_block(n)
    row = lambda i: (i, 0)
    const = lambda i: (0, 0)
    return pl.pallas_call(
        _final_kernel,
        grid=(n // tb,),
        in_specs=[
            pl.BlockSpec((tb, d), row),
            pl.BlockSpec((tb, d), row),
            pl.BlockSpec((tb, p.shape[1]), row),
            pl.BlockSpec((1, d), const),
            pl.BlockSpec(wg.shape, const),
            pl.BlockSpec(wp.shape, const),
        ],
        out_specs=pl.BlockSpec((tb, d), row),
        out_shape=jax.ShapeDtypeStruct((n, d), F32),
        compiler_params=_params("parallel"),
        name="ple_epilogue",
    )(x1, pe, p, g, wg, wp)


def _layer(xp, xs, pp, ps, past_k, past_v, lp):
    b, s, d = xp.shape
    bd, l, _ = xs.shape
    n_p = b * s
    kvw = N_KV_HEADS * HEAD_DIM
    sgw = d // 2
    gd = sgw // SGU_GROUPS
    x = jnp.concatenate([xp.reshape(n_p, d), xs.reshape(bd * l, d)], axis=0)
    p = jnp.concatenate([pp.reshape(n_p, -1), ps.reshape(bd * l, -1)], axis=0)

    q, k, v, u, vn, ga, gb = _inproj(
        x, lp['attn_norm_g'][None], lp['w_in'].astype(BF16),
        jnp.tile(lp['q_norm_g'], N_HEADS)[None], jnp.tile(lp['k_norm_g'], N_KV_HEADS)[None],
        lp['sgu_norm_g'][None], lp['sgu_norm_b'][None])

    split = lambda t: (t[:n_p].reshape(b, s, -1), t[n_p:])
    q_p, q_s = split(q)
    k_p, k_s = split(k)
    v_p, v_s = split(v)
    u_p, u_s = split(u)
    vn_p, vn_s = split(vn)

    sgu_w, sgu_b = lp['sgu_w'], lp['sgu_b']
    bias_p = jnp.repeat(sgu_b.T, gd, axis=1)
    a_p, m_p = _prompt_mix(lp['attn_sinks'], q_p, k_p, v_p, vn_p, u_p, sgu_w, bias_p)
    wexp = jnp.repeat(jnp.transpose(sgu_w[:, :l, :l], (2, 1, 0)), gd, axis=2)
    a_s, m_s = _sample_mix(lp['attn_sinks'], q_s, k_s, v_s,
                           past_k.reshape(bd, -1, kvw), past_v.reshape(bd, -1, kvw),
                           vn_s, u_s, wexp, bias_p[:l])
    a = jnp.concatenate([a_p.reshape(n_p, -1), a_s], axis=0)
    m = jnp.concatenate([m_p.reshape(n_p, -1), m_s], axis=0)

    keys = lp['peer_sub_keys'].reshape(2 * PEER_HEADS, PEER_N_KEYS, -1).astype(BF16)
    x1, xn1, st = _merge(x, a, m, ga, gb, lp['w_branch_a'].astype(BF16), lp['w_branch_b'].astype(BF16),
                         lp['w_out'].astype(BF16), lp['ffn_norm_g'][None], lp['peer_w_q'].astype(BF16), keys)
    tau, e1, e2 = _thresholds(st)
    pe = _peer(xn1, lp['peer_u'].astype(BF16), lp['peer_v'].astype(BF16).T, st, tau, e1, e2)
    y = _final(x1, pe, p, lp['ple_norm_g'][None], lp['w_ple_gate'].astype(BF16), lp['w_ple'].astype(BF16))

    wp = min(WINDOW, s)
    heads = lambda t, nb: t.reshape(nb, -1, N_KV_HEADS, HEAD_DIM)
    return (y[:n_p].reshape(b, s, d), y[n_p:].reshape(bd, l, d),
            heads(k_p[:, s - wp:], b), heads(v_p[:, s - wp:], b), heads(k_s, bd), heads(v_s, bd),
            vn_p[:, s - CHUNK:], vn_s.reshape(bd, l, sgw))


def kernel(x_prompt, x_sample, cache_k, cache_v, p_prompt, p_sample, attn_norm_g, w_in, q_norm_g, k_norm_g, attn_sinks, sgu_norm_g, sgu_norm_b, sgu_w, sgu_b, w_branch_a, w_branch_b, w_out, ffn_norm_g, peer_w_q, peer_sub_keys, peer_u, peer_v, ple_norm_g, w_ple, w_ple_gate):
    depth = w_in.shape[0]
    hp, hs = x_prompt, x_sample
    outs = [[] for _ in range(6)]
    for i in range(depth):
        lp = dict(attn_norm_g=attn_norm_g[i], w_in=w_in[i], q_norm_g=q_norm_g[i], k_norm_g=k_norm_g[i],
                  attn_sinks=attn_sinks[i], sgu_norm_g=sgu_norm_g[i], sgu_norm_b=sgu_norm_b[i],
                  sgu_w=sgu_w[i], sgu_b=sgu_b[i], w_branch_a=w_branch_a[i], w_branch_b=w_branch_b[i],
                  w_out=w_out[i], ffn_norm_g=ffn_norm_g[i], peer_w_q=peer_w_q[i],
                  peer_sub_keys=peer_sub_keys[i], peer_u=peer_u[i], peer_v=peer_v[i],
                  ple_norm_g=ple_norm_g[i], w_ple=w_ple[i], w_ple_gate=w_ple_gate[i])
        res = _layer(hp, hs, p_prompt[i], p_sample[i], cache_k[i], cache_v[i], lp)
        hp, hs = res[0], res[1]
        for lst, t in zip(outs, res[2:]):
            lst.append(t)
    return (hp, hs) + tuple(jnp.stack(o) for o in outs)
```

```python
import functools
import math

import jax
import jax.numpy as jnp
from jax import lax
from jax.experimental import pallas as pl
from jax.experimental.pallas import tpu as pltpu

F32 = jnp.float32
BF16 = jnp.bfloat16

N_HEADS = 8
N_KV_HEADS = 2
HEAD_DIM = 64
Q_GROUP = N_HEADS // N_KV_HEADS
WINDOW = 128
CHUNK = 128
SGU_GROUPS = 4
PEER_HEADS = 8
PEER_N_KEYS = 128
PEER_TOPK = 16
EPS = 1e-6
NEG_INF = -1e30
ALIBI_SLOPES = tuple(2.0 ** (-8.0 * h / N_HEADS) for h in range(1, N_HEADS + 1))

LANES = 128
SUBLANES = 8
VMEM_LIMIT = 56 * 1024 * 1024


def _params(*semantics, flags=None):
    return pltpu.CompilerParams(dimension_semantics=semantics, vmem_limit_bytes=VMEM_LIMIT, flags=flags)


def _token_block(n):
    for tb in (512, 256, 128):
        if n % tb == 0:
            return tb
    raise ValueError(f"token count {n} must be a multiple of 128")


def _rms(x, g):
    return x * lax.rsqrt(jnp.mean(x * x, axis=-1, keepdims=True) + EPS) * g


def _group_rms(t, ones_blk, g):
    t2 = t * t
    hi = t2.astype(BF16)
    lo = (t2 - hi.astype(F32)).astype(BF16)
    ss = (jnp.dot(hi, ones_blk, preferred_element_type=F32)
          + jnp.dot(lo, ones_blk, preferred_element_type=F32))
    return t * lax.rsqrt(ss * (1.0 / HEAD_DIM) + EPS) * g


def _inproj_kernel(x_ref, g_ref, w_ref, qg_ref, kg_ref, lg_ref, lb_ref, bq_ref, bk_ref,
                   q_ref, k_ref, v_ref, u_ref, vn_ref, ga_ref, gb_ref):
    x = x_ref[...]
    xn = _rms(x, g_ref[...])
    z = jnp.dot(xn.astype(BF16), w_ref[...], preferred_element_type=F32)
    att = N_HEADS * HEAD_DIM
    kvw = N_KV_HEADS * HEAD_DIM
    sgw = (z.shape[1] - att - 2 * kvw) // 6
    o = 0
    q = z[:, o:o + att]; o += att
    k = z[:, o:o + kvw]; o += kvw
    v = z[:, o:o + kvw]; o += kvw
    su = z[:, o:o + sgw]; o += sgw
    sv = z[:, o:o + sgw]; o += sgw
    g_a = z[:, o:o + 2 * sgw]; o += 2 * sgw
    g_b = z[:, o:o + 2 * sgw]
    qn = _group_rms(q, bq_ref[...], qg_ref[...])
    q_ref[...] = (qn * (HEAD_DIM ** -0.5)).astype(BF16)
    k_ref[...] = _group_rms(k, bk_ref[...], kg_ref[...])
    v_ref[...] = v
    u_ref[...] = jax.nn.gelu(su).astype(BF16)
    gv = jax.nn.gelu(sv)
    mu = jnp.mean(gv, axis=-1, keepdims=True)
    gc = gv - mu
    vn_ref[...] = gc * lax.rsqrt(jnp.mean(gc * gc, axis=-1, keepdims=True) + EPS) * lg_ref[...] + lb_ref[...]
    ga_ref[...] = jax.nn.sigmoid(g_a).astype(BF16)
    gb_ref[...] = jax.nn.sigmoid(g_b).astype(BF16)


def _inproj(x, g, w_in, qg, kg, lg, lb):
    n, d = x.shape
    tb = _token_block(n)
    att = N_HEADS * HEAD_DIM
    kvw = N_KV_HEADS * HEAD_DIM
    sgw = d // 2
    hid = jnp.arange(att) // HEAD_DIM
    bq = (hid[:, None] == hid[None, :]).astype(BF16)
    bk = bq[:kvw, :kvw]
    const = lambda i: (0, 0)
    row = lambda i: (i, 0)
    outs = [
        jax.ShapeDtypeStruct((n, att), BF16),
        jax.ShapeDtypeStruct((n, kvw), F32),
        jax.ShapeDtypeStruct((n, kvw), F32),
        jax.ShapeDtypeStruct((n, sgw), BF16),
        jax.ShapeDtypeStruct((n, sgw), F32),
        jax.ShapeDtypeStruct((n, d), BF16),
        jax.ShapeDtypeStruct((n, d), BF16),
    ]
    return pl.pallas_call(
        _inproj_kernel,
        grid=(n // tb,),
        in_specs=[
            pl.BlockSpec((tb, d), row),
            pl.BlockSpec((1, d), const),
            pl.BlockSpec(w_in.shape, const),
            pl.BlockSpec((1, att), const),
            pl.BlockSpec((1, kvw), const),
            pl.BlockSpec((1, sgw), const),
            pl.BlockSpec((1, sgw), const),
            pl.BlockSpec((att, att), const),
            pl.BlockSpec((kvw, kvw), const),
        ],
        out_specs=[pl.BlockSpec((tb, s.shape[1]), row) for s in outs],
        out_shape=outs,
        compiler_params=_params("parallel"),
        name="inproj",
    )(x, g, w_in, qg, kg, lg, lb, bq, bk)


def _sink_softmax(s, sink):
    mx = jnp.maximum(jnp.max(s, axis=-1, keepdims=True), sink)
    p = jnp.exp(s - mx)
    den = jnp.sum(p, axis=-1, keepdims=True) + jnp.exp(sink - mx)
    return p / den


def _prompt_kernel(sinks_ref, q_ref, kc_ref, kp_ref, vc_ref, vp_ref, vn_ref, u_ref, w_ref, bias_ref,
                   a_ref, m_ref):
    i = pl.program_id(1)
    tq = q_ref.shape[1]
    nblk = tq // WINDOW
    q = q_ref[0]
    kc = kc_ref[0].astype(BF16)
    vc = vc_ref[0].astype(BF16)
    kp = kp_ref[0].astype(BF16)
    vp = vp_ref[0].astype(BF16)
    row = lax.broadcasted_iota(jnp.int32, (WINDOW, 2 * WINDOW), 0)
    col = lax.broadcasted_iota(jnp.int32, (WINDOW, 2 * WINDOW), 1)
    dist = row - col + WINDOW
    in_window = (dist >= 0) & (dist < WINDOW)
    distf = dist.astype(F32)
    for jq in range(nblk):
        r0 = jq * WINDOW
        if jq == 0:
            kprev, vprev = kp, vp
            valid = in_window & (col >= jnp.where(i > 0, 0, WINDOW))
        else:
            kprev, vprev = kc[r0 - WINDOW:r0], vc[r0 - WINDOW:r0]
            valid = in_window
        kcat = jnp.concatenate([kprev, kc[r0:r0 + WINDOW]], axis=0)
        vcat = jnp.concatenate([vprev, vc[r0:r0 + WINDOW]], axis=0)
        for g in range(N_KV_HEADS):
            heads = range(g * Q_GROUP, (g + 1) * Q_GROUP)
            qg = jnp.concatenate([q[r0:r0 + WINDOW, h * HEAD_DIM:(h + 1) * HEAD_DIM] for h in heads], axis=0)
            s_all = lax.dot_general(qg, kcat[:, g * HEAD_DIM:(g + 1) * HEAD_DIM],
                                    (((1,), (1,)), ((), ())), preferred_element_type=F32)
            probs = []
            for hl, h in enumerate(heads):
                s = s_all[hl * WINDOW:(hl + 1) * WINDOW] - ALIBI_SLOPES[h] * distf
                s = jnp.where(valid, s, NEG_INF)
                probs.append(_sink_softmax(s, sinks_ref[h]).astype(BF16))
            o_all = jnp.dot(jnp.concatenate(probs, axis=0), vcat[:, g * HEAD_DIM:(g + 1) * HEAD_DIM],
                            preferred_element_type=F32)
            for hl, h in enumerate(heads):
                a_ref[0, r0:r0 + WINDOW, h * HEAD_DIM:(h + 1) * HEAD_DIM] = (
                    o_all[hl * WINDOW:(hl + 1) * WINDOW].astype(BF16))
    tr = lax.broadcasted_iota(jnp.int32, (CHUNK, CHUNK), 0)
    tc = lax.broadcasted_iota(jnp.int32, (CHUNK, CHUNK), 1)
    gd = vn_ref.shape[2] // SGU_GROUPS
    wm = [jnp.where(tr >= tc, w_ref[g], 0.0).astype(BF16) for g in range(SGU_GROUPS)]
    for c in range(tq // CHUNK):
        r0 = c * CHUNK
        vnc = vn_ref[0, r0:r0 + CHUNK, :].astype(BF16)
        for g in range(SGU_GROUPS):
            s = jnp.dot(wm[g], vnc[:, g * gd:(g + 1) * gd], preferred_element_type=F32)
            s = s + bias_ref[:, g * gd:(g + 1) * gd]
            m_ref[0, r0:r0 + CHUNK, g * gd:(g + 1) * gd] = (
                u_ref[0, r0:r0 + CHUNK, g * gd:(g + 1) * gd].astype(F32) * s).astype(BF16)


def _prompt_mix(sinks, q, k, v, vn, u, sgu_w, sgu_bias):
    b, s, att = q.shape
    kvw = k.shape[2]
    sgw = vn.shape[2]
    tq = 512 if s % 512 == 0 else WINDOW
    assert s % tq == 0 and tq % WINDOW == 0 and WINDOW == CHUNK
    r = tq // WINDOW
    cur = lambda bi, i: (bi, i, 0)
    prev = lambda bi, i: (bi, jnp.maximum(i * r - 1, 0), 0)
    outs = [jax.ShapeDtypeStruct((b, s, att), BF16), jax.ShapeDtypeStruct((b, s, sgw), BF16)]
    return pl.pallas_call(
        _prompt_kernel,
        grid=(b, s // tq),
        in_specs=[
            pl.BlockSpec(memory_space=pltpu.SMEM),
            pl.BlockSpec((1, tq, att), cur),
            pl.BlockSpec((1, tq, kvw), cur),
            pl.BlockSpec((1, WINDOW, kvw), prev),
            pl.BlockSpec((1, tq, kvw), cur),
            pl.BlockSpec((1, WINDOW, kvw), prev),
            pl.BlockSpec((1, tq, sgw), cur),
            pl.BlockSpec((1, tq, sgw), cur),
            pl.BlockSpec(sgu_w.shape, lambda bi, i: (0, 0, 0)),
            pl.BlockSpec(sgu_bias.shape, lambda bi, i: (0, 0)),
        ],
        out_specs=[pl.BlockSpec((1, tq, att), cur), pl.BlockSpec((1, tq, sgw), cur)],
        out_shape=outs,
        compiler_params=_params("parallel", "parallel"),
        name="prompt_mix",
    )(sinks, q, k, k, v, v, vn, u, sgu_w, sgu_bias)


def _sample_kernel(sinks_ref, q_ref, kn_ref, vn_new_ref, ck_ref, cv_ref, vn_ref, u_ref, wexp_ref, bias_ref,
                   a_ref, m_ref):
    bb, w, _ = ck_ref.shape
    l = q_ref.shape[0] // bb
    per_seq = lambda ref: ref[...].astype(F32).reshape(bb, l, ref.shape[1])
    q = per_seq(q_ref)
    kcat = jnp.concatenate([ck_ref[...], per_seq(kn_ref)], axis=1).astype(BF16)
    vcat = jnp.concatenate([cv_ref[...], per_seq(vn_new_ref)], axis=1).astype(BF16)
    rows = Q_GROUP * l
    t = lax.broadcasted_iota(jnp.int32, (rows, w + l), 0) % l
    key = lax.broadcasted_iota(jnp.int32, (rows, w + l), 1)
    dist = t - (key - w)
    valid = (dist >= 0) & (dist < WINDOW)
    distf = dist.astype(F32)
    hl_of_row = lax.broadcasted_iota(jnp.int32, (rows, 1), 0) // l
    for g in range(N_KV_HEADS):
        heads = range(g * Q_GROUP, (g + 1) * Q_GROUP)
        qg = jnp.concatenate([q[:, :, h * HEAD_DIM:(h + 1) * HEAD_DIM] for h in heads], axis=1)
        s = jnp.einsum('bqd,bkd->bqk', qg.astype(BF16), kcat[:, :, g * HEAD_DIM:(g + 1) * HEAD_DIM],
                       preferred_element_type=F32)
        slope = jnp.zeros((rows, 1), F32)
        sink = jnp.zeros((rows, 1), F32)
        for hl, h in enumerate(heads):
            slope = jnp.where(hl_of_row == hl, ALIBI_SLOPES[h], slope)
            sink = jnp.where(hl_of_row == hl, sinks_ref[h], sink)
        s = jnp.where(valid[None], s - (slope * distf)[None], NEG_INF)
        p = _sink_softmax(s, sink[None]).astype(BF16)
        o = jnp.einsum('bqk,bkd->bqd', p, vcat[:, :, g * HEAD_DIM:(g + 1) * HEAD_DIM],
                       preferred_element_type=F32)
        for hl, h in enumerate(heads):
            a_ref[:, h * HEAD_DIM:(h + 1) * HEAD_DIM] = (
                o[:, hl * l:(hl + 1) * l, :].reshape(bb * l, HEAD_DIM).astype(BF16))
    vn = per_seq(vn_ref)
    tt = lax.broadcasted_iota(jnp.int32, (l, vn.shape[2]), 0)
    s = jnp.broadcast_to(bias_ref[...][None], vn.shape)
    for sp in range(l):
        wm = jnp.where(tt >= sp, wexp_ref[sp], 0.0)
        s = s + wm[None] * vn[:, sp:sp + 1, :]
    m_ref[...] = (per_seq(u_ref) * s).reshape(bb * l, vn.shape[2]).astype(BF16)


def _sample_mix(sinks, q, k, v, cache_k, cache_v, vn, u, wexp, bias):
    b, w, kvw = cache_k.shape
    l = q.shape[0] // b
    att = q.shape[1]
    sgw = vn.shape[1]
    bb = 16 if b % 16 == 0 else b
    tok = lambda width: pl.BlockSpec((bb * l, width), lambda i: (i, 0))
    past = pl.BlockSpec((bb, w, kvw), lambda i: (i, 0, 0))
    outs = [jax.ShapeDtypeStruct((b * l, att), BF16), jax.ShapeDtypeStruct((b * l, sgw), BF16)]
    return pl.pallas_call(
        _sample_kernel,
        grid=(b // bb,),
        in_specs=[
            pl.BlockSpec(memory_space=pltpu.SMEM),
            tok(att), tok(kvw), tok(kvw), past, past, tok(sgw), tok(sgw),
            pl.BlockSpec(wexp.shape, lambda i: (0, 0, 0)),
            pl.BlockSpec(bias.shape, lambda i: (0, 0)),
        ],
        out_specs=[tok(att), tok(sgw)],
        out_shape=outs,
        compiler_params=_params("parallel"),
        name="sample_mix",
    )(sinks, q, k, v, cache_k, cache_v, vn, u, wexp, bias)


def _merge_kernel(x_ref, a_ref, m_ref, ga_ref, gb_ref, wa_ref, wb_ref, wo_ref, fg_ref, wq_ref, keys_ref,
                  x1_ref, xn_ref, s1_ref, s2_ref):
    ha = jnp.dot(a_ref[...], wa_ref[...], preferred_element_type=F32)
    hb = jnp.dot(m_ref[...], wb_ref[...], preferred_element_type=F32)
    h = ga_ref[...].astype(F32) * ha + gb_ref[...].astype(F32) * hb
    x1 = x_ref[...] + jnp.dot(h.astype(BF16), wo_ref[...], preferred_element_type=F32)
    x1_ref[...] = x1
    xn = _rms(x1, fg_ref[...]).astype(BF16)
    xn_ref[...] = xn
    qp = jnp.dot(xn, wq_ref[...], preferred_element_type=F32).astype(BF16)
    half = keys_ref.shape[2]
    nsub = s1_ref.shape[0]
    for hc in range(keys_ref.shape[0]):
        st = lax.dot_general(keys_ref[hc], qp[:, hc * half:(hc + 1) * half],
                             (((1,), (1,)), ((), ())), preferred_element_type=F32)
        dst = s1_ref if hc % 2 == 0 else s2_ref
        for tl in range(nsub):
            dst[tl, hc // 2] = st[:, tl * LANES:(tl + 1) * LANES]


def _merge(x, a, m, ga, gb, wa, wb, wo, fg, wq, keys):
    n, d = x.shape
    tb = _token_block(n)
    nsub = tb // LANES
    hc, nk, half = keys.shape
    row = lambda i: (i, 0)
    const2 = lambda i: (0, 0)
    heads = hc // 2
    outs = [
        jax.ShapeDtypeStruct((n, d), F32),
        jax.ShapeDtypeStruct((n, d), BF16),
        jax.ShapeDtypeStruct((n // LANES, heads, nk, LANES), F32),
        jax.ShapeDtypeStruct((n // LANES, heads, nk, LANES), F32),
    ]
    return pl.pallas_call(
        _merge_kernel,
        grid=(n // tb,),
        in_specs=[
            pl.BlockSpec((tb, d), row),
            pl.BlockSpec((tb, a.shape[1]), row),
            pl.BlockSpec((tb, m.shape[1]), row),
            pl.BlockSpec((tb, d), row),
            pl.BlockSpec((tb, d), row),
            pl.BlockSpec(wa.shape, const2),
            pl.BlockSpec(wb.shape, const2),
            pl.BlockSpec(wo.shape, const2),
            pl.BlockSpec((1, d), const2),
            pl.BlockSpec(wq.shape, const2),
            pl.BlockSpec(keys.shape, lambda i: (0, 0, 0)),
        ],
        out_specs=[
            pl.BlockSpec((tb, d), row),
            pl.BlockSpec((tb, d), row),
            pl.BlockSpec((nsub, heads, nk, LANES), lambda i: (i, 0, 0, 0)),
            pl.BlockSpec((nsub, heads, nk, LANES), lambda i: (i, 0, 0, 0)),
        ],
        out_shape=outs,
        compiler_params=_params("parallel"),
        name="merge",
    )(x, a, m, ga, gb, wa, wb, wo, fg, wq, keys)


def _oddeven_merge(lo, hi, r):
    step = r * 2
    if step < hi - lo:
        yield from _oddeven_merge(lo, hi, step)
        yield from _oddeven_merge(lo + r, hi, step)
        yield from [(i, i + r) for i in range(lo + r, hi - r, step)]
    else:
        yield (lo, lo + r)


def _oddeven_merge_sort(lo, hi):
    if hi - lo >= 1:
        mid = lo + (hi - lo) // 2
        yield from _oddeven_merge_sort(lo, mid)
        yield from _oddeven_merge_sort(mid + 1, hi)
        yield from _oddeven_merge(lo, hi, 1)


_SORT_TOPK = tuple(_oddeven_merge_sort(0, PEER_TOPK - 1))


def _cmpx(w, i, j):
    a, b = w[i], w[j]
    if b is None:
        return
    if a is None:
        w[i], w[j] = b, None
        return
    w[i], w[j] = jnp.maximum(a, b), jnp.minimum(a, b)


def _top_values(w):
    k = PEER_TOPK
    w = list(w)
    for i, j in _SORT_TOPK:
        _cmpx(w, i, j)
    shift = SUBLANES // 2
    while shift >= 1:
        y = [None if v is None else pltpu.roll(v, shift, 0) for v in w]
        z = []
        for r in range(k):
            a, b = w[r], y[k - 1 - r]
            z.append(b if a is None else a if b is None else jnp.maximum(a, b))
        stride = k // 2
        while stride >= 1:
            for i in range(k):
                if i & stride == 0:
                    _cmpx(z, i, i + stride)
            stride //= 2
        w = z
        shift //= 2
    return w


def _thresh_kernel(s1_ref, s2_ref, th_ref, g1_ref, g2_ref):
    k = PEER_TOPK
    nk, lanes = s1_ref.shape[2], s1_ref.shape[3]
    nslot = nk // SUBLANES
    assert nslot == k and k == 2 * SUBLANES
    sub = lax.broadcasted_iota(jnp.int32, (SUBLANES, lanes), 0)

    def pack(vals):
        out = vals[0]
        for j in range(1, SUBLANES):
            out = jnp.where(sub == j, vals[j], out)
        return out

    def head(h, carry):
        w1 = [s1_ref[0, h, r * SUBLANES:(r + 1) * SUBLANES, :] for r in range(nslot)]
        w2 = [s2_ref[0, h, r * SUBLANES:(r + 1) * SUBLANES, :] for r in range(nslot)]
        a = _top_values(w1)
        b = _top_values(w2)
        b_lo, b_hi, a_hi = pack(b[:SUBLANES]), pack(b[SUBLANES:]), pack(a[SUBLANES:])
        cands = ([a[0] + b_lo, a[0] + b_hi] + [a[i] + b_lo for i in range(1, SUBLANES)] + [a_hi + b[0]])
        best = _top_values(cands + [None] * (k - len(cands)))
        tau = best[k - 1]
        z = jnp.ones_like(tau)
        for r in range(1, k):
            z = z + jnp.exp(best[r] - best[0])
        inv_z = 1.0 / z
        theta = []
        for i in range(k):
            t = jnp.full_like(tau, jnp.inf)
            for j in range(k // (i + 1)):
                t = jnp.where(a[i] + b[j] >= tau, b[j], t)
            theta.append(t)
        for r in range(nslot):
            th = jnp.full_like(tau, jnp.inf)
            for i in range(k - 1, -1, -1):
                th = jnp.where(w1[r] >= a[i], theta[i], th)
            rows = pl.ds(r * SUBLANES, SUBLANES)
            th_ref[0, h, rows, :] = th
            g1_ref[0, h, rows, :] = jnp.exp(w1[r] - a[0]) * inv_z
            g2_ref[0, h, rows, :] = jnp.exp(w2[r] - b[0])
        return carry

    lax.fori_loop(0, s1_ref.shape[1], head, 0)


def _thresholds(s1, s2):
    nsub, heads, nk, lanes = s1.shape
    spec = pl.BlockSpec((1, heads, nk, lanes), lambda i: (i, 0, 0, 0))
    out = jax.ShapeDtypeStruct(s1.shape, F32)
    return pl.pallas_call(
        _thresh_kernel,
        grid=(nsub,),
        in_specs=[spec, spec],
        out_specs=[spec, spec, spec],
        out_shape=[out, out, out],
        compiler_params=_params("parallel"),
        name="peer_thresholds",
    )(s1, s2)


GATE_ROWS = 64
MXU_COLS = 256


def _gate_tasks(at_ref, wt_ref, i1_base, s2_ref, th_ref, g1_ref, g2_ref):
    nsub, heads, nk, lanes = s2_ref.shape

    def tile(j, tl, r0):
        i1 = i1_base + j
        gate = jnp.zeros((GATE_ROWS, lanes), F32)
        for h in range(heads):
            th_row = th_ref[tl, h, pl.ds(i1, 1), :]
            g1_row = g1_ref[tl, h, pl.ds(i1, 1), :]
            sel = s2_ref[tl, h, r0:r0 + GATE_ROWS, :] >= th_row
            gate = gate + jnp.where(sel, g1_row * g2_ref[tl, h, r0:r0 + GATE_ROWS, :], 0.0)
        rows = slice(j * nk + r0, j * nk + r0 + GATE_ROWS)
        cols = slice(tl * lanes, (tl + 1) * lanes)
        wt_ref[rows, cols] = (jax.nn.gelu(at_ref[rows, cols]) * gate).astype(BF16)

    return [functools.partial(tile, j, tl, r0)
            for tl in range(nsub) for j in range(at_ref.shape[0] // nk) for r0 in range(0, nk, GATE_ROWS)]


def _interleave(main, *others):
    slots = [[] for _ in range(len(main) + 1)]
    for oi, tasks in enumerate(others):
        for k, t in enumerate(tasks):
            slots[((k * len(others) + oi) * len(main)) // (len(tasks) * len(others))].append(t)
    order = []
    for k, m in enumerate(main):
        order.extend(slots[k])
        order.append(m)
    return order


def _peer_kernel(xn_ref, u0_ref, ub_ref, ua_ref, vp_ref, va_ref, s2_ref, th_ref, g1_ref, g2_ref, out_ref,
                 at0, at1, wt0, wt1, acc):
    s = pl.program_id(1)
    last = pl.num_programs(1) - 1
    per = at0.shape[0] // s2_ref.shape[2]
    tb = xn_ref.shape[0]
    tables = (s2_ref, th_ref, g1_ref, g2_ref)
    nt = (((1,), (1,)), ((), ()))
    col_tiles = [slice(c, c + min(MXU_COLS, tb)) for c in range(0, tb, min(MXU_COLS, tb))]

    def act_tasks(u_ref, at):
        def piece(cs):
            at[:, cs] = lax.dot_general(u_ref[...], xn_ref[cs, :], nt, preferred_element_type=F32)
        return [functools.partial(piece, cs) for cs in col_tiles]

    def mix_tasks(v_ref, wt):
        def piece(cs):
            acc[:, cs] += jnp.dot(v_ref[...], wt[:, cs], preferred_element_type=F32)
        return [functools.partial(piece, cs) for cs in col_tiles]

    @pl.when(s == 0)
    def _():
        for t in act_tasks(u0_ref, at0):
            t()
        wt1[...] = jnp.zeros_like(wt1)
        acc[...] = jnp.zeros_like(acc)

    @pl.when(s < last)
    def _():
        for t in _interleave(_gate_tasks(at0, wt0, 2 * s * per, *tables),
                             act_tasks(ub_ref, at1), mix_tasks(vp_ref, wt1)):
            t()
        for t in _interleave(_gate_tasks(at1, wt1, (2 * s + 1) * per, *tables),
                             act_tasks(ua_ref, at0), mix_tasks(va_ref, wt0)):
            t()

    @pl.when(s == last)
    def _():
        for t in mix_tasks(vp_ref, wt1):
            t()
        out_ref[...] = acc[...].T


def _peer(xn, u, vt, s2, th, g1, g2):
    n, d = xn.shape
    ne = u.shape[0]
    tb = _token_block(n)
    nsub = tb // LANES
    _, heads, nk, _ = s2.shape
    ec = 512
    assert ne % (2 * ec) == 0 and ec % nk == 0 and ne == nk * nk
    nc = ne // ec
    table = pl.BlockSpec((nsub, heads, nk, LANES), lambda i, s: (i, 0, 0, 0))
    return pl.pallas_call(
        _peer_kernel,
        grid=(n // tb, nc // 2 + 1),
        in_specs=[
            pl.BlockSpec((tb, d), lambda i, s: (i, 0)),
            pl.BlockSpec((ec, d), lambda i, s: (0, 0)),
            pl.BlockSpec((ec, d), lambda i, s: (jnp.minimum(2 * s + 1, nc - 1), 0)),
            pl.BlockSpec((ec, d), lambda i, s: (jnp.minimum(2 * s + 2, nc - 2), 0)),
            pl.BlockSpec((d, ec), lambda i, s: (0, jnp.maximum(2 * s - 1, 0))),
            pl.BlockSpec((d, ec), lambda i, s: (0, jnp.minimum(2 * s, nc - 2))),
            table, table, table, table,
        ],
        out_specs=pl.BlockSpec((tb, d), lambda i, s: (i, 0)),
        out_shape=jax.ShapeDtypeStruct((n, d), F32),
        scratch_shapes=[
            pltpu.VMEM((ec, tb), F32),
            pltpu.VMEM((ec, tb), F32),
            pltpu.VMEM((ec, tb), BF16),
            pltpu.VMEM((ec, tb), BF16),
            pltpu.VMEM((d, tb), F32),
        ],
        compiler_params=_params("parallel", "arbitrary"),
        name="peer_mix",
    )(xn, u, u, u, vt, vt, s2, th, g1, g2)


def _final_kernel(x1_ref, pe_ref, p_ref, g_ref, wg_ref, wp_ref, y_ref):
    x2 = x1_ref[...] + pe_ref[...]
    xn = _rms(x2, g_ref[...]).astype(BF16)
    gate = jax.nn.sigmoid(jnp.dot(xn, wg_ref[...], preferred_element_type=F32))
    y_ref[...] = x2 + gate * jnp.dot(p_ref[...].astype(BF16), wp_ref[...], preferred_element_type=F32)


def _final(x1, pe, p, g, wg, wp):
    n, d = x1.shape
    tb = _token_block(n)
    row = lambda i: (i, 0)
    const = lambda i: (0, 0)
    return pl.pallas_call(
        _final_kernel,
        grid=(n // tb,),
        in_specs=[
            pl.BlockSpec((tb, d), row),
            pl.BlockSpec((tb, d), row),
            pl.BlockSpec((tb, p.shape[1]), row),
            pl.BlockSpec((1, d), const),
            pl.BlockSpec(wg.shape, const),
            pl.BlockSpec(wp.shape, const),
        ],
        out_specs=pl.BlockSpec((tb, d), row),
        out_shape=jax.ShapeDtypeStruct((n, d), F32),
        compiler_params=_params("parallel"),
        name="ple_epilogue",
    )(x1, pe, p, g, wg, wp)


def _layer(xp, xs, pp, ps, past_k, past_v, lp):
    b, s, d = xp.shape
    bd, l, _ = xs.shape
    n_p = b * s
    kvw = N_KV_HEADS * HEAD_DIM
    sgw = d // 2
    gd = sgw // SGU_GROUPS
    x = jnp.concatenate([xp.reshape(n_p, d), xs.reshape(bd * l, d)], axis=0)
    p = jnp.concatenate([pp.reshape(n_p, -1), ps.reshape(bd * l, -1)], axis=0)

    q, k, v, u, vn, ga, gb = _inproj(
        x, lp['attn_norm_g'][None], lp['w_in'].astype(BF16),
        jnp.tile(lp['q_norm_g'], N_HEADS)[None], jnp.tile(lp['k_norm_g'], N_KV_HEADS)[None],
        lp['sgu_norm_g'][None], lp['sgu_norm_b'][None])

    split = lambda t: (t[:n_p].reshape(b, s, -1), t[n_p:])
    q_p, q_s = split(q)
    k_p, k_s = split(k)
    v_p, v_s = split(v)
    u_p, u_s = split(u)
    vn_p, vn_s = split(vn)

    sgu_w, sgu_b = lp['sgu_w'], lp['sgu_b']
    bias_p = jnp.repeat(sgu_b.T, gd, axis=1)
    a_p, m_p = _prompt_mix(lp['attn_sinks'], q_p, k_p, v_p, vn_p, u_p, sgu_w, bias_p)
    wexp = jnp.repeat(jnp.transpose(sgu_w[:, :l, :l], (2, 1, 0)), gd, axis=2)
    a_s, m_s = _sample_mix(lp['attn_sinks'], q_s, k_s, v_s,
                           past_k.reshape(bd, -1, kvw), past_v.reshape(bd, -1, kvw),
                           vn_s, u_s, wexp, bias_p[:l])
    a = jnp.concatenate([a_p.reshape(n_p, -1), a_s], axis=0)
    m = jnp.concatenate([m_p.reshape(n_p, -1), m_s], axis=0)

    keys = lp['peer_sub_keys'].reshape(2 * PEER_HEADS, PEER_N_KEYS, -1).astype(BF16)
    x1, xn1, s1, s2 = _merge(x, a, m, ga, gb, lp['w_branch_a'].astype(BF16), lp['w_branch_b'].astype(BF16),
                             lp['w_out'].astype(BF16), lp['ffn_norm_g'][None], lp['peer_w_q'].astype(BF16), keys)
    th, g1, g2 = _thresholds(s1, s2)
    pe = _peer(xn1, lp['peer_u'].astype(BF16), lp['peer_v'].astype(BF16).T, s2, th, g1, g2)
    y = _final(x1, pe, p, lp['ple_norm_g'][None], lp['w_ple_gate'].astype(BF16), lp['w_ple'].astype(BF16))

    wp = min(WINDOW, s)
    heads = lambda t, nb: t.reshape(nb, -1, N_KV_HEADS, HEAD_DIM)
    return (y[:n_p].reshape(b, s, d), y[n_p:].reshape(bd, l, d),
            heads(k_p[:, s - wp:], b), heads(v_p[:, s - wp:], b), heads(k_s, bd), heads(v_s, bd),
            vn_p[:, s - CHUNK:], vn_s.reshape(bd, l, sgw))


def kernel(x_prompt, x_sample, cache_k, cache_v, p_prompt, p_sample, attn_norm_g, w_in, q_norm_g, k_norm_g, attn_sinks, sgu_norm_g, sgu_norm_b, sgu_w, sgu_b, w_branch_a, w_branch_b, w_out, ffn_norm_g, peer_w_q, peer_sub_keys, peer_u, peer_v, ple_norm_g, w_ple, w_ple_gate):
    depth = w_in.shape[0]
    hp, hs = x_prompt, x_sample
    outs = [[] for _ in range(6)]
    for i in range(depth):
        lp = dict(attn_norm_g=attn_norm_g[i], w_in=w_in[i], q_norm_g=q_norm_g[i], k_norm_g=k_norm_g[i],
                  attn_sinks=attn_sinks[i], sgu_norm_g=sgu_norm_g[i], sgu_norm_b=sgu_norm_b[i],
                  sgu_w=sgu_w[i], sgu_b=sgu_b[i], w_branch_a=w_branch_a[i], w_branch_b=w_branch_b[i],
                  w_out=w_out[i], ffn_norm_g=ffn_norm_g[i], peer_w_q=peer_w_q[i],
                  peer_sub_keys=peer_sub_keys[i], peer_u=peer_u[i], peer_v=peer_v[i],
                  ple_norm_g=ple_norm_g[i], w_ple=w_ple[i], w_ple_gate=w_ple_gate[i])
        res = _layer(hp, hs, p_prompt[i], p_sample[i], cache_k[i], cache_v[i], lp)
        hp, hs = res[0], res[1]
        for lst, t in zip(outs, res[2:]):
            lst.append(t)
    return (hp, hs) + tuple(jnp.stack(o) for o in outs)
```

```python
import functools
import math

import jax
import jax.numpy as jnp
from jax import lax
from jax.experimental import pallas as pl
from jax.experimental.pallas import tpu as pltpu

F32 = jnp.float32
BF16 = jnp.bfloat16

N_HEADS = 8
N_KV_HEADS = 2
HEAD_DIM = 64
Q_GROUP = N_HEADS // N_KV_HEADS
WINDOW = 128
CHUNK = 128
SGU_GROUPS = 4
PEER_HEADS = 8
PEER_N_KEYS = 128
PEER_TOPK = 16
EPS = 1e-6
NEG_INF = -1e30
ALIBI_SLOPES = tuple(2.0 ** (-8.0 * h / N_HEADS) for h in range(1, N_HEADS + 1))

LANES = 128
SUBLANES = 8
VMEM_LIMIT = 56 * 1024 * 1024


def _params(*semantics, flags=None):
    return pltpu.CompilerParams(dimension_semantics=semantics, vmem_limit_bytes=VMEM_LIMIT, flags=flags)


def _token_block(n):
    for tb in (512, 256, 128):
        if n % tb == 0:
            return tb
    raise ValueError(f"token count {n} must be a multiple of 128")


def _rms(x, g):
    return x * lax.rsqrt(jnp.mean(x * x, axis=-1, keepdims=True) + EPS) * g


def _group_rms(t, ones_blk, g):
    t2 = t * t
    hi = t2.astype(BF16)
    lo = (t2 - hi.astype(F32)).astype(BF16)
    ss = (jnp.dot(hi, ones_blk, preferred_element_type=F32)
          + jnp.dot(lo, ones_blk, preferred_element_type=F32))
    return t * lax.rsqrt(ss * (1.0 / HEAD_DIM) + EPS) * g


def _inproj_kernel(x_ref, g_ref, w_ref, qg_ref, kg_ref, lg_ref, lb_ref, bq_ref, bk_ref,
                   q_ref, k_ref, v_ref, u_ref, vn_ref, ga_ref, gb_ref):
    x = x_ref[...]
    xn = _rms(x, g_ref[...])
    z = jnp.dot(xn.astype(BF16), w_ref[...], preferred_element_type=F32)
    att = N_HEADS * HEAD_DIM
    kvw = N_KV_HEADS * HEAD_DIM
    sgw = (z.shape[1] - att - 2 * kvw) // 6
    o = 0
    q = z[:, o:o + att]; o += att
    k = z[:, o:o + kvw]; o += kvw
    v = z[:, o:o + kvw]; o += kvw
    su = z[:, o:o + sgw]; o += sgw
    sv = z[:, o:o + sgw]; o += sgw
    g_a = z[:, o:o + 2 * sgw]; o += 2 * sgw
    g_b = z[:, o:o + 2 * sgw]
    qn = _group_rms(q, bq_ref[...], qg_ref[...])
    q_ref[...] = (qn * (HEAD_DIM ** -0.5)).astype(BF16)
    k_ref[...] = _group_rms(k, bk_ref[...], kg_ref[...])
    v_ref[...] = v
    u_ref[...] = jax.nn.gelu(su).astype(BF16)
    gv = jax.nn.gelu(sv)
    mu = jnp.mean(gv, axis=-1, keepdims=True)
    gc = gv - mu
    vn_ref[...] = gc * lax.rsqrt(jnp.mean(gc * gc, axis=-1, keepdims=True) + EPS) * lg_ref[...] + lb_ref[...]
    ga_ref[...] = jax.nn.sigmoid(g_a).astype(BF16)
    gb_ref[...] = jax.nn.sigmoid(g_b).astype(BF16)


def _inproj(x, g, w_in, qg, kg, lg, lb):
    n, d = x.shape
    tb = _token_block(n)
    att = N_HEADS * HEAD_DIM
    kvw = N_KV_HEADS * HEAD_DIM
    sgw = d // 2
    hid = jnp.arange(att) // HEAD_DIM
    bq = (hid[:, None] == hid[None, :]).astype(BF16)
    bk = bq[:kvw, :kvw]
    const = lambda i: (0, 0)
    row = lambda i: (i, 0)
    outs = [
        jax.ShapeDtypeStruct((n, att), BF16),
        jax.ShapeDtypeStruct((n, kvw), F32),
        jax.ShapeDtypeStruct((n, kvw), F32),
        jax.ShapeDtypeStruct((n, sgw), BF16),
        jax.ShapeDtypeStruct((n, sgw), F32),
        jax.ShapeDtypeStruct((n, d), BF16),
        jax.ShapeDtypeStruct((n, d), BF16),
    ]
    return pl.pallas_call(
        _inproj_kernel,
        grid=(n // tb,),
        in_specs=[
            pl.BlockSpec((tb, d), row),
            pl.BlockSpec((1, d), const),
            pl.BlockSpec(w_in.shape, const),
            pl.BlockSpec((1, att), const),
            pl.BlockSpec((1, kvw), const),
            pl.BlockSpec((1, sgw), const),
            pl.BlockSpec((1, sgw), const),
            pl.BlockSpec((att, att), const),
            pl.BlockSpec((kvw, kvw), const),
        ],
        out_specs=[pl.BlockSpec((tb, s.shape[1]), row) for s in outs],
        out_shape=outs,
        compiler_params=_params("parallel"),
        name="inproj",
    )(x, g, w_in, qg, kg, lg, lb, bq, bk)


def _sink_softmax(s, sink):
    mx = jnp.maximum(jnp.max(s, axis=-1, keepdims=True), sink)
    p = jnp.exp(s - mx)
    den = jnp.sum(p, axis=-1, keepdims=True) + jnp.exp(sink - mx)
    return p / den


def _prompt_kernel(sinks_ref, q_ref, kc_ref, kp_ref, vc_ref, vp_ref, vn_ref, u_ref, w_ref, bias_ref,
                   a_ref, m_ref):
    i = pl.program_id(1)
    tq = q_ref.shape[1]
    nblk = tq // WINDOW
    q = q_ref[0]
    kc = kc_ref[0].astype(BF16)
    vc = vc_ref[0].astype(BF16)
    kp = kp_ref[0].astype(BF16)
    vp = vp_ref[0].astype(BF16)
    row = lax.broadcasted_iota(jnp.int32, (WINDOW, 2 * WINDOW), 0)
    col = lax.broadcasted_iota(jnp.int32, (WINDOW, 2 * WINDOW), 1)
    dist = row - col + WINDOW
    in_window = (dist >= 0) & (dist < WINDOW)
    distf = dist.astype(F32)
    for jq in range(nblk):
        r0 = jq * WINDOW
        if jq == 0:
            kprev, vprev = kp, vp
            valid = in_window & (col >= jnp.where(i > 0, 0, WINDOW))
        else:
            kprev, vprev = kc[r0 - WINDOW:r0], vc[r0 - WINDOW:r0]
            valid = in_window
        kcat = jnp.concatenate([kprev, kc[r0:r0 + WINDOW]], axis=0)
        vcat = jnp.concatenate([vprev, vc[r0:r0 + WINDOW]], axis=0)
        for g in range(N_KV_HEADS):
            heads = range(g * Q_GROUP, (g + 1) * Q_GROUP)
            qg = jnp.concatenate([q[r0:r0 + WINDOW, h * HEAD_DIM:(h + 1) * HEAD_DIM] for h in heads], axis=0)
            s_all = lax.dot_general(qg, kcat[:, g * HEAD_DIM:(g + 1) * HEAD_DIM],
                                    (((1,), (1,)), ((), ())), preferred_element_type=F32)
            probs = []
            for hl, h in enumerate(heads):
                s = s_all[hl * WINDOW:(hl + 1) * WINDOW] - ALIBI_SLOPES[h] * distf
                s = jnp.where(valid, s, NEG_INF)
                probs.append(_sink_softmax(s, sinks_ref[h]).astype(BF16))
            o_all = jnp.dot(jnp.concatenate(probs, axis=0), vcat[:, g * HEAD_DIM:(g + 1) * HEAD_DIM],
                            preferred_element_type=F32)
            for hl, h in enumerate(heads):
                a_ref[0, r0:r0 + WINDOW, h * HEAD_DIM:(h + 1) * HEAD_DIM] = (
                    o_all[hl * WINDOW:(hl + 1) * WINDOW].astype(BF16))
    tr = lax.broadcasted_iota(jnp.int32, (CHUNK, CHUNK), 0)
    tc = lax.broadcasted_iota(jnp.int32, (CHUNK, CHUNK), 1)
    gd = vn_ref.shape[2] // SGU_GROUPS
    wm = [jnp.where(tr >= tc, w_ref[g], 0.0).astype(BF16) for g in range(SGU_GROUPS)]
    for c in range(tq // CHUNK):
        r0 = c * CHUNK
        vnc = vn_ref[0, r0:r0 + CHUNK, :].astype(BF16)
        for g in range(SGU_GROUPS):
            s = jnp.dot(wm[g], vnc[:, g * gd:(g + 1) * gd], preferred_element_type=F32)
            s = s + bias_ref[:, g * gd:(g + 1) * gd]
            m_ref[0, r0:r0 + CHUNK, g * gd:(g + 1) * gd] = (
                u_ref[0, r0:r0 + CHUNK, g * gd:(g + 1) * gd].astype(F32) * s).astype(BF16)


def _prompt_mix(sinks, q, k, v, vn, u, sgu_w, sgu_bias):
    b, s, att = q.shape
    kvw = k.shape[2]
    sgw = vn.shape[2]
    tq = 512 if s % 512 == 0 else WINDOW
    assert s % tq == 0 and tq % WINDOW == 0 and WINDOW == CHUNK
    r = tq // WINDOW
    cur = lambda bi, i: (bi, i, 0)
    prev = lambda bi, i: (bi, jnp.maximum(i * r - 1, 0), 0)
    outs = [jax.ShapeDtypeStruct((b, s, att), BF16), jax.ShapeDtypeStruct((b, s, sgw), BF16)]
    return pl.pallas_call(
        _prompt_kernel,
        grid=(b, s // tq),
        in_specs=[
            pl.BlockSpec(memory_space=pltpu.SMEM),
            pl.BlockSpec((1, tq, att), cur),
            pl.BlockSpec((1, tq, kvw), cur),
            pl.BlockSpec((1, WINDOW, kvw), prev),
            pl.BlockSpec((1, tq, kvw), cur),
            pl.BlockSpec((1, WINDOW, kvw), prev),
            pl.BlockSpec((1, tq, sgw), cur),
            pl.BlockSpec((1, tq, sgw), cur),
            pl.BlockSpec(sgu_w.shape, lambda bi, i: (0, 0, 0)),
            pl.BlockSpec(sgu_bias.shape, lambda bi, i: (0, 0)),
        ],
        out_specs=[pl.BlockSpec((1, tq, att), cur), pl.BlockSpec((1, tq, sgw), cur)],
        out_shape=outs,
        compiler_params=_params("parallel", "parallel"),
        name="prompt_mix",
    )(sinks, q, k, k, v, v, vn, u, sgu_w, sgu_bias)


def _sample_kernel(sinks_ref, q_ref, kn_ref, vn_new_ref, ck_ref, cv_ref, vn_ref, u_ref, wexp_ref, bias_ref,
                   a_ref, m_ref):
    bb, w, _ = ck_ref.shape
    l = q_ref.shape[0] // bb
    per_seq = lambda ref: ref[...].astype(F32).reshape(bb, l, ref.shape[1])
    q = per_seq(q_ref)
    kcat = jnp.concatenate([ck_ref[...], per_seq(kn_ref)], axis=1).astype(BF16)
    vcat = jnp.concatenate([cv_ref[...], per_seq(vn_new_ref)], axis=1).astype(BF16)
    rows = Q_GROUP * l
    t = lax.broadcasted_iota(jnp.int32, (rows, w + l), 0) % l
    key = lax.broadcasted_iota(jnp.int32, (rows, w + l), 1)
    dist = t - (key - w)
    valid = (dist >= 0) & (dist < WINDOW)
    distf = dist.astype(F32)
    hl_of_row = lax.broadcasted_iota(jnp.int32, (rows, 1), 0) // l
    for g in range(N_KV_HEADS):
        heads = range(g * Q_GROUP, (g + 1) * Q_GROUP)
        qg = jnp.concatenate([q[:, :, h * HEAD_DIM:(h + 1) * HEAD_DIM] for h in heads], axis=1)
        s = jnp.einsum('bqd,bkd->bqk', qg.astype(BF16), kcat[:, :, g * HEAD_DIM:(g + 1) * HEAD_DIM],
                       preferred_element_type=F32)
        slope = jnp.zeros((rows, 1), F32)
        sink = jnp.zeros((rows, 1), F32)
        for hl, h in enumerate(heads):
            slope = jnp.where(hl_of_row == hl, ALIBI_SLOPES[h], slope)
            sink = jnp.where(hl_of_row == hl, sinks_ref[h], sink)
        s = jnp.where(valid[None], s - (slope * distf)[None], NEG_INF)
        p = _sink_softmax(s, sink[None]).astype(BF16)
        o = jnp.einsum('bqk,bkd->bqd', p, vcat[:, :, g * HEAD_DIM:(g + 1) * HEAD_DIM],
                       preferred_element_type=F32)
        for hl, h in enumerate(heads):
            a_ref[:, h * HEAD_DIM:(h + 1) * HEAD_DIM] = (
                o[:, hl * l:(hl + 1) * l, :].reshape(bb * l, HEAD_DIM).astype(BF16))
    vn = per_seq(vn_ref)
    tt = lax.broadcasted_iota(jnp.int32, (l, vn.shape[2]), 0)
    s = jnp.broadcast_to(bias_ref[...][None], vn.shape)
    for sp in range(l):
        wm = jnp.where(tt >= sp, wexp_ref[sp], 0.0)
        s = s + wm[None] * vn[:, sp:sp + 1, :]
    m_ref[...] = (per_seq(u_ref) * s).reshape(bb * l, vn.shape[2]).astype(BF16)


def _sample_mix(sinks, q, k, v, cache_k, cache_v, vn, u, wexp, bias):
    b, w, kvw = cache_k.shape
    l = q.shape[0] // b
    att = q.shape[1]
    sgw = vn.shape[1]
    bb = 16 if b % 16 == 0 else b
    tok = lambda width: pl.BlockSpec((bb * l, width), lambda i: (i, 0))
    past = pl.BlockSpec((bb, w, kvw), lambda i: (i, 0, 0))
    outs = [jax.ShapeDtypeStruct((b * l, att), BF16), jax.ShapeDtypeStruct((b * l, sgw), BF16)]
    return pl.pallas_call(
        _sample_kernel,
        grid=(b // bb,),
        in_specs=[
            pl.BlockSpec(memory_space=pltpu.SMEM),
            tok(att), tok(kvw), tok(kvw), past, past, tok(sgw), tok(sgw),
            pl.BlockSpec(wexp.shape, lambda i: (0, 0, 0)),
            pl.BlockSpec(bias.shape, lambda i: (0, 0)),
        ],
        out_specs=[tok(att), tok(sgw)],
        out_shape=outs,
        compiler_params=_params("parallel"),
        name="sample_mix",
    )(sinks, q, k, v, cache_k, cache_v, vn, u, wexp, bias)


def _merge_kernel(x_ref, a_ref, m_ref, ga_ref, gb_ref, wa_ref, wb_ref, wo_ref, fg_ref, wq_ref, keys_ref,
                  x1_ref, xn_ref, s1_ref, s2_ref):
    ha = jnp.dot(a_ref[...], wa_ref[...], preferred_element_type=F32)
    hb = jnp.dot(m_ref[...], wb_ref[...], preferred_element_type=F32)
    h = ga_ref[...].astype(F32) * ha + gb_ref[...].astype(F32) * hb
    x1 = x_ref[...] + jnp.dot(h.astype(BF16), wo_ref[...], preferred_element_type=F32)
    x1_ref[...] = x1
    xn = _rms(x1, fg_ref[...]).astype(BF16)
    xn_ref[...] = xn
    qp = jnp.dot(xn, wq_ref[...], preferred_element_type=F32).astype(BF16)
    half = keys_ref.shape[2]
    nsub = s1_ref.shape[0]
    for hc in range(keys_ref.shape[0]):
        st = lax.dot_general(keys_ref[hc], qp[:, hc * half:(hc + 1) * half],
                             (((1,), (1,)), ((), ())), preferred_element_type=F32)
        dst = s1_ref if hc % 2 == 0 else s2_ref
        for tl in range(nsub):
            dst[tl, hc // 2] = st[:, tl * LANES:(tl + 1) * LANES]


def _merge(x, a, m, ga, gb, wa, wb, wo, fg, wq, keys):
    n, d = x.shape
    tb = _token_block(n)
    nsub = tb // LANES
    hc, nk, half = keys.shape
    row = lambda i: (i, 0)
    const2 = lambda i: (0, 0)
    heads = hc // 2
    outs = [
        jax.ShapeDtypeStruct((n, d), F32),
        jax.ShapeDtypeStruct((n, d), BF16),
        jax.ShapeDtypeStruct((n // LANES, heads, nk, LANES), F32),
        jax.ShapeDtypeStruct((n // LANES, heads, nk, LANES), F32),
    ]
    return pl.pallas_call(
        _merge_kernel,
        grid=(n // tb,),
        in_specs=[
            pl.BlockSpec((tb, d), row),
            pl.BlockSpec((tb, a.shape[1]), row),
            pl.BlockSpec((tb, m.shape[1]), row),
            pl.BlockSpec((tb, d), row),
            pl.BlockSpec((tb, d), row),
            pl.BlockSpec(wa.shape, const2),
            pl.BlockSpec(wb.shape, const2),
            pl.BlockSpec(wo.shape, const2),
            pl.BlockSpec((1, d), const2),
            pl.BlockSpec(wq.shape, const2),
            pl.BlockSpec(keys.shape, lambda i: (0, 0, 0)),
        ],
        out_specs=[
            pl.BlockSpec((tb, d), row),
            pl.BlockSpec((tb, d), row),
            pl.BlockSpec((nsub, heads, nk, LANES), lambda i: (i, 0, 0, 0)),
            pl.BlockSpec((nsub, heads, nk, LANES), lambda i: (i, 0, 0, 0)),
        ],
        out_shape=outs,
        compiler_params=_params("parallel"),
        name="merge",
    )(x, a, m, ga, gb, wa, wb, wo, fg, wq, keys)


def _oddeven_merge(lo, hi, r):
    step = r * 2
    if step < hi - lo:
        yield from _oddeven_merge(lo, hi, step)
        yield from _oddeven_merge(lo + r, hi, step)
        yield from [(i, i + r) for i in range(lo + r, hi - r, step)]
    else:
        yield (lo, lo + r)


def _oddeven_merge_sort(lo, hi):
    if hi - lo >= 1:
        mid = lo + (hi - lo) // 2
        yield from _oddeven_merge_sort(lo, mid)
        yield from _oddeven_merge_sort(mid + 1, hi)
        yield from _oddeven_merge(lo, hi, 1)


_SORT_TOPK = tuple(_oddeven_merge_sort(0, PEER_TOPK - 1))


def _cmpx(w, i, j):
    a, b = w[i], w[j]
    if b is None:
        return
    if a is None:
        w[i], w[j] = b, None
        return
    w[i], w[j] = jnp.maximum(a, b), jnp.minimum(a, b)


def _top_values(w):
    k = PEER_TOPK
    w = list(w)
    for i, j in _SORT_TOPK:
        _cmpx(w, i, j)
    shift = SUBLANES // 2
    while shift >= 1:
        y = [None if v is None else pltpu.roll(v, shift, 0) for v in w]
        z = []
        for r in range(k):
            a, b = w[r], y[k - 1 - r]
            z.append(b if a is None else a if b is None else jnp.maximum(a, b))
        stride = k // 2
        while stride >= 1:
            for i in range(k):
                if i & stride == 0:
                    _cmpx(z, i, i + stride)
            stride //= 2
        w = z
        shift //= 2
    return w


def _thresh_kernel(s1_ref, s2_ref, r2_ref, g2_ref, ns_ref, g1_ref):
    k = PEER_TOPK
    nk, lanes = s1_ref.shape[2], s1_ref.shape[3]
    nslot = nk // SUBLANES
    assert nslot == k and k == 2 * SUBLANES
    sub = lax.broadcasted_iota(jnp.int32, (SUBLANES, lanes), 0)

    def pack(vals):
        out = vals[0]
        for j in range(1, SUBLANES):
            out = jnp.where(sub == j, vals[j], out)
        return out

    def head(h, carry):
        w1 = [s1_ref[0, h, r * SUBLANES:(r + 1) * SUBLANES, :] for r in range(nslot)]
        w2 = [s2_ref[0, h, r * SUBLANES:(r + 1) * SUBLANES, :] for r in range(nslot)]
        a = _top_values(w1)
        b = _top_values(w2)
        b_lo, b_hi, a_hi = pack(b[:SUBLANES]), pack(b[SUBLANES:]), pack(a[SUBLANES:])
        cands = ([a[0] + b_lo, a[0] + b_hi] + [a[i] + b_lo for i in range(1, SUBLANES)] + [a_hi + b[0]])
        best = _top_values(cands + [None] * (k - len(cands)))
        tau = best[k - 1]
        z = jnp.ones_like(tau)
        for r in range(1, k):
            z = z + jnp.exp(best[r] - best[0])
        inv_z = 1.0 / z
        nsel = []
        for i in range(k):
            cnt = jnp.zeros_like(tau)
            for j in range(k // (i + 1)):
                cnt = jnp.where(a[i] + b[j] >= tau, float(j + 1), cnt)
            nsel.append(cnt)
        for r in range(nslot):
            ns = jnp.zeros_like(tau)
            rk = jnp.full_like(tau, float(k))
            for i in range(k - 1, -1, -1):
                ns = jnp.where(w1[r] >= a[i], nsel[i], ns)
                rk = jnp.where(w2[r] >= b[i], float(i), rk)
            rows = pl.ds(r * SUBLANES, SUBLANES)
            ns_ref[0, h, rows, :] = ns
            g1_ref[0, h, rows, :] = jnp.exp(w1[r] - a[0]) * (0.5 * inv_z)
            r2_ref[0, h, rows, :] = rk
            g2_ref[0, h, rows, :] = jnp.exp(w2[r] - b[0])
        return carry

    lax.fori_loop(0, s1_ref.shape[1], head, 0)


def _thresholds(s1, s2):
    nsub, heads, nk, lanes = s1.shape
    spec = pl.BlockSpec((1, heads, nk, lanes), lambda i: (i, 0, 0, 0))
    out = jax.ShapeDtypeStruct(s1.shape, F32)
    return pl.pallas_call(
        _thresh_kernel,
        grid=(nsub,),
        in_specs=[spec, spec],
        out_specs=[spec, spec, spec, spec],
        out_shape=[out, out, out, out],
        compiler_params=_params("parallel"),
        name="peer_thresholds",
    )(s1, s2)


GATE_ROWS = 64
GELU_C0 = math.sqrt(2.0 / math.pi)
GELU_C1 = 0.044715 * GELU_C0
PEER_CHUNK = 2048


def _peer_kernel(xn_ref, u_ref, vt_ref, r2_ref, g2_ref, ns_ref, g1_ref, out_ref, at_ref, wt_ref, acc_ref):
    c = pl.program_id(1)
    nsub, heads, nk, lanes = r2_ref.shape
    per = u_ref.shape[0] // nk

    @pl.when(c == 0)
    def _():
        acc_ref[...] = jnp.zeros_like(acc_ref)

    at_ref[...] = lax.dot_general(u_ref[...], xn_ref[...], (((1,), (1,)), ((), ())),
                                  preferred_element_type=F32)

    def gate_rows(j, carry):
        i1 = c * per + j
        row0 = pl.multiple_of(j * nk, nk)
        for tl in range(nsub):
            cols = slice(tl * lanes, (tl + 1) * lanes)
            for r0 in range(0, nk, GATE_ROWS):
                gate = jnp.zeros((GATE_ROWS, lanes), F32)
                for h in range(heads):
                    ns = ns_ref[tl, h, pl.ds(i1, 1), :]
                    g1 = g1_ref[tl, h, pl.ds(i1, 1), :]
                    sel = r2_ref[tl, h, r0:r0 + GATE_ROWS, :] < ns
                    gate = gate + jnp.where(sel, g1 * g2_ref[tl, h, r0:r0 + GATE_ROWS, :], 0.0)
                rows = pl.ds(row0 + r0, GATE_ROWS)
                x = at_ref[rows, cols]
                t = jnp.tanh(x * (GELU_C0 + GELU_C1 * (x * x)))
                wt_ref[rows, cols] = ((x + x * t) * gate).astype(BF16)
        return carry

    lax.fori_loop(0, per, gate_rows, 0)
    acc_ref[...] += jnp.dot(vt_ref[...], wt_ref[...], preferred_element_type=F32)

    @pl.when(c == pl.num_programs(1) - 1)
    def _():
        out_ref[...] = acc_ref[...].T


def _peer(xn, u, vt, r2, g2, ns, g1):
    n, d = xn.shape
    ne = u.shape[0]
    tb = _token_block(n)
    nsub = tb // LANES
    _, heads, nk, _ = r2.shape
    ec = PEER_CHUNK
    assert ne % ec == 0 and ec % nk == 0 and ne == nk * nk and nk % GATE_ROWS == 0
    table = pl.BlockSpec((nsub, heads, nk, LANES), lambda i, c: (i, 0, 0, 0))
    return pl.pallas_call(
        _peer_kernel,
        grid=(n // tb, ne // ec),
        in_specs=[
            pl.BlockSpec((tb, d), lambda i, c: (i, 0)),
            pl.BlockSpec((ec, d), lambda i, c: (c, 0)),
            pl.BlockSpec((d, ec), lambda i, c: (0, c)),
            table, table, table, table,
        ],
        out_specs=pl.BlockSpec((tb, d), lambda i, c: (i, 0)),
        out_shape=jax.ShapeDtypeStruct((n, d), F32),
        scratch_shapes=[
            pltpu.VMEM((ec, tb), F32),
            pltpu.VMEM((ec, tb), BF16),
            pltpu.VMEM((d, tb), F32),
        ],
        compiler_params=_params("parallel", "arbitrary"),
        name="peer_mix",
    )(xn, u, vt, r2, g2, ns, g1)


def _final_kernel(x1_ref, pe_ref, p_ref, g_ref, wg_ref, wp_ref, y_ref):
    x2 = x1_ref[...] + pe_ref[...]
    xn = _rms(x2, g_ref[...]).astype(BF16)
    gate = jax.nn.sigmoid(jnp.dot(xn, wg_ref[...], preferred_element_type=F32))
    y_ref[...] = x2 + gate * jnp.dot(p_ref[...].astype(BF16), wp_ref[...], preferred_element_type=F32)


def _final(x1, pe, p, g, wg, wp):
    n, d = x1.shape
    tb = _token_block(n)
    row = lambda i: (i, 0)
    const = lambda i: (0, 0)
    return pl.pallas_call(
        _final_kernel,
        grid=(n // tb,),
        in_specs=[
            pl.BlockSpec((tb, d), row),
            pl.BlockSpec((tb, d), row),
            pl.BlockSpec((tb, p.shape[1]), row),
            pl.BlockSpec((1, d), const),
            pl.BlockSpec(wg.shape, const),
            pl.BlockSpec(wp.shape, const),
        ],
        out_specs=pl.BlockSpec((tb, d), row),
        out_shape=jax.ShapeDtypeStruct((n, d), F32),
        compiler_params=_params("parallel"),
        name="ple_epilogue",
    )(x1, pe, p, g, wg, wp)


def _layer(xp, xs, pp, ps, past_k, past_v, lp):
    b, s, d = xp.shape
    bd, l, _ = xs.shape
    n_p = b * s
    kvw = N_KV_HEADS * HEAD_DIM
    sgw = d // 2
    gd = sgw // SGU_GROUPS
    x = jnp.concatenate([xp.reshape(n_p, d), xs.reshape(bd * l, d)], axis=0)
    p = jnp.concatenate([pp.reshape(n_p, -1), ps.reshape(bd * l, -1)], axis=0)

    q, k, v, u, vn, ga, gb = _inproj(
        x, lp['attn_norm_g'][None], lp['w_in'].astype(BF16),
        jnp.tile(lp['q_norm_g'], N_HEADS)[None], jnp.tile(lp['k_norm_g'], N_KV_HEADS)[None],
        lp['sgu_norm_g'][None], lp['sgu_norm_b'][None])

    split = lambda t: (t[:n_p].reshape(b, s, -1), t[n_p:])
    q_p, q_s = split(q)
    k_p, k_s = split(k)
    v_p, v_s = split(v)
    u_p, u_s = split(u)
    vn_p, vn_s = split(vn)

    sgu_w, sgu_b = lp['sgu_w'], lp['sgu_b']
    bias_p = jnp.repeat(sgu_b.T, gd, axis=1)
    a_p, m_p = _prompt_mix(lp['attn_sinks'], q_p, k_p, v_p, vn_p, u_p, sgu_w, bias_p)
    wexp = jnp.repeat(jnp.transpose(sgu_w[:, :l, :l], (2, 1, 0)), gd, axis=2)
    a_s, m_s = _sample_mix(lp['attn_sinks'], q_s, k_s, v_s,
                           past_k.reshape(bd, -1, kvw), past_v.reshape(bd, -1, kvw),
                           vn_s, u_s, wexp, bias_p[:l])
    a = jnp.concatenate([a_p.reshape(n_p, -1), a_s], axis=0)
    m = jnp.concatenate([m_p.reshape(n_p, -1), m_s], axis=0)

    keys = lp['peer_sub_keys'].reshape(2 * PEER_HEADS, PEER_N_KEYS, -1).astype(BF16)
    x1, xn1, s1, s2 = _merge(x, a, m, ga, gb, lp['w_branch_a'].astype(BF16), lp['w_branch_b'].astype(BF16),
                             lp['w_out'].astype(BF16), lp['ffn_norm_g'][None], lp['peer_w_q'].astype(BF16), keys)
    r2, g2, ns, g1 = _thresholds(s1, s2)
    pe = _peer(xn1, lp['peer_u'].astype(BF16), lp['peer_v'].astype(BF16).T, r2, g2, ns, g1)
    y = _final(x1, pe, p, lp['ple_norm_g'][None], lp['w_ple_gate'].astype(BF16), lp['w_ple'].astype(BF16))

    wp = min(WINDOW, s)
    heads = lambda t, nb: t.reshape(nb, -1, N_KV_HEADS, HEAD_DIM)
    return (y[:n_p].reshape(b, s, d), y[n_p:].reshape(bd, l, d),
            heads(k_p[:, s - wp:], b), heads(v_p[:, s - wp:], b), heads(k_s, bd), heads(v_s, bd),
            vn_p[:, s - CHUNK:], vn_s.reshape(bd, l, sgw))


def kernel(x_prompt, x_sample, cache_k, cache_v, p_prompt, p_sample, attn_norm_g, w_in, q_norm_g, k_norm_g, attn_sinks, sgu_norm_g, sgu_norm_b, sgu_w, sgu_b, w_branch_a, w_branch_b, w_out, ffn_norm_g, peer_w_q, peer_sub_keys, peer_u, peer_v, ple_norm_g, w_ple, w_ple_gate):
    depth = w_in.shape[0]
    hp, hs = x_prompt, x_sample
    outs = [[] for _ in range(6)]
    for i in range(depth):
        lp = dict(attn_norm_g=attn_norm_g[i], w_in=w_in[i], q_norm_g=q_norm_g[i], k_norm_g=k_norm_g[i],
                  attn_sinks=attn_sinks[i], sgu_norm_g=sgu_norm_g[i], sgu_norm_b=sgu_norm_b[i],
                  sgu_w=sgu_w[i], sgu_b=sgu_b[i], w_branch_a=w_branch_a[i], w_branch_b=w_branch_b[i],
                  w_out=w_out[i], ffn_norm_g=ffn_norm_g[i], peer_w_q=peer_w_q[i],
                  peer_sub_keys=peer_sub_keys[i], peer_u=peer_u[i], peer_v=peer_v[i],
                  ple_norm_g=ple_norm_g[i], w_ple=w_ple[i], w_ple_gate=w_ple_gate[i])
        res = _layer(hp, hs, p_prompt[i], p_sample[i], cache_k[i], cache_v[i], lp)
        hp, hs = res[0], res[1]
        for lst, t in zip(outs, res[2:]):
            lst.append(t)
    return (hp, hs) + tuple(jnp.stack(o) for o in outs)
```

```python
import functools
import math

import jax
import jax.numpy as jnp
from jax import lax
from jax.experimental import pallas as pl
from jax.experimental.pallas import tpu as pltpu

F32 = jnp.float32
BF16 = jnp.bfloat16

N_HEADS = 8
N_KV_HEADS = 2
HEAD_DIM = 64
Q_GROUP = N_HEADS // N_KV_HEADS
WINDOW = 128
CHUNK = 128
SGU_GROUPS = 4
PEER_HEADS = 8
PEER_N_KEYS = 128
PEER_TOPK = 16
EPS = 1e-6
NEG_INF = -1e30
ALIBI_SLOPES = tuple(2.0 ** (-8.0 * h / N_HEADS) for h in range(1, N_HEADS + 1))

LANES = 128
SUBLANES = 8
MXU_COLS = 256
VMEM_LIMIT = 56 * 1024 * 1024


def _params(*semantics):
    return pltpu.CompilerParams(dimension_semantics=semantics, vmem_limit_bytes=VMEM_LIMIT)


def _token_block(*counts):
    for tb in (512, 256, 128):
        if all(n % tb == 0 for n in counts):
            return tb
    raise ValueError(f"token counts {counts} must be multiples of 128")


def _rms(x, g):
    return x * lax.rsqrt(jnp.mean(x * x, axis=-1, keepdims=True) + EPS) * g


def _group_rms(t, ones_blk, g):
    t2 = t * t
    hi = t2.astype(BF16)
    lo = (t2 - hi.astype(F32)).astype(BF16)
    ss = (jnp.dot(hi, ones_blk, preferred_element_type=F32)
          + jnp.dot(lo, ones_blk, preferred_element_type=F32))
    return t * lax.rsqrt(ss * (1.0 / HEAD_DIM) + EPS) * g


def _inproj_kernel(xp_ref, xs_ref, g_ref, w_ref, qg_ref, kg_ref, lg_ref, lb_ref, bq_ref, bk_ref,
                   q_ref, k_ref, v_ref, u_ref, vn_ref, ga_ref, gb_ref, *, prompt_blocks):
    x = jnp.where(pl.program_id(0) < prompt_blocks, xp_ref[...], xs_ref[...])
    xn = _rms(x, g_ref[...])
    z = jnp.dot(xn.astype(BF16), w_ref[...], preferred_element_type=F32)
    att = N_HEADS * HEAD_DIM
    kvw = N_KV_HEADS * HEAD_DIM
    sgw = (z.shape[1] - att - 2 * kvw) // 6
    o = 0
    q = z[:, o:o + att]; o += att
    k = z[:, o:o + kvw]; o += kvw
    v = z[:, o:o + kvw]; o += kvw
    su = z[:, o:o + sgw]; o += sgw
    sv = z[:, o:o + sgw]; o += sgw
    g_a = z[:, o:o + 2 * sgw]; o += 2 * sgw
    g_b = z[:, o:o + 2 * sgw]
    qn = _group_rms(q, bq_ref[...], qg_ref[...])
    q_ref[...] = (qn * (HEAD_DIM ** -0.5)).astype(BF16)
    k_ref[...] = _group_rms(k, bk_ref[...], kg_ref[...])
    v_ref[...] = v
    u_ref[...] = jax.nn.gelu(su).astype(BF16)
    gv = jax.nn.gelu(sv)
    mu = jnp.mean(gv, axis=-1, keepdims=True)
    gc = gv - mu
    vn_ref[...] = gc * lax.rsqrt(jnp.mean(gc * gc, axis=-1, keepdims=True) + EPS) * lg_ref[...] + lb_ref[...]
    ga_ref[...] = jax.nn.sigmoid(g_a).astype(BF16)
    gb_ref[...] = jax.nn.sigmoid(g_b).astype(BF16)


def _inproj(xp, xs, g, w_in, qg, kg, lg, lb):
    (n_p, d), n_s = xp.shape, xs.shape[0]
    n = n_p + n_s
    tb = _token_block(n_p, n_s)
    nbp = n_p // tb
    att = N_HEADS * HEAD_DIM
    kvw = N_KV_HEADS * HEAD_DIM
    sgw = d // 2
    hid = jnp.arange(att) // HEAD_DIM
    bq = (hid[:, None] == hid[None, :]).astype(BF16)
    bk = bq[:kvw, :kvw]
    const = lambda i: (0, 0)
    row = lambda i: (i, 0)
    outs = [
        jax.ShapeDtypeStruct((n, att), BF16),
        jax.ShapeDtypeStruct((n, kvw), F32),
        jax.ShapeDtypeStruct((n, kvw), F32),
        jax.ShapeDtypeStruct((n, sgw), BF16),
        jax.ShapeDtypeStruct((n, sgw), F32),
        jax.ShapeDtypeStruct((n, d), BF16),
        jax.ShapeDtypeStruct((n, d), BF16),
    ]
    return pl.pallas_call(
        functools.partial(_inproj_kernel, prompt_blocks=nbp),
        grid=(n // tb,),
        in_specs=[
            pl.BlockSpec((tb, d), lambda i: (jnp.minimum(i, nbp - 1), 0)),
            pl.BlockSpec((tb, d), lambda i: (jnp.maximum(i - nbp, 0), 0)),
            pl.BlockSpec((1, d), const),
            pl.BlockSpec(w_in.shape, const),
            pl.BlockSpec((1, att), const),
            pl.BlockSpec((1, kvw), const),
            pl.BlockSpec((1, sgw), const),
            pl.BlockSpec((1, sgw), const),
            pl.BlockSpec((att, att), const),
            pl.BlockSpec((kvw, kvw), const),
        ],
        out_specs=[pl.BlockSpec((tb, s.shape[1]), row) for s in outs],
        out_shape=outs,
        compiler_params=_params("parallel"),
        name="inproj",
    )(xp, xs, g, w_in, qg, kg, lg, lb, bq, bk)


def _sink_softmax(s, sink):
    mx = jnp.maximum(jnp.max(s, axis=-1, keepdims=True), sink)
    p = jnp.exp(s - mx)
    den = jnp.sum(p, axis=-1, keepdims=True) + jnp.exp(sink - mx)
    return p / den


def _prompt_kernel(sinks_ref, q_ref, kc_ref, kp_ref, vc_ref, vp_ref, vn_ref, u_ref, w_ref, bias_ref,
                   a_ref, m_ref):
    i = pl.program_id(1)
    tq = q_ref.shape[0]
    nblk = tq // WINDOW
    q = q_ref[...]
    kc = kc_ref[...].astype(BF16)
    vc = vc_ref[...].astype(BF16)
    kp = kp_ref[...].astype(BF16)
    vp = vp_ref[...].astype(BF16)
    row = lax.broadcasted_iota(jnp.int32, (WINDOW, 2 * WINDOW), 0)
    col = lax.broadcasted_iota(jnp.int32, (WINDOW, 2 * WINDOW), 1)
    dist = row - col + WINDOW
    in_window = (dist >= 0) & (dist < WINDOW)
    distf = dist.astype(F32)
    for jq in range(nblk):
        r0 = jq * WINDOW
        if jq == 0:
            kprev, vprev = kp, vp
            valid = in_window & (col >= jnp.where(i > 0, 0, WINDOW))
        else:
            kprev, vprev = kc[r0 - WINDOW:r0], vc[r0 - WINDOW:r0]
            valid = in_window
        kcat = jnp.concatenate([kprev, kc[r0:r0 + WINDOW]], axis=0)
        vcat = jnp.concatenate([vprev, vc[r0:r0 + WINDOW]], axis=0)
        for g in range(N_KV_HEADS):
            heads = range(g * Q_GROUP, (g + 1) * Q_GROUP)
            qg = jnp.concatenate([q[r0:r0 + WINDOW, h * HEAD_DIM:(h + 1) * HEAD_DIM] for h in heads], axis=0)
            s_all = lax.dot_general(qg, kcat[:, g * HEAD_DIM:(g + 1) * HEAD_DIM],
                                    (((1,), (1,)), ((), ())), preferred_element_type=F32)
            probs = []
            for hl, h in enumerate(heads):
                s = s_all[hl * WINDOW:(hl + 1) * WINDOW] - ALIBI_SLOPES[h] * distf
                s = jnp.where(valid, s, NEG_INF)
                probs.append(_sink_softmax(s, sinks_ref[h]).astype(BF16))
            o_all = jnp.dot(jnp.concatenate(probs, axis=0), vcat[:, g * HEAD_DIM:(g + 1) * HEAD_DIM],
                            preferred_element_type=F32)
            for hl, h in enumerate(heads):
                a_ref[r0:r0 + WINDOW, h * HEAD_DIM:(h + 1) * HEAD_DIM] = (
                    o_all[hl * WINDOW:(hl + 1) * WINDOW].astype(BF16))
    tr = lax.broadcasted_iota(jnp.int32, (CHUNK, CHUNK), 0)
    tc = lax.broadcasted_iota(jnp.int32, (CHUNK, CHUNK), 1)
    gd = vn_ref.shape[1] // SGU_GROUPS
    wm = [jnp.where(tr >= tc, w_ref[g], 0.0).astype(BF16) for g in range(SGU_GROUPS)]
    for c in range(tq // CHUNK):
        r0 = c * CHUNK
        vnc = vn_ref[r0:r0 + CHUNK, :].astype(BF16)
        for g in range(SGU_GROUPS):
            s = jnp.dot(wm[g], vnc[:, g * gd:(g + 1) * gd], preferred_element_type=F32)
            s = s + bias_ref[:, g * gd:(g + 1) * gd]
            m_ref[r0:r0 + CHUNK, g * gd:(g + 1) * gd] = (
                u_ref[r0:r0 + CHUNK, g * gd:(g + 1) * gd].astype(F32) * s).astype(BF16)


def _prompt_mix(sinks, q, k, v, vn, u, sgu_w, sgu_bias, b, s):
    n, att = q.shape
    kvw = k.shape[1]
    sgw = vn.shape[1]
    tq = 512 if s % 512 == 0 else WINDOW
    assert s % tq == 0 and tq % WINDOW == 0 and WINDOW == CHUNK
    r = tq // WINDOW
    nq = s // tq
    cur = lambda bi, i: (bi * nq + i, 0)
    prev = lambda bi, i: (jnp.maximum((bi * nq + i) * r - 1, 0), 0)
    outs = [jax.ShapeDtypeStruct((n, att), BF16), jax.ShapeDtypeStruct((n, sgw), BF16)]
    return pl.pallas_call(
        _prompt_kernel,
        grid=(b, nq),
        in_specs=[
            pl.BlockSpec(memory_space=pltpu.SMEM),
            pl.BlockSpec((tq, att), cur),
            pl.BlockSpec((tq, kvw), cur),
            pl.BlockSpec((WINDOW, kvw), prev),
            pl.BlockSpec((tq, kvw), cur),
            pl.BlockSpec((WINDOW, kvw), prev),
            pl.BlockSpec((tq, sgw), cur),
            pl.BlockSpec((tq, sgw), cur),
            pl.BlockSpec(sgu_w.shape, lambda bi, i: (0, 0, 0)),
            pl.BlockSpec(sgu_bias.shape, lambda bi, i: (0, 0)),
        ],
        out_specs=[pl.BlockSpec((tq, att), cur), pl.BlockSpec((tq, sgw), cur)],
        out_shape=outs,
        compiler_params=_params("parallel", "parallel"),
        name="prompt_mix",
    )(sinks, q, k, k, v, v, vn, u, sgu_w, sgu_bias)


def _sample_kernel(sinks_ref, q_ref, kn_ref, vn_new_ref, ck_ref, cv_ref, vn_ref, u_ref, wexp_ref, bias_ref,
                   a_in_ref, m_in_ref, a_ref, m_ref):
    del a_in_ref, m_in_ref
    bb, w, _ = ck_ref.shape
    l = q_ref.shape[0] // bb
    per_seq = lambda ref: ref[...].astype(F32).reshape(bb, l, ref.shape[1])
    q = per_seq(q_ref)
    kcat = jnp.concatenate([ck_ref[...], per_seq(kn_ref)], axis=1).astype(BF16)
    vcat = jnp.concatenate([cv_ref[...], per_seq(vn_new_ref)], axis=1).astype(BF16)
    rows = Q_GROUP * l
    t = lax.broadcasted_iota(jnp.int32, (rows, w + l), 0) % l
    key = lax.broadcasted_iota(jnp.int32, (rows, w + l), 1)
    dist = t - (key - w)
    valid = (dist >= 0) & (dist < WINDOW)
    distf = dist.astype(F32)
    hl_of_row = lax.broadcasted_iota(jnp.int32, (rows, 1), 0) // l
    for g in range(N_KV_HEADS):
        heads = range(g * Q_GROUP, (g + 1) * Q_GROUP)
        qg = jnp.concatenate([q[:, :, h * HEAD_DIM:(h + 1) * HEAD_DIM] for h in heads], axis=1)
        s = jnp.einsum('bqd,bkd->bqk', qg.astype(BF16), kcat[:, :, g * HEAD_DIM:(g + 1) * HEAD_DIM],
                       preferred_element_type=F32)
        slope = jnp.zeros((rows, 1), F32)
        sink = jnp.zeros((rows, 1), F32)
        for hl, h in enumerate(heads):
            slope = jnp.where(hl_of_row == hl, ALIBI_SLOPES[h], slope)
            sink = jnp.where(hl_of_row == hl, sinks_ref[h], sink)
        s = jnp.where(valid[None], s - (slope * distf)[None], NEG_INF)
        p = _sink_softmax(s, sink[None]).astype(BF16)
        o = jnp.einsum('bqk,bkd->bqd', p, vcat[:, :, g * HEAD_DIM:(g + 1) * HEAD_DIM],
                       preferred_element_type=F32)
        for hl, h in enumerate(heads):
            a_ref[:, h * HEAD_DIM:(h + 1) * HEAD_DIM] = (
                o[:, hl * l:(hl + 1) * l, :].reshape(bb * l, HEAD_DIM).astype(BF16))
    vn = per_seq(vn_ref)
    tt = lax.broadcasted_iota(jnp.int32, (l, vn.shape[2]), 0)
    s = jnp.broadcast_to(bias_ref[...][None], vn.shape)
    for sp in range(l):
        wm = jnp.where(tt >= sp, wexp_ref[sp], 0.0)
        s = s + wm[None] * vn[:, sp:sp + 1, :]
    m_ref[...] = (per_seq(u_ref) * s).reshape(bb * l, vn.shape[2]).astype(BF16)


def _sample_mix(sinks, q, k, v, cache_k, cache_v, vn, u, wexp, bias, a, m, n_p):
    b, w, kvw = cache_k.shape
    n, att = q.shape
    l = (n - n_p) // b
    sgw = vn.shape[1]
    bb = 16 if b % 16 == 0 else b
    assert n_p % (bb * l) == 0
    first = n_p // (bb * l)
    tok = lambda width: pl.BlockSpec((bb * l, width), lambda i: (first + i, 0))
    past = pl.BlockSpec((bb, w, kvw), lambda i: (i, 0, 0))
    whole = pl.BlockSpec(memory_space=pl.ANY)
    outs = [jax.ShapeDtypeStruct(a.shape, a.dtype), jax.ShapeDtypeStruct(m.shape, m.dtype)]
    return pl.pallas_call(
        _sample_kernel,
        grid=(b // bb,),
        in_specs=[
            pl.BlockSpec(memory_space=pltpu.SMEM),
            tok(att), tok(kvw), tok(kvw), past, past, tok(sgw), tok(sgw),
            pl.BlockSpec(wexp.shape, lambda i: (0, 0, 0)),
            pl.BlockSpec(bias.shape, lambda i: (0, 0)),
            whole, whole,
        ],
        out_specs=[tok(att), tok(sgw)],
        out_shape=outs,
        input_output_aliases={10: 0, 11: 1},
        compiler_params=_params("parallel"),
        name="sample_mix",
    )(sinks, q, k, v, cache_k, cache_v, vn, u, wexp, bias, a, m)


def _merge_kernel(x_ref, a_ref, m_ref, ga_ref, gb_ref, wa_ref, wb_ref, wo_ref, fg_ref, wq_ref, keys_ref,
                  x1_ref, xn_ref, s1_ref, s2_ref):
    ha = jnp.dot(a_ref[...], wa_ref[...], preferred_element_type=F32)
    hb = jnp.dot(m_ref[...], wb_ref[...], preferred_element_type=F32)
    h = ga_ref[...].astype(F32) * ha + gb_ref[...].astype(F32) * hb
    x1 = x_ref[...] + jnp.dot(h.astype(BF16), wo_ref[...], preferred_element_type=F32)
    x1_ref[...] = x1
    xn = _rms(x1, fg_ref[...]).astype(BF16)
    xn_ref[...] = xn
    qp = jnp.dot(xn, wq_ref[...], preferred_element_type=F32).astype(BF16)
    half = keys_ref.shape[2]
    nsub = s1_ref.shape[0]
    for hc in range(keys_ref.shape[0]):
        st = lax.dot_general(keys_ref[hc], qp[:, hc * half:(hc + 1) * half],
                             (((1,), (1,)), ((), ())), preferred_element_type=F32)
        dst = s1_ref if hc % 2 == 0 else s2_ref
        for tl in range(nsub):
            dst[tl, hc // 2] = st[:, tl * LANES:(tl + 1) * LANES]


def _merge(xp, xs, a, m, ga, gb, wa, wb, wo, fg, wq, keys):
    (n_p, d), n_s = xp.shape, xs.shape[0]
    n = n_p + n_s
    tb = _token_block(n_p, n_s)
    nbp = n_p // tb
    nsub = tb // LANES
    hc, nk, half = keys.shape
    row = lambda i: (i, 0)
    const2 = lambda i: (0, 0)
    heads = hc // 2
    outs = [
        jax.ShapeDtypeStruct((n, d), F32),
        jax.ShapeDtypeStruct((n, d), BF16),
        jax.ShapeDtypeStruct((n // LANES, heads, nk, LANES), F32),
        jax.ShapeDtypeStruct((n // LANES, heads, nk, LANES), F32),
    ]

    def body(xp_ref, xs_ref, *rest):
        x = jnp.where(pl.program_id(0) < nbp, xp_ref[...], xs_ref[...])
        _merge_kernel(_Value(x), *rest)

    return pl.pallas_call(
        body,
        grid=(n // tb,),
        in_specs=[
            pl.BlockSpec((tb, d), lambda i: (jnp.minimum(i, nbp - 1), 0)),
            pl.BlockSpec((tb, d), lambda i: (jnp.maximum(i - nbp, 0), 0)),
            pl.BlockSpec((tb, a.shape[1]), row),
            pl.BlockSpec((tb, m.shape[1]), row),
            pl.BlockSpec((tb, d), row),
            pl.BlockSpec((tb, d), row),
            pl.BlockSpec(wa.shape, const2),
            pl.BlockSpec(wb.shape, const2),
            pl.BlockSpec(wo.shape, const2),
            pl.BlockSpec((1, d), const2),
            pl.BlockSpec(wq.shape, const2),
            pl.BlockSpec(keys.shape, lambda i: (0, 0, 0)),
        ],
        out_specs=[
            pl.BlockSpec((tb, d), row),
            pl.BlockSpec((tb, d), row),
            pl.BlockSpec((nsub, heads, nk, LANES), lambda i: (i, 0, 0, 0)),
            pl.BlockSpec((nsub, heads, nk, LANES), lambda i: (i, 0, 0, 0)),
        ],
        out_shape=outs,
        compiler_params=_params("parallel"),
        name="merge",
    )(xp, xs, a, m, ga, gb, wa, wb, wo, fg, wq, keys)


class _Value:
    def __init__(self, value):
        self._value = value

    def __getitem__(self, idx):
        return self._value[idx]


def _oddeven_merge(lo, hi, r):
    step = r * 2
    if step < hi - lo:
        yield from _oddeven_merge(lo, hi, step)
        yield from _oddeven_merge(lo + r, hi, step)
        yield from [(i, i + r) for i in range(lo + r, hi - r, step)]
    else:
        yield (lo, lo + r)


def _oddeven_merge_sort(lo, hi):
    if hi - lo >= 1:
        mid = lo + (hi - lo) // 2
        yield from _oddeven_merge_sort(lo, mid)
        yield from _oddeven_merge_sort(mid + 1, hi)
        yield from _oddeven_merge(lo, hi, 1)


_SORT_TOPK = tuple(_oddeven_merge_sort(0, PEER_TOPK - 1))


def _cmpx(w, i, j):
    a, b = w[i], w[j]
    if b is None:
        return
    if a is None:
        w[i], w[j] = b, None
        return
    w[i], w[j] = jnp.maximum(a, b), jnp.minimum(a, b)


def _top_values(w):
    k = PEER_TOPK
    w = list(w)
    for i, j in _SORT_TOPK:
        _cmpx(w, i, j)
    shift = SUBLANES // 2
    while shift >= 1:
        y = [None if v is None else pltpu.roll(v, shift, 0) for v in w]
        z = []
        for r in range(k):
            a, b = w[r], y[k - 1 - r]
            z.append(b if a is None else a if b is None else jnp.maximum(a, b))
        stride = k // 2
        while stride >= 1:
            for i in range(k):
                if i & stride == 0:
                    _cmpx(z, i, i + stride)
            stride //= 2
        w = z
        shift //= 2
    return w


def _thresh_kernel(s1_ref, s2_ref, g2_ref, gm_ref, g1_ref):
    k = PEER_TOPK
    nk, lanes = s1_ref.shape[2], s1_ref.shape[3]
    nslot = nk // SUBLANES
    assert nslot == k and k == 2 * SUBLANES
    sub = lax.broadcasted_iota(jnp.int32, (SUBLANES, lanes), 0)

    def pack(vals):
        out = vals[0]
        for j in range(1, SUBLANES):
            out = jnp.where(sub == j, vals[j], out)
        return out

    def head(h, carry):
        w1 = [s1_ref[0, h, r * SUBLANES:(r + 1) * SUBLANES, :] for r in range(nslot)]
        w2 = [s2_ref[0, h, r * SUBLANES:(r + 1) * SUBLANES, :] for r in range(nslot)]
        a = _top_values(w1)
        b = _top_values(w2)
        b_lo, b_hi, a_hi = pack(b[:SUBLANES]), pack(b[SUBLANES:]), pack(a[SUBLANES:])
        cands = ([a[0] + b_lo, a[0] + b_hi] + [a[i] + b_lo for i in range(1, SUBLANES)] + [a_hi + b[0]])
        best = _top_values(cands + [None] * (k - len(cands)))
        tau = best[k - 1]
        z = jnp.ones_like(tau)
        for r in range(1, k):
            z = z + jnp.exp(best[r] - best[0])
        inv_z = 1.0 / z
        eb = [jnp.exp(b[j] - b[0]) for j in range(k)]
        gamma = []
        for i in range(k):
            t = jnp.full_like(tau, jnp.inf)
            for j in range(k // (i + 1)):
                t = jnp.where(a[i] + b[j] >= tau, eb[j], t)
            gamma.append(t)
        for r in range(nslot):
            gm = jnp.full_like(tau, jnp.inf)
            for i in range(k - 1, -1, -1):
                gm = jnp.where(w1[r] >= a[i], gamma[i], gm)
            rows = pl.ds(r * SUBLANES, SUBLANES)
            gm_ref[0, h, rows, :] = gm
            g1_ref[0, h, rows, :] = jnp.exp(w1[r] - a[0]) * (0.5 * inv_z)
            g2_ref[0, h, rows, :] = jnp.exp(w2[r] - b[0])
        return carry

    lax.fori_loop(0, s1_ref.shape[1], head, 0)


def _thresholds(s1, s2):
    nsub, heads, nk, lanes = s1.shape
    spec = pl.BlockSpec((1, heads, nk, lanes), lambda i: (i, 0, 0, 0))
    out = jax.ShapeDtypeStruct(s1.shape, F32)
    return pl.pallas_call(
        _thresh_kernel,
        grid=(nsub,),
        in_specs=[spec, spec],
        out_specs=[spec, spec, spec],
        out_shape=[out, out, out],
        compiler_params=_params("parallel"),
        name="peer_thresholds",
    )(s1, s2)


GATE_ROWS = 64
GELU_C0 = math.sqrt(2.0 / math.pi)
GELU_C1 = 0.044715 * GELU_C0
PEER_CHUNK = 1024


def _gate_tasks(at_ref, wt_ref, j0, g2_ref, gm_ref, g1_ref):
    nsub, heads, nk, lanes = g2_ref.shape

    def tile(j, tl, r0):
        gate = jnp.zeros((GATE_ROWS, lanes), F32)
        for h in range(heads):
            gm = gm_ref[tl, h, j0 + j:j0 + j + 1, :]
            g1 = g1_ref[tl, h, j0 + j:j0 + j + 1, :]
            g2 = g2_ref[tl, h, r0:r0 + GATE_ROWS, :]
            gate = gate + jnp.where(g2 >= gm, g1 * g2, 0.0)
        rows = slice(j * nk + r0, j * nk + r0 + GATE_ROWS)
        cols = slice(tl * lanes, (tl + 1) * lanes)
        x = at_ref[rows, cols]
        t = jnp.tanh(x * (GELU_C0 + GELU_C1 * (x * x)))
        wt_ref[rows, cols] = ((x + x * t) * gate).astype(BF16)

    return [functools.partial(tile, j, tl, r0)
            for tl in range(nsub) for j in range(at_ref.shape[0] // nk) for r0 in range(0, nk, GATE_ROWS)]


def _interleave(main, *others):
    slots = [[] for _ in range(len(main) + 1)]
    for oi, tasks in enumerate(others):
        for k, t in enumerate(tasks):
            slots[((k * len(others) + oi) * len(main)) // (len(tasks) * len(others))].append(t)
    order = []
    for k, m in enumerate(main):
        order.extend(slots[k])
        order.append(m)
    return order


def _peer_kernel(xn_ref, u0_ref, ub_ref, ua_ref, vp_ref, va_ref, g2_ref, gm_ref, g1_ref, out_ref,
                 at0, at1, wt0, wt1, acc):
    s = pl.program_id(1)
    last = pl.num_programs(1) - 1
    per = at0.shape[0] // g2_ref.shape[2]
    tb = xn_ref.shape[0]
    tables = (g2_ref, gm_ref, g1_ref)
    nt = (((1,), (1,)), ((), ()))
    col_tiles = [slice(c, c + min(MXU_COLS, tb)) for c in range(0, tb, min(MXU_COLS, tb))]

    def act_tasks(u_ref, at):
        def piece(cs):
            at[:, cs] = lax.dot_general(u_ref[...], xn_ref[cs, :], nt, preferred_element_type=F32)
        return [functools.partial(piece, cs) for cs in col_tiles]

    def mix_tasks(v_ref, wt):
        def piece(cs):
            acc[:, cs] += jnp.dot(v_ref[...], wt[:, cs], preferred_element_type=F32)
        return [functools.partial(piece, cs) for cs in col_tiles]

    @pl.when(s == 0)
    def _():
        for t in act_tasks(u0_ref, at0):
            t()
        wt1[...] = jnp.zeros_like(wt1)
        acc[...] = jnp.zeros_like(acc)

    @pl.when(s < last)
    def _():
        for t in _interleave(_gate_tasks(at0, wt0, 0, *tables),
                             act_tasks(ub_ref, at1), mix_tasks(vp_ref, wt1)):
            t()
        for t in _interleave(_gate_tasks(at1, wt1, per, *tables),
                             act_tasks(ua_ref, at0), mix_tasks(va_ref, wt0)):
            t()

    @pl.when(s == last)
    def _():
        for t in mix_tasks(vp_ref, wt1):
            t()
        out_ref[...] = acc[...].T


def _peer(xn, u, vt, g2, gm, g1):
    n, d = xn.shape
    ne = u.shape[0]
    tb = _token_block(n)
    nsub = tb // LANES
    _, heads, nk, _ = g2.shape
    ec = PEER_CHUNK
    assert ne % (2 * ec) == 0 and ec % nk == 0 and ne == nk * nk and nk % GATE_ROWS == 0
    nc = ne // ec
    steps = nc // 2
    per = ec // nk
    keyed2 = pl.BlockSpec((nsub, heads, nk, LANES), lambda i, s: (i, 0, 0, 0))
    keyed1 = pl.BlockSpec((nsub, heads, 2 * per, LANES), lambda i, s: (i, 0, jnp.minimum(s, steps - 1), 0))
    return pl.pallas_call(
        _peer_kernel,
        grid=(n // tb, steps + 1),
        in_specs=[
            pl.BlockSpec((tb, d), lambda i, s: (i, 0)),
            pl.BlockSpec((ec, d), lambda i, s: (0, 0)),
            pl.BlockSpec((ec, d), lambda i, s: (jnp.minimum(2 * s + 1, nc - 1), 0)),
            pl.BlockSpec((ec, d), lambda i, s: (jnp.minimum(2 * s + 2, nc - 2), 0)),
            pl.BlockSpec((d, ec), lambda i, s: (0, jnp.maximum(2 * s - 1, 0))),
            pl.BlockSpec((d, ec), lambda i, s: (0, jnp.minimum(2 * s, nc - 2))),
            keyed2, keyed1, keyed1,
        ],
        out_specs=pl.BlockSpec((tb, d), lambda i, s: (i, 0)),
        out_shape=jax.ShapeDtypeStruct((n, d), F32),
        scratch_shapes=[
            pltpu.VMEM((ec, tb), F32),
            pltpu.VMEM((ec, tb), F32),
            pltpu.VMEM((ec, tb), BF16),
            pltpu.VMEM((ec, tb), BF16),
            pltpu.VMEM((d, tb), F32),
        ],
        compiler_params=_params("parallel", "arbitrary"),
        name="peer_mix",
    )(xn, u, u, u, vt, vt, g2, gm, g1)


def _final_kernel(x1_ref, pe_ref, p_ref, g_ref, wg_ref, wp_ref, y_ref):
    x2 = x1_ref[...] + pe_ref[...]
    xn = _rms(x2, g_ref[...]).astype(BF16)
    gate = jax.nn.sigmoid(jnp.dot(xn, wg_ref[...], preferred_element_type=F32))
    y_ref[...] = x2 + gate * jnp.dot(p_ref[...].astype(BF16), wp_ref[...], preferred_element_type=F32)


def _final(x1, pe, p, g, wg, wp, row0):
    n, pd = p.shape
    d = x1.shape[1]
    tb = _token_block(n, row0)
    first = row0 // tb
    src = lambda i: (first + i, 0)
    row = lambda i: (i, 0)
    const = lambda i: (0, 0)
    return pl.pallas_call(
        _final_kernel,
        grid=(n // tb,),
        in_specs=[
            pl.BlockSpec((tb, d), src),
            pl.BlockSpec((tb, d), src),
            pl.BlockSpec((tb, pd), row),
            pl.BlockSpec((1, d), const),
            pl.BlockSpec(wg.shape, const),
            pl.BlockSpec(wp.shape, const),
        ],
        out_specs=pl.BlockSpec((tb, d), row),
        out_shape=jax.ShapeDtypeStruct((n, d), F32),
        compiler_params=_params("parallel"),
        name="ple_epilogue",
    )(x1, pe, p, g, wg, wp)


def _layer(xp, xs, pp, ps, past_k, past_v, lp):
    b, s, d = xp.shape
    bd, l, _ = xs.shape
    n_p, n_s = b * s, bd * l
    kvw = N_KV_HEADS * HEAD_DIM
    sgw = d // 2
    gd = sgw // SGU_GROUPS
    xp2, xs2 = xp.reshape(n_p, d), xs.reshape(n_s, d)

    q, k, v, u, vn, ga, gb = _inproj(
        xp2, xs2, lp['attn_norm_g'][None], lp['w_in'].astype(BF16),
        jnp.tile(lp['q_norm_g'], N_HEADS)[None], jnp.tile(lp['k_norm_g'], N_KV_HEADS)[None],
        lp['sgu_norm_g'][None], lp['sgu_norm_b'][None])

    sgu_w, sgu_b = lp['sgu_w'], lp['sgu_b']
    bias_p = jnp.repeat(sgu_b.T, gd, axis=1)
    a, m = _prompt_mix(lp['attn_sinks'], q, k, v, vn, u, sgu_w, bias_p, b, s)
    wexp = jnp.repeat(jnp.transpose(sgu_w[:, :l, :l], (2, 1, 0)), gd, axis=2)
    a, m = _sample_mix(lp['attn_sinks'], q, k, v, past_k.reshape(bd, -1, kvw), past_v.reshape(bd, -1, kvw),
                       vn, u, wexp, bias_p[:l], a, m, n_p)

    keys = lp['peer_sub_keys'].reshape(2 * PEER_HEADS, PEER_N_KEYS, -1).astype(BF16)
    x1, xn1, s1, s2 = _merge(xp2, xs2, a, m, ga, gb, lp['w_branch_a'].astype(BF16),
                             lp['w_branch_b'].astype(BF16), lp['w_out'].astype(BF16), lp['ffn_norm_g'][None],
                             lp['peer_w_q'].astype(BF16), keys)
    g2, gm, g1 = _thresholds(s1, s2)
    pe = _peer(xn1, lp['peer_u'].astype(BF16), lp['peer_v'].astype(BF16).T, g2, gm, g1)
    ple = (lp['ple_norm_g'][None], lp['w_ple_gate'].astype(BF16), lp['w_ple'].astype(BF16))
    y_p = _final(x1, pe, pp.reshape(n_p, -1), *ple, 0)
    y_s = _final(x1, pe, ps.reshape(n_s, -1), *ple, n_p)

    wp = min(WINDOW, s)
    tail = lambda t, rows: t[:n_p].reshape(b, s, -1)[:, s - rows:]
    heads = lambda t: t.reshape(t.shape[0], t.shape[1], N_KV_HEADS, HEAD_DIM)
    return (y_p.reshape(b, s, d), y_s.reshape(bd, l, d),
            heads(tail(k, wp)), heads(tail(v, wp)),
            heads(k[n_p:].reshape(bd, l, kvw)), heads(v[n_p:].reshape(bd, l, kvw)),
            tail(vn, CHUNK), vn[n_p:].reshape(bd, l, sgw))


def kernel(x_prompt, x_sample, cache_k, cache_v, p_prompt, p_sample, attn_norm_g, w_in, q_norm_g, k_norm_g, attn_sinks, sgu_norm_g, sgu_norm_b, sgu_w, sgu_b, w_branch_a, w_branch_b, w_out, ffn_norm_g, peer_w_q, peer_sub_keys, peer_u, peer_v, ple_norm_g, w_ple, w_ple_gate):
    depth = w_in.shape[0]
    hp, hs = x_prompt, x_sample
    outs = [[] for _ in range(6)]
    for i in range(depth):
        lp = dict(attn_norm_g=attn_norm_g[i], w_in=w_in[i], q_norm_g=q_norm_g[i], k_norm_g=k_norm_g[i],
                  attn_sinks=attn_sinks[i], sgu_norm_g=sgu_norm_g[i], sgu_norm_b=sgu_norm_b[i],
                  sgu_w=sgu_w[i], sgu_b=sgu_b[i], w_branch_a=w_branch_a[i], w_branch_b=w_branch_b[i],
                  w_out=w_out[i], ffn_norm_g=ffn_norm_g[i], peer_w_q=peer_w_q[i],
                  peer_sub_keys=peer_sub_keys[i], peer_u=peer_u[i], peer_v=peer_v[i],
                  ple_norm_g=ple_norm_g[i], w_ple=w_ple[i], w_ple_gate=w_ple_gate[i])
        res = _layer(hp, hs, p_prompt[i], p_sample[i], cache_k[i], cache_v[i], lp)
        hp, hs = res[0], res[1]
        for lst, t in zip(outs, res[2:]):
            lst.append(t)
    return (hp, hs) + tuple(jnp.stack(o) for o in outs)
```

```python
import functools
import math

import jax
import jax.numpy as jnp
from jax import lax
from jax.experimental import pallas as pl
from jax.experimental.pallas import tpu as pltpu

F32 = jnp.float32
BF16 = jnp.bfloat16

N_HEADS = 8
N_KV_HEADS = 2
HEAD_DIM = 64
Q_GROUP = N_HEADS // N_KV_HEADS
WINDOW = 128
CHUNK = 128
SGU_GROUPS = 4
PEER_HEADS = 8
PEER_N_KEYS = 128
PEER_TOPK = 16
EPS = 1e-6
NEG_INF = -1e30
ALIBI_SLOPES = tuple(2.0 ** (-8.0 * h / N_HEADS) for h in range(1, N_HEADS + 1))

LANES = 128
SUBLANES = 8
MXU_COLS = 256
VMEM_LIMIT = 56 * 1024 * 1024


def _params(*semantics):
    return pltpu.CompilerParams(dimension_semantics=semantics, vmem_limit_bytes=VMEM_LIMIT)


def _token_block(*counts):
    for tb in (512, 256, 128):
        if all(n % tb == 0 for n in counts):
            return tb
    raise ValueError(f"token counts {counts} must be multiples of 128")


def _rms(x, g):
    return x * lax.rsqrt(jnp.mean(x * x, axis=-1, keepdims=True) + EPS) * g


def _group_rms(t, ones_blk, g):
    t2 = t * t
    hi = t2.astype(BF16)
    lo = (t2 - hi.astype(F32)).astype(BF16)
    ss = (jnp.dot(hi, ones_blk, preferred_element_type=F32)
          + jnp.dot(lo, ones_blk, preferred_element_type=F32))
    return t * lax.rsqrt(ss * (1.0 / HEAD_DIM) + EPS) * g


def _inproj_kernel(xp_ref, xs_ref, g_ref, w_ref, qg_ref, kg_ref, lg_ref, lb_ref, bq_ref, bk_ref,
                   q_ref, k_ref, v_ref, u_ref, vn_ref, ga_ref, gb_ref, *, prompt_blocks):
    x = jnp.where(pl.program_id(0) < prompt_blocks, xp_ref[...], xs_ref[...])
    xn = _rms(x, g_ref[...])
    z = jnp.dot(xn.astype(BF16), w_ref[...], preferred_element_type=F32)
    att = N_HEADS * HEAD_DIM
    kvw = N_KV_HEADS * HEAD_DIM
    sgw = (z.shape[1] - att - 2 * kvw) // 6
    o = 0
    q = z[:, o:o + att]; o += att
    k = z[:, o:o + kvw]; o += kvw
    v = z[:, o:o + kvw]; o += kvw
    su = z[:, o:o + sgw]; o += sgw
    sv = z[:, o:o + sgw]; o += sgw
    g_a = z[:, o:o + 2 * sgw]; o += 2 * sgw
    g_b = z[:, o:o + 2 * sgw]
    qn = _group_rms(q, bq_ref[...], qg_ref[...])
    q_ref[...] = (qn * (HEAD_DIM ** -0.5)).astype(BF16)
    k_ref[...] = _group_rms(k, bk_ref[...], kg_ref[...])
    v_ref[...] = v
    u_ref[...] = jax.nn.gelu(su).astype(BF16)
    gv = jax.nn.gelu(sv)
    mu = jnp.mean(gv, axis=-1, keepdims=True)
    gc = gv - mu
    vn_ref[...] = gc * lax.rsqrt(jnp.mean(gc * gc, axis=-1, keepdims=True) + EPS) * lg_ref[...] + lb_ref[...]
    ga_ref[...] = jax.nn.sigmoid(g_a).astype(BF16)
    gb_ref[...] = jax.nn.sigmoid(g_b).astype(BF16)


def _inproj(xp, xs, g, w_in, qg, kg, lg, lb):
    (n_p, d), n_s = xp.shape, xs.shape[0]
    n = n_p + n_s
    tb = _token_block(n_p, n_s)
    nbp = n_p // tb
    att = N_HEADS * HEAD_DIM
    kvw = N_KV_HEADS * HEAD_DIM
    sgw = d // 2
    hid = jnp.arange(att) // HEAD_DIM
    bq = (hid[:, None] == hid[None, :]).astype(BF16)
    bk = bq[:kvw, :kvw]
    const = lambda i: (0, 0)
    row = lambda i: (i, 0)
    outs = [
        jax.ShapeDtypeStruct((n, att), BF16),
        jax.ShapeDtypeStruct((n, kvw), F32),
        jax.ShapeDtypeStruct((n, kvw), F32),
        jax.ShapeDtypeStruct((n, sgw), BF16),
        jax.ShapeDtypeStruct((n, sgw), F32),
        jax.ShapeDtypeStruct((n, d), BF16),
        jax.ShapeDtypeStruct((n, d), BF16),
    ]
    return pl.pallas_call(
        functools.partial(_inproj_kernel, prompt_blocks=nbp),
        grid=(n // tb,),
        in_specs=[
            pl.BlockSpec((tb, d), lambda i: (jnp.minimum(i, nbp - 1), 0)),
            pl.BlockSpec((tb, d), lambda i: (jnp.maximum(i - nbp, 0), 0)),
            pl.BlockSpec((1, d), const),
            pl.BlockSpec(w_in.shape, const),
            pl.BlockSpec((1, att), const),
            pl.BlockSpec((1, kvw), const),
            pl.BlockSpec((1, sgw), const),
            pl.BlockSpec((1, sgw), const),
            pl.BlockSpec((att, att), const),
            pl.BlockSpec((kvw, kvw), const),
        ],
        out_specs=[pl.BlockSpec((tb, s.shape[1]), row) for s in outs],
        out_shape=outs,
        compiler_params=_params("parallel"),
        name="inproj",
    )(xp, xs, g, w_in, qg, kg, lg, lb, bq, bk)


def _sink_softmax(s, sink):
    mx = jnp.maximum(jnp.max(s, axis=-1, keepdims=True), sink)
    p = jnp.exp(s - mx)
    den = jnp.sum(p, axis=-1, keepdims=True) + jnp.exp(sink - mx)
    return p / den


def _prompt_kernel(sinks_ref, q_ref, kc_ref, kp_ref, vc_ref, vp_ref, vn_ref, u_ref, w_ref, bias_ref,
                   a_ref, m_ref):
    i = pl.program_id(1)
    tq = q_ref.shape[0]
    nblk = tq // WINDOW
    q = q_ref[...]
    kc = kc_ref[...].astype(BF16)
    vc = vc_ref[...].astype(BF16)
    kp = kp_ref[...].astype(BF16)
    vp = vp_ref[...].astype(BF16)
    row = lax.broadcasted_iota(jnp.int32, (WINDOW, 2 * WINDOW), 0)
    col = lax.broadcasted_iota(jnp.int32, (WINDOW, 2 * WINDOW), 1)
    dist = row - col + WINDOW
    in_window = (dist >= 0) & (dist < WINDOW)
    distf = dist.astype(F32)
    for jq in range(nblk):
        r0 = jq * WINDOW
        if jq == 0:
            kprev, vprev = kp, vp
            valid = in_window & (col >= jnp.where(i > 0, 0, WINDOW))
        else:
            kprev, vprev = kc[r0 - WINDOW:r0], vc[r0 - WINDOW:r0]
            valid = in_window
        kcat = jnp.concatenate([kprev, kc[r0:r0 + WINDOW]], axis=0)
        vcat = jnp.concatenate([vprev, vc[r0:r0 + WINDOW]], axis=0)
        for g in range(N_KV_HEADS):
            heads = range(g * Q_GROUP, (g + 1) * Q_GROUP)
            qg = jnp.concatenate([q[r0:r0 + WINDOW, h * HEAD_DIM:(h + 1) * HEAD_DIM] for h in heads], axis=0)
            s_all = lax.dot_general(qg, kcat[:, g * HEAD_DIM:(g + 1) * HEAD_DIM],
                                    (((1,), (1,)), ((), ())), preferred_element_type=F32)
            probs = []
            for hl, h in enumerate(heads):
                s = s_all[hl * WINDOW:(hl + 1) * WINDOW] - ALIBI_SLOPES[h] * distf
                s = jnp.where(valid, s, NEG_INF)
                probs.append(_sink_softmax(s, sinks_ref[h]).astype(BF16))
            o_all = jnp.dot(jnp.concatenate(probs, axis=0), vcat[:, g * HEAD_DIM:(g + 1) * HEAD_DIM],
                            preferred_element_type=F32)
            for hl, h in enumerate(heads):
                a_ref[r0:r0 + WINDOW, h * HEAD_DIM:(h + 1) * HEAD_DIM] = (
                    o_all[hl * WINDOW:(hl + 1) * WINDOW].astype(BF16))
    tr = lax.broadcasted_iota(jnp.int32, (CHUNK, CHUNK), 0)
    tc = lax.broadcasted_iota(jnp.int32, (CHUNK, CHUNK), 1)
    gd = vn_ref.shape[1] // SGU_GROUPS
    wm = [jnp.where(tr >= tc, w_ref[g], 0.0).astype(BF16) for g in range(SGU_GROUPS)]
    for c in range(tq // CHUNK):
        r0 = c * CHUNK
        vnc = vn_ref[r0:r0 + CHUNK, :].astype(BF16)
        for g in range(SGU_GROUPS):
            s = jnp.dot(wm[g], vnc[:, g * gd:(g + 1) * gd], preferred_element_type=F32)
            s = s + bias_ref[:, g * gd:(g + 1) * gd]
            m_ref[r0:r0 + CHUNK, g * gd:(g + 1) * gd] = (
                u_ref[r0:r0 + CHUNK, g * gd:(g + 1) * gd].astype(F32) * s).astype(BF16)


def _prompt_mix(sinks, q, k, v, vn, u, sgu_w, sgu_bias, b, s):
    n, att = q.shape
    kvw = k.shape[1]
    sgw = vn.shape[1]
    tq = 512 if s % 512 == 0 else WINDOW
    assert s % tq == 0 and tq % WINDOW == 0 and WINDOW == CHUNK
    r = tq // WINDOW
    nq = s // tq
    cur = lambda bi, i: (bi * nq + i, 0)
    prev = lambda bi, i: (jnp.maximum((bi * nq + i) * r - 1, 0), 0)
    outs = [jax.ShapeDtypeStruct((n, att), BF16), jax.ShapeDtypeStruct((n, sgw), BF16)]
    return pl.pallas_call(
        _prompt_kernel,
        grid=(b, nq),
        in_specs=[
            pl.BlockSpec(memory_space=pltpu.SMEM),
            pl.BlockSpec((tq, att), cur),
            pl.BlockSpec((tq, kvw), cur),
            pl.BlockSpec((WINDOW, kvw), prev),
            pl.BlockSpec((tq, kvw), cur),
            pl.BlockSpec((WINDOW, kvw), prev),
            pl.BlockSpec((tq, sgw), cur),
            pl.BlockSpec((tq, sgw), cur),
            pl.BlockSpec(sgu_w.shape, lambda bi, i: (0, 0, 0)),
            pl.BlockSpec(sgu_bias.shape, lambda bi, i: (0, 0)),
        ],
        out_specs=[pl.BlockSpec((tq, att), cur), pl.BlockSpec((tq, sgw), cur)],
        out_shape=outs,
        compiler_params=_params("parallel", "parallel"),
        name="prompt_mix",
    )(sinks, q, k, k, v, v, vn, u, sgu_w, sgu_bias)


def _sample_kernel(sinks_ref, q_ref, kn_ref, vn_new_ref, ck_ref, cv_ref, vn_ref, u_ref, wexp_ref, bias_ref,
                   a_in_ref, m_in_ref, a_ref, m_ref):
    del a_in_ref, m_in_ref
    bb, w, _ = ck_ref.shape
    l = q_ref.shape[0] // bb
    per_seq = lambda ref: ref[...].astype(F32).reshape(bb, l, ref.shape[1])
    q = per_seq(q_ref)
    kcat = jnp.concatenate([ck_ref[...], per_seq(kn_ref)], axis=1).astype(BF16)
    vcat = jnp.concatenate([cv_ref[...], per_seq(vn_new_ref)], axis=1).astype(BF16)
    rows = Q_GROUP * l
    t = lax.broadcasted_iota(jnp.int32, (rows, w + l), 0) % l
    key = lax.broadcasted_iota(jnp.int32, (rows, w + l), 1)
    dist = t - (key - w)
    valid = (dist >= 0) & (dist < WINDOW)
    distf = dist.astype(F32)
    hl_of_row = lax.broadcasted_iota(jnp.int32, (rows, 1), 0) // l
    for g in range(N_KV_HEADS):
        heads = range(g * Q_GROUP, (g + 1) * Q_GROUP)
        qg = jnp.concatenate([q[:, :, h * HEAD_DIM:(h + 1) * HEAD_DIM] for h in heads], axis=1)
        s = jnp.einsum('bqd,bkd->bqk', qg.astype(BF16), kcat[:, :, g * HEAD_DIM:(g + 1) * HEAD_DIM],
                       preferred_element_type=F32)
        slope = jnp.zeros((rows, 1), F32)
        sink = jnp.zeros((rows, 1), F32)
        for hl, h in enumerate(heads):
            slope = jnp.where(hl_of_row == hl, ALIBI_SLOPES[h], slope)
            sink = jnp.where(hl_of_row == hl, sinks_ref[h], sink)
        s = jnp.where(valid[None], s - (slope * distf)[None], NEG_INF)
        p = _sink_softmax(s, sink[None]).astype(BF16)
        o = jnp.einsum('bqk,bkd->bqd', p, vcat[:, :, g * HEAD_DIM:(g + 1) * HEAD_DIM],
                       preferred_element_type=F32)
        for hl, h in enumerate(heads):
            a_ref[:, h * HEAD_DIM:(h + 1) * HEAD_DIM] = (
                o[:, hl * l:(hl + 1) * l, :].reshape(bb * l, HEAD_DIM).astype(BF16))
    vn = per_seq(vn_ref)
    tt = lax.broadcasted_iota(jnp.int32, (l, vn.shape[2]), 0)
    s = jnp.broadcast_to(bias_ref[...][None], vn.shape)
    for sp in range(l):
        wm = jnp.where(tt >= sp, wexp_ref[sp], 0.0)
        s = s + wm[None] * vn[:, sp:sp + 1, :]
    m_ref[...] = (per_seq(u_ref) * s).reshape(bb * l, vn.shape[2]).astype(BF16)


def _sample_mix(sinks, q, k, v, cache_k, cache_v, vn, u, wexp, bias, a, m, n_p):
    b, w, kvw = cache_k.shape
    n, att = q.shape
    l = (n - n_p) // b
    sgw = vn.shape[1]
    bb = 16 if b % 16 == 0 else b
    assert n_p % (bb * l) == 0
    first = n_p // (bb * l)
    tok = lambda width: pl.BlockSpec((bb * l, width), lambda i: (first + i, 0))
    past = pl.BlockSpec((bb, w, kvw), lambda i: (i, 0, 0))
    whole = pl.BlockSpec(memory_space=pl.ANY)
    outs = [jax.ShapeDtypeStruct(a.shape, a.dtype), jax.ShapeDtypeStruct(m.shape, m.dtype)]
    return pl.pallas_call(
        _sample_kernel,
        grid=(b // bb,),
        in_specs=[
            pl.BlockSpec(memory_space=pltpu.SMEM),
            tok(att), tok(kvw), tok(kvw), past, past, tok(sgw), tok(sgw),
            pl.BlockSpec(wexp.shape, lambda i: (0, 0, 0)),
            pl.BlockSpec(bias.shape, lambda i: (0, 0)),
            whole, whole,
        ],
        out_specs=[tok(att), tok(sgw)],
        out_shape=outs,
        input_output_aliases={10: 0, 11: 1},
        compiler_params=_params("parallel"),
        name="sample_mix",
    )(sinks, q, k, v, cache_k, cache_v, vn, u, wexp, bias, a, m)


def _merge_kernel(x_ref, a_ref, m_ref, ga_ref, gb_ref, wa_ref, wb_ref, wo_ref, fg_ref, wq_ref, keys_ref,
                  x1_ref, xn_ref, s1_ref, s2_ref):
    ha = jnp.dot(a_ref[...], wa_ref[...], preferred_element_type=F32)
    hb = jnp.dot(m_ref[...], wb_ref[...], preferred_element_type=F32)
    h = ga_ref[...].astype(F32) * ha + gb_ref[...].astype(F32) * hb
    x1 = x_ref[...] + jnp.dot(h.astype(BF16), wo_ref[...], preferred_element_type=F32)
    x1_ref[...] = x1
    xn = _rms(x1, fg_ref[...]).astype(BF16)
    xn_ref[...] = xn
    qp = jnp.dot(xn, wq_ref[...], preferred_element_type=F32).astype(BF16)
    half = keys_ref.shape[2]
    nsub = s1_ref.shape[0]
    for hc in range(keys_ref.shape[0]):
        st = lax.dot_general(keys_ref[hc], qp[:, hc * half:(hc + 1) * half],
                             (((1,), (1,)), ((), ())), preferred_element_type=F32)
        dst = s1_ref if hc % 2 == 0 else s2_ref
        for tl in range(nsub):
            dst[tl, hc // 2] = st[:, tl * LANES:(tl + 1) * LANES]


def _merge(xp, xs, a, m, ga, gb, wa, wb, wo, fg, wq, keys):
    (n_p, d), n_s = xp.shape, xs.shape[0]
    n = n_p + n_s
    tb = _token_block(n_p, n_s)
    nbp = n_p // tb
    nsub = tb // LANES
    hc, nk, half = keys.shape
    row = lambda i: (i, 0)
    const2 = lambda i: (0, 0)
    heads = hc // 2
    outs = [
        jax.ShapeDtypeStruct((n, d), F32),
        jax.ShapeDtypeStruct((n, d), BF16),
        jax.ShapeDtypeStruct((n // LANES, heads, nk, LANES), F32),
        jax.ShapeDtypeStruct((n // LANES, heads, nk, LANES), F32),
    ]

    def body(xp_ref, xs_ref, *rest):
        x = jnp.where(pl.program_id(0) < nbp, xp_ref[...], xs_ref[...])
        _merge_kernel(_Value(x), *rest)

    return pl.pallas_call(
        body,
        grid=(n // tb,),
        in_specs=[
            pl.BlockSpec((tb, d), lambda i: (jnp.minimum(i, nbp - 1), 0)),
            pl.BlockSpec((tb, d), lambda i: (jnp.maximum(i - nbp, 0), 0)),
            pl.BlockSpec((tb, a.shape[1]), row),
            pl.BlockSpec((tb, m.shape[1]), row),
            pl.BlockSpec((tb, d), row),
            pl.BlockSpec((tb, d), row),
            pl.BlockSpec(wa.shape, const2),
            pl.BlockSpec(wb.shape, const2),
            pl.BlockSpec(wo.shape, const2),
            pl.BlockSpec((1, d), const2),
            pl.BlockSpec(wq.shape, const2),
            pl.BlockSpec(keys.shape, lambda i: (0, 0, 0)),
        ],
        out_specs=[
            pl.BlockSpec((tb, d), row),
            pl.BlockSpec((tb, d), row),
            pl.BlockSpec((nsub, heads, nk, LANES), lambda i: (i, 0, 0, 0)),
            pl.BlockSpec((nsub, heads, nk, LANES), lambda i: (i, 0, 0, 0)),
        ],
        out_shape=outs,
        compiler_params=_params("parallel"),
        name="merge",
    )(xp, xs, a, m, ga, gb, wa, wb, wo, fg, wq, keys)


class _Value:
    def __init__(self, value):
        self._value = value

    def __getitem__(self, idx):
        return self._value[idx]


def _oddeven_merge(lo, hi, r):
    step = r * 2
    if step < hi - lo:
        yield from _oddeven_merge(lo, hi, step)
        yield from _oddeven_merge(lo + r, hi, step)
        yield from [(i, i + r) for i in range(lo + r, hi - r, step)]
    else:
        yield (lo, lo + r)


def _oddeven_merge_sort(lo, hi):
    if hi - lo >= 1:
        mid = lo + (hi - lo) // 2
        yield from _oddeven_merge_sort(lo, mid)
        yield from _oddeven_merge_sort(mid + 1, hi)
        yield from _oddeven_merge(lo, hi, 1)


_SORT_TOPK = tuple(_oddeven_merge_sort(0, PEER_TOPK - 1))


def _cmpx(w, i, j):
    a, b = w[i], w[j]
    if b is None:
        return
    if a is None:
        w[i], w[j] = b, None
        return
    w[i], w[j] = jnp.maximum(a, b), jnp.minimum(a, b)


def _top_values(w):
    k = PEER_TOPK
    w = list(w)
    for i, j in _SORT_TOPK:
        _cmpx(w, i, j)
    shift = SUBLANES // 2
    while shift >= 1:
        y = [None if v is None else pltpu.roll(v, shift, 0) for v in w]
        z = []
        for r in range(k):
            a, b = w[r], y[k - 1 - r]
            z.append(b if a is None else a if b is None else jnp.maximum(a, b))
        stride = k // 2
        while stride >= 1:
            for i in range(k):
                if i & stride == 0:
                    _cmpx(z, i, i + stride)
            stride //= 2
        w = z
        shift //= 2
    return w


def _thresh_kernel(s1_ref, s2_ref, g2_ref, gm_ref, g1_ref):
    k = PEER_TOPK
    nk, lanes = s1_ref.shape[2], s1_ref.shape[3]
    nslot = nk // SUBLANES
    assert nslot == k and k == 2 * SUBLANES
    sub = lax.broadcasted_iota(jnp.int32, (SUBLANES, lanes), 0)

    def pack(vals):
        out = vals[0]
        for j in range(1, SUBLANES):
            out = jnp.where(sub == j, vals[j], out)
        return out

    def head(h, carry):
        w1 = [s1_ref[0, h, r * SUBLANES:(r + 1) * SUBLANES, :] for r in range(nslot)]
        w2 = [s2_ref[0, h, r * SUBLANES:(r + 1) * SUBLANES, :] for r in range(nslot)]
        a = _top_values(w1)
        b = _top_values(w2)
        b_lo, b_hi, a_hi = pack(b[:SUBLANES]), pack(b[SUBLANES:]), pack(a[SUBLANES:])
        cands = ([a[0] + b_lo, a[0] + b_hi] + [a[i] + b_lo for i in range(1, SUBLANES)] + [a_hi + b[0]])
        best = _top_values(cands + [None] * (k - len(cands)))
        tau = best[k - 1]
        z = jnp.ones_like(tau)
        for r in range(1, k):
            z = z + jnp.exp(best[r] - best[0])
        inv_z = 1.0 / z
        eb = [jnp.exp(b[j] - b[0]) for j in range(k)]
        gamma = []
        for i in range(k):
            t = jnp.full_like(tau, jnp.inf)
            for j in range(k // (i + 1)):
                t = jnp.where(a[i] + b[j] >= tau, eb[j], t)
            gamma.append(t)
        for r in range(nslot):
            gm = jnp.full_like(tau, jnp.inf)
            for i in range(k - 1, -1, -1):
                gm = jnp.where(w1[r] >= a[i], gamma[i], gm)
            rows = pl.ds(r * SUBLANES, SUBLANES)
            gm_ref[0, h, rows, :] = gm
            g1_ref[0, h, rows, :] = jnp.exp(w1[r] - a[0]) * (0.5 * inv_z)
            g2_ref[0, h, rows, :] = jnp.exp(w2[r] - b[0])
        return carry

    lax.fori_loop(0, s1_ref.shape[1], head, 0)


def _thresholds(s1, s2):
    nsub, heads, nk, lanes = s1.shape
    spec = pl.BlockSpec((1, heads, nk, lanes), lambda i: (i, 0, 0, 0))
    out = jax.ShapeDtypeStruct(s1.shape, F32)
    return pl.pallas_call(
        _thresh_kernel,
        grid=(nsub,),
        in_specs=[spec, spec],
        out_specs=[spec, spec, spec],
        out_shape=[out, out, out],
        compiler_params=_params("parallel"),
        name="peer_thresholds",
    )(s1, s2)


GATE_ROWS = 64
GELU_C0 = math.sqrt(2.0 / math.pi)
GELU_C1 = 0.044715 * GELU_C0
PEER_CHUNK = 512
PEER_TOKENS = 1024


def _gate_tasks(at_ref, wt_ref, j0, g2_ref, gm_ref, g1_ref):
    nsub, heads, nk, lanes = g2_ref.shape

    def tile(j, tl, r0):
        gate = jnp.zeros((GATE_ROWS, lanes), F32)
        for h in range(heads):
            gm = gm_ref[tl, h, j0 + j:j0 + j + 1, :]
            g1 = g1_ref[tl, h, j0 + j:j0 + j + 1, :]
            g2 = g2_ref[tl, h, r0:r0 + GATE_ROWS, :]
            gate = gate + jnp.where(g2 >= gm, g1 * g2, 0.0)
        rows = slice(j * nk + r0, j * nk + r0 + GATE_ROWS)
        cols = slice(tl * lanes, (tl + 1) * lanes)
        x = at_ref[rows, cols]
        t = jnp.tanh(x * (GELU_C0 + GELU_C1 * (x * x)))
        wt_ref[rows, cols] = ((x + x * t) * gate).astype(BF16)

    return [functools.partial(tile, j, tl, r0)
            for tl in range(nsub) for j in range(at_ref.shape[0] // nk) for r0 in range(0, nk, GATE_ROWS)]


def _interleave(main, *others):
    slots = [[] for _ in range(len(main) + 1)]
    for oi, tasks in enumerate(others):
        for k, t in enumerate(tasks):
            slots[((k * len(others) + oi) * len(main)) // (len(tasks) * len(others))].append(t)
    order = []
    for k, m in enumerate(main):
        order.extend(slots[k])
        order.append(m)
    return order


def _peer_kernel(xn_ref, u0_ref, ub_ref, ua_ref, vp_ref, va_ref, g2_ref, gm_ref, g1_ref, out_ref,
                 at0, at1, wt0, wt1, acc):
    s = pl.program_id(1)
    last = pl.num_programs(1) - 1
    per = at0.shape[0] // g2_ref.shape[2]
    tb = xn_ref.shape[0]
    tables = (g2_ref, gm_ref, g1_ref)
    nt = (((1,), (1,)), ((), ()))
    col_tiles = [slice(c, c + min(MXU_COLS, tb)) for c in range(0, tb, min(MXU_COLS, tb))]

    def act_tasks(u_ref, at):
        def piece(cs):
            at[:, cs] = lax.dot_general(u_ref[...], xn_ref[cs, :], nt, preferred_element_type=F32)
        return [functools.partial(piece, cs) for cs in col_tiles]

    def mix_tasks(v_ref, wt):
        def piece(cs):
            acc[:, cs] += jnp.dot(v_ref[...], wt[:, cs], preferred_element_type=F32)
        return [functools.partial(piece, cs) for cs in col_tiles]

    @pl.when(s == 0)
    def _():
        for t in act_tasks(u0_ref, at0):
            t()
        wt1[...] = jnp.zeros_like(wt1)
        acc[...] = jnp.zeros_like(acc)

    @pl.when(s < last)
    def _():
        for t in _interleave(_gate_tasks(at0, wt0, 0, *tables),
                             act_tasks(ub_ref, at1), mix_tasks(vp_ref, wt1)):
            t()
        for t in _interleave(_gate_tasks(at1, wt1, per, *tables),
                             act_tasks(ua_ref, at0), mix_tasks(va_ref, wt0)):
            t()

    @pl.when(s == last)
    def _():
        for t in mix_tasks(vp_ref, wt1):
            t()
        out_ref[...] = acc[...].T


def _peer(xn, u, vt, g2, gm, g1):
    n, d = xn.shape
    ne = u.shape[0]
    tb = PEER_TOKENS if n % PEER_TOKENS == 0 else _token_block(n)
    nsub = tb // LANES
    _, heads, nk, _ = g2.shape
    ec = PEER_CHUNK
    assert ne % (2 * ec) == 0 and ec % nk == 0 and ne == nk * nk and nk % GATE_ROWS == 0
    nc = ne // ec
    steps = nc // 2
    per = ec // nk
    keyed2 = pl.BlockSpec((nsub, heads, nk, LANES), lambda i, s: (i, 0, 0, 0))
    keyed1 = pl.BlockSpec((nsub, heads, 2 * per, LANES), lambda i, s: (i, 0, jnp.minimum(s, steps - 1), 0))
    return pl.pallas_call(
        _peer_kernel,
        grid=(n // tb, steps + 1),
        in_specs=[
            pl.BlockSpec((tb, d), lambda i, s: (i, 0)),
            pl.BlockSpec((ec, d), lambda i, s: (0, 0)),
            pl.BlockSpec((ec, d), lambda i, s: (jnp.minimum(2 * s + 1, nc - 1), 0)),
            pl.BlockSpec((ec, d), lambda i, s: (jnp.minimum(2 * s + 2, nc - 2), 0)),
            pl.BlockSpec((d, ec), lambda i, s: (0, jnp.maximum(2 * s - 1, 0))),
            pl.BlockSpec((d, ec), lambda i, s: (0, jnp.minimum(2 * s, nc - 2))),
            keyed2, keyed1, keyed1,
        ],
        out_specs=pl.BlockSpec((tb, d), lambda i, s: (i, 0)),
        out_shape=jax.ShapeDtypeStruct((n, d), F32),
        scratch_shapes=[
            pltpu.VMEM((ec, tb), F32),
            pltpu.VMEM((ec, tb), F32),
            pltpu.VMEM((ec, tb), BF16),
            pltpu.VMEM((ec, tb), BF16),
            pltpu.VMEM((d, tb), F32),
        ],
        compiler_params=_params("parallel", "arbitrary"),
        name="peer_mix",
    )(xn, u, u, u, vt, vt, g2, gm, g1)


def _final_kernel(x1_ref, pe_ref, p_ref, g_ref, wg_ref, wp_ref, y_ref):
    x2 = x1_ref[...] + pe_ref[...]
    xn = _rms(x2, g_ref[...]).astype(BF16)
    gate = jax.nn.sigmoid(jnp.dot(xn, wg_ref[...], preferred_element_type=F32))
    y_ref[...] = x2 + gate * jnp.dot(p_ref[...].astype(BF16), wp_ref[...], preferred_element_type=F32)


def _final(x1, pe, p, g, wg, wp, row0):
    n, pd = p.shape
    d = x1.shape[1]
    tb = _token_block(n, row0)
    first = row0 // tb
    src = lambda i: (first + i, 0)
    row = lambda i: (i, 0)
    const = lambda i: (0, 0)
    return pl.pallas_call(
        _final_kernel,
        grid=(n // tb,),
        in_specs=[
            pl.BlockSpec((tb, d), src),
            pl.BlockSpec((tb, d), src),
            pl.BlockSpec((tb, pd), row),
            pl.BlockSpec((1, d), const),
            pl.BlockSpec(wg.shape, const),
            pl.BlockSpec(wp.shape, const),
        ],
        out_specs=pl.BlockSpec((tb, d), row),
        out_shape=jax.ShapeDtypeStruct((n, d), F32),
        compiler_params=_params("parallel"),
        name="ple_epilogue",
    )(x1, pe, p, g, wg, wp)


def _layer(xp, xs, pp, ps, past_k, past_v, lp):
    b, s, d = xp.shape
    bd, l, _ = xs.shape
    n_p, n_s = b * s, bd * l
    kvw = N_KV_HEADS * HEAD_DIM
    sgw = d // 2
    gd = sgw // SGU_GROUPS
    xp2, xs2 = xp.reshape(n_p, d), xs.reshape(n_s, d)

    q, k, v, u, vn, ga, gb = _inproj(
        xp2, xs2, lp['attn_norm_g'][None], lp['w_in'].astype(BF16),
        jnp.tile(lp['q_norm_g'], N_HEADS)[None], jnp.tile(lp['k_norm_g'], N_KV_HEADS)[None],
        lp['sgu_norm_g'][None], lp['sgu_norm_b'][None])

    sgu_w, sgu_b = lp['sgu_w'], lp['sgu_b']
    bias_p = jnp.repeat(sgu_b.T, gd, axis=1)
    a, m = _prompt_mix(lp['attn_sinks'], q, k, v, vn, u, sgu_w, bias_p, b, s)
    wexp = jnp.repeat(jnp.transpose(sgu_w[:, :l, :l], (2, 1, 0)), gd, axis=2)
    a, m = _sample_mix(lp['attn_sinks'], q, k, v, past_k.reshape(bd, -1, kvw), past_v.reshape(bd, -1, kvw),
                       vn, u, wexp, bias_p[:l], a, m, n_p)

    keys = lp['peer_sub_keys'].reshape(2 * PEER_HEADS, PEER_N_KEYS, -1).astype(BF16)
    x1, xn1, s1, s2 = _merge(xp2, xs2, a, m, ga, gb, lp['w_branch_a'].astype(BF16),
                             lp['w_branch_b'].astype(BF16), lp['w_out'].astype(BF16), lp['ffn_norm_g'][None],
                             lp['peer_w_q'].astype(BF16), keys)
    g2, gm, g1 = _thresholds(s1, s2)
    pe = _peer(xn1, lp['peer_u'].astype(BF16), lp['peer_v'].astype(BF16).T, g2, gm, g1)
    ple = (lp['ple_norm_g'][None], lp['w_ple_gate'].astype(BF16), lp['w_ple'].astype(BF16))
    y_p = _final(x1, pe, pp.reshape(n_p, -1), *ple, 0)
    y_s = _final(x1, pe, ps.reshape(n_s, -1), *ple, n_p)

    wp = min(WINDOW, s)
    tail = lambda t, rows: t[:n_p].reshape(b, s, -1)[:, s - rows:]
    heads = lambda t: t.reshape(t.shape[0], t.shape[1], N_KV_HEADS, HEAD_DIM)
    return (y_p.reshape(b, s, d), y_s.reshape(bd, l, d),
            heads(tail(k, wp)), heads(tail(v, wp)),
            heads(k[n_p:].reshape(bd, l, kvw)), heads(v[n_p:].reshape(bd, l, kvw)),
            tail(vn, CHUNK), vn[n_p:].reshape(bd, l, sgw))


def kernel(x_prompt, x_sample, cache_k, cache_v, p_prompt, p_sample, attn_norm_g, w_in, q_norm_g, k_norm_g, attn_sinks, sgu_norm_g, sgu_norm_b, sgu_w, sgu_b, w_branch_a, w_branch_b, w_out, ffn_norm_g, peer_w_q, peer_sub_keys, peer_u, peer_v, ple_norm_g, w_ple, w_ple_gate):
    depth = w_in.shape[0]
    hp, hs = x_prompt, x_sample
    outs = [[] for _ in range(6)]
    for i in range(depth):
        lp = dict(attn_norm_g=attn_norm_g[i], w_in=w_in[i], q_norm_g=q_norm_g[i], k_norm_g=k_norm_g[i],
                  attn_sinks=attn_sinks[i], sgu_norm_g=sgu_norm_g[i], sgu_norm_b=sgu_norm_b[i],
                  sgu_w=sgu_w[i], sgu_b=sgu_b[i], w_branch_a=w_branch_a[i], w_branch_b=w_branch_b[i],
                  w_out=w_out[i], ffn_norm_g=ffn_norm_g[i], peer_w_q=peer_w_q[i],
                  peer_sub_keys=peer_sub_keys[i], peer_u=peer_u[i], peer_v=peer_v[i],
                  ple_norm_g=ple_norm_g[i], w_ple=w_ple[i], w_ple_gate=w_ple_gate[i])
        res = _layer(hp, hs, p_prompt[i], p_sample[i], cache_k[i], cache_v[i], lp)
        hp, hs = res[0], res[1]
        for lst, t in zip(outs, res[2:]):
            lst.append(t)
    return (hp, hs) + tuple(jnp.stack(o) for o in outs)
```

```python
import functools
import math

import jax
import jax.numpy as jnp
from jax import lax
from jax.experimental import pallas as pl
from jax.experimental.pallas import tpu as pltpu

F32 = jnp.float32
BF16 = jnp.bfloat16

N_HEADS = 8
N_KV_HEADS = 2
HEAD_DIM = 64
Q_GROUP = N_HEADS // N_KV_HEADS
WINDOW = 128
CHUNK = 128
SGU_GROUPS = 4
PEER_HEADS = 8
PEER_N_KEYS = 128
PEER_TOPK = 16
EPS = 1e-6
NEG_INF = -1e30
ALIBI_SLOPES = tuple(2.0 ** (-8.0 * h / N_HEADS) for h in range(1, N_HEADS + 1))

LANES = 128
SUBLANES = 8
MXU_COLS = 256
VMEM_LIMIT = 56 * 1024 * 1024


def _params(*semantics):
    return pltpu.CompilerParams(dimension_semantics=semantics, vmem_limit_bytes=VMEM_LIMIT)


def _token_block(*counts):
    for tb in (512, 256, 128):
        if all(n % tb == 0 for n in counts):
            return tb
    raise ValueError(f"token counts {counts} must be multiples of 128")


def _rms(x, g):
    return x * lax.rsqrt(jnp.mean(x * x, axis=-1, keepdims=True) + EPS) * g


def _group_rms(t, ones_blk, g):
    t2 = t * t
    hi = t2.astype(BF16)
    lo = (t2 - hi.astype(F32)).astype(BF16)
    ss = (jnp.dot(hi, ones_blk, preferred_element_type=F32)
          + jnp.dot(lo, ones_blk, preferred_element_type=F32))
    return t * lax.rsqrt(ss * (1.0 / HEAD_DIM) + EPS) * g


def _inproj_kernel(xp_ref, xs_ref, g_ref, w_ref, qg_ref, kg_ref, lg_ref, lb_ref, bq_ref, bk_ref,
                   q_ref, k_ref, v_ref, u_ref, vn_ref, ga_ref, gb_ref, *, prompt_blocks):
    x = jnp.where(pl.program_id(0) < prompt_blocks, xp_ref[...], xs_ref[...])
    xn = _rms(x, g_ref[...])
    z = jnp.dot(xn.astype(BF16), w_ref[...], preferred_element_type=F32)
    att = N_HEADS * HEAD_DIM
    kvw = N_KV_HEADS * HEAD_DIM
    sgw = (z.shape[1] - att - 2 * kvw) // 6
    o = 0
    q = z[:, o:o + att]; o += att
    k = z[:, o:o + kvw]; o += kvw
    v = z[:, o:o + kvw]; o += kvw
    su = z[:, o:o + sgw]; o += sgw
    sv = z[:, o:o + sgw]; o += sgw
    g_a = z[:, o:o + 2 * sgw]; o += 2 * sgw
    g_b = z[:, o:o + 2 * sgw]
    qn = _group_rms(q, bq_ref[...], qg_ref[...])
    q_ref[...] = (qn * (HEAD_DIM ** -0.5)).astype(BF16)
    k_ref[...] = _group_rms(k, bk_ref[...], kg_ref[...])
    v_ref[...] = v
    u_ref[...] = jax.nn.gelu(su).astype(BF16)
    gv = jax.nn.gelu(sv)
    mu = jnp.mean(gv, axis=-1, keepdims=True)
    gc = gv - mu
    vn_ref[...] = gc * lax.rsqrt(jnp.mean(gc * gc, axis=-1, keepdims=True) + EPS) * lg_ref[...] + lb_ref[...]
    ga_ref[...] = jax.nn.sigmoid(g_a).astype(BF16)
    gb_ref[...] = jax.nn.sigmoid(g_b).astype(BF16)


def _inproj(xp, xs, g, w_in, qg, kg, lg, lb):
    (n_p, d), n_s = xp.shape, xs.shape[0]
    n = n_p + n_s
    tb = _token_block(n_p, n_s)
    nbp = n_p // tb
    att = N_HEADS * HEAD_DIM
    kvw = N_KV_HEADS * HEAD_DIM
    sgw = d // 2
    hid = jnp.arange(att) // HEAD_DIM
    bq = (hid[:, None] == hid[None, :]).astype(BF16)
    bk = bq[:kvw, :kvw]
    const = lambda i: (0, 0)
    row = lambda i: (i, 0)
    outs = [
        jax.ShapeDtypeStruct((n, att), BF16),
        jax.ShapeDtypeStruct((n, kvw), F32),
        jax.ShapeDtypeStruct((n, kvw), F32),
        jax.ShapeDtypeStruct((n, sgw), BF16),
        jax.ShapeDtypeStruct((n, sgw), F32),
        jax.ShapeDtypeStruct((n, d), BF16),
        jax.ShapeDtypeStruct((n, d), BF16),
    ]
    return pl.pallas_call(
        functools.partial(_inproj_kernel, prompt_blocks=nbp),
        grid=(n // tb,),
        in_specs=[
            pl.BlockSpec((tb, d), lambda i: (jnp.minimum(i, nbp - 1), 0)),
            pl.BlockSpec((tb, d), lambda i: (jnp.maximum(i - nbp, 0), 0)),
            pl.BlockSpec((1, d), const),
            pl.BlockSpec(w_in.shape, const),
            pl.BlockSpec((1, att), const),
            pl.BlockSpec((1, kvw), const),
            pl.BlockSpec((1, sgw), const),
            pl.BlockSpec((1, sgw), const),
            pl.BlockSpec((att, att), const),
            pl.BlockSpec((kvw, kvw), const),
        ],
        out_specs=[pl.BlockSpec((tb, s.shape[1]), row) for s in outs],
        out_shape=outs,
        compiler_params=_params("parallel"),
        name="inproj",
    )(xp, xs, g, w_in, qg, kg, lg, lb, bq, bk)


def _sink_softmax(s, sink):
    mx = jnp.maximum(jnp.max(s, axis=-1, keepdims=True), sink)
    p = jnp.exp(s - mx)
    den = jnp.sum(p, axis=-1, keepdims=True) + jnp.exp(sink - mx)
    return p / den


def _prompt_kernel(sinks_ref, q_ref, kc_ref, kp_ref, vc_ref, vp_ref, vn_ref, u_ref, w_ref, bias_ref,
                   a_ref, m_ref):
    i = pl.program_id(1)
    tq = q_ref.shape[0]
    nblk = tq // WINDOW
    q = q_ref[...]
    kc = kc_ref[...].astype(BF16)
    vc = vc_ref[...].astype(BF16)
    kp = kp_ref[...].astype(BF16)
    vp = vp_ref[...].astype(BF16)
    row = lax.broadcasted_iota(jnp.int32, (WINDOW, 2 * WINDOW), 0)
    col = lax.broadcasted_iota(jnp.int32, (WINDOW, 2 * WINDOW), 1)
    dist = row - col + WINDOW
    in_window = (dist >= 0) & (dist < WINDOW)
    distf = dist.astype(F32)
    for jq in range(nblk):
        r0 = jq * WINDOW
        if jq == 0:
            kprev, vprev = kp, vp
            valid = in_window & (col >= jnp.where(i > 0, 0, WINDOW))
        else:
            kprev, vprev = kc[r0 - WINDOW:r0], vc[r0 - WINDOW:r0]
            valid = in_window
        kcat = jnp.concatenate([kprev, kc[r0:r0 + WINDOW]], axis=0)
        vcat = jnp.concatenate([vprev, vc[r0:r0 + WINDOW]], axis=0)
        for g in range(N_KV_HEADS):
            heads = range(g * Q_GROUP, (g + 1) * Q_GROUP)
            qg = jnp.concatenate([q[r0:r0 + WINDOW, h * HEAD_DIM:(h + 1) * HEAD_DIM] for h in heads], axis=0)
            s_all = lax.dot_general(qg, kcat[:, g * HEAD_DIM:(g + 1) * HEAD_DIM],
                                    (((1,), (1,)), ((), ())), preferred_element_type=F32)
            probs = []
            for hl, h in enumerate(heads):
                s = s_all[hl * WINDOW:(hl + 1) * WINDOW] - ALIBI_SLOPES[h] * distf
                s = jnp.where(valid, s, NEG_INF)
                probs.append(_sink_softmax(s, sinks_ref[h]).astype(BF16))
            o_all = jnp.dot(jnp.concatenate(probs, axis=0), vcat[:, g * HEAD_DIM:(g + 1) * HEAD_DIM],
                            preferred_element_type=F32)
            for hl, h in enumerate(heads):
                a_ref[r0:r0 + WINDOW, h * HEAD_DIM:(h + 1) * HEAD_DIM] = (
                    o_all[hl * WINDOW:(hl + 1) * WINDOW].astype(BF16))
    tr = lax.broadcasted_iota(jnp.int32, (CHUNK, CHUNK), 0)
    tc = lax.broadcasted_iota(jnp.int32, (CHUNK, CHUNK), 1)
    gd = vn_ref.shape[1] // SGU_GROUPS
    wm = [jnp.where(tr >= tc, w_ref[g], 0.0).astype(BF16) for g in range(SGU_GROUPS)]
    for c in range(tq // CHUNK):
        r0 = c * CHUNK
        vnc = vn_ref[r0:r0 + CHUNK, :].astype(BF16)
        for g in range(SGU_GROUPS):
            s = jnp.dot(wm[g], vnc[:, g * gd:(g + 1) * gd], preferred_element_type=F32)
            s = s + bias_ref[:, g * gd:(g + 1) * gd]
            m_ref[r0:r0 + CHUNK, g * gd:(g + 1) * gd] = (
                u_ref[r0:r0 + CHUNK, g * gd:(g + 1) * gd].astype(F32) * s).astype(BF16)


def _prompt_mix(sinks, q, k, v, vn, u, sgu_w, sgu_bias, b, s):
    n, att = b * s, q.shape[1]
    kvw = k.shape[1]
    sgw = vn.shape[1]
    tq = 512 if s % 512 == 0 else WINDOW
    assert s % tq == 0 and tq % WINDOW == 0 and WINDOW == CHUNK
    r = tq // WINDOW
    nq = s // tq
    cur = lambda bi, i: (bi * nq + i, 0)
    prev = lambda bi, i: (jnp.maximum((bi * nq + i) * r - 1, 0), 0)
    outs = [jax.ShapeDtypeStruct((n, att), BF16), jax.ShapeDtypeStruct((n, sgw), BF16)]
    return pl.pallas_call(
        _prompt_kernel,
        grid=(b, nq),
        in_specs=[
            pl.BlockSpec(memory_space=pltpu.SMEM),
            pl.BlockSpec((tq, att), cur),
            pl.BlockSpec((tq, kvw), cur),
            pl.BlockSpec((WINDOW, kvw), prev),
            pl.BlockSpec((tq, kvw), cur),
            pl.BlockSpec((WINDOW, kvw), prev),
            pl.BlockSpec((tq, sgw), cur),
            pl.BlockSpec((tq, sgw), cur),
            pl.BlockSpec(sgu_w.shape, lambda bi, i: (0, 0, 0)),
            pl.BlockSpec(sgu_bias.shape, lambda bi, i: (0, 0)),
        ],
        out_specs=[pl.BlockSpec((tq, att), cur), pl.BlockSpec((tq, sgw), cur)],
        out_shape=outs,
        compiler_params=_params("parallel", "parallel"),
        name="prompt_mix",
    )(sinks, q, k, k, v, v, vn, u, sgu_w, sgu_bias)


def _sample_kernel(sinks_ref, q_ref, kn_ref, vn_new_ref, ck_ref, cv_ref, vn_ref, u_ref, wexp_ref, bias_ref,
                   a_ref, m_ref):
    bb, w, _ = ck_ref.shape
    l = q_ref.shape[0] // bb
    per_seq = lambda ref: ref[...].astype(F32).reshape(bb, l, ref.shape[1])
    q = per_seq(q_ref)
    kcat = jnp.concatenate([ck_ref[...], per_seq(kn_ref)], axis=1).astype(BF16)
    vcat = jnp.concatenate([cv_ref[...], per_seq(vn_new_ref)], axis=1).astype(BF16)
    rows = Q_GROUP * l
    t = lax.broadcasted_iota(jnp.int32, (rows, w + l), 0) % l
    key = lax.broadcasted_iota(jnp.int32, (rows, w + l), 1)
    dist = t - (key - w)
    valid = (dist >= 0) & (dist < WINDOW)
    distf = dist.astype(F32)
    hl_of_row = lax.broadcasted_iota(jnp.int32, (rows, 1), 0) // l
    for g in range(N_KV_HEADS):
        heads = range(g * Q_GROUP, (g + 1) * Q_GROUP)
        qg = jnp.concatenate([q[:, :, h * HEAD_DIM:(h + 1) * HEAD_DIM] for h in heads], axis=1)
        s = jnp.einsum('bqd,bkd->bqk', qg.astype(BF16), kcat[:, :, g * HEAD_DIM:(g + 1) * HEAD_DIM],
                       preferred_element_type=F32)
        slope = jnp.zeros((rows, 1), F32)
        sink = jnp.zeros((rows, 1), F32)
        for hl, h in enumerate(heads):
            slope = jnp.where(hl_of_row == hl, ALIBI_SLOPES[h], slope)
            sink = jnp.where(hl_of_row == hl, sinks_ref[h], sink)
        s = jnp.where(valid[None], s - (slope * distf)[None], NEG_INF)
        p = _sink_softmax(s, sink[None]).astype(BF16)
        o = jnp.einsum('bqk,bkd->bqd', p, vcat[:, :, g * HEAD_DIM:(g + 1) * HEAD_DIM],
                       preferred_element_type=F32)
        for hl, h in enumerate(heads):
            a_ref[:, h * HEAD_DIM:(h + 1) * HEAD_DIM] = (
                o[:, hl * l:(hl + 1) * l, :].reshape(bb * l, HEAD_DIM).astype(BF16))
    vn = per_seq(vn_ref)
    tt = lax.broadcasted_iota(jnp.int32, (l, vn.shape[2]), 0)
    s = jnp.broadcast_to(bias_ref[...][None], vn.shape)
    for sp in range(l):
        wm = jnp.where(tt >= sp, wexp_ref[sp], 0.0)
        s = s + wm[None] * vn[:, sp:sp + 1, :]
    m_ref[...] = (per_seq(u_ref) * s).reshape(bb * l, vn.shape[2]).astype(BF16)


def _sample_mix(sinks, q, k, v, cache_k, cache_v, vn, u, wexp, bias, n_p):
    b, w, kvw = cache_k.shape
    n, att = q.shape
    l = (n - n_p) // b
    sgw = vn.shape[1]
    bb = 16 if b % 16 == 0 else b
    assert n_p % (bb * l) == 0
    first = n_p // (bb * l)
    tok = lambda width: pl.BlockSpec((bb * l, width), lambda i: (first + i, 0))
    own = lambda width: pl.BlockSpec((bb * l, width), lambda i: (i, 0))
    past = pl.BlockSpec((bb, w, kvw), lambda i: (i, 0, 0))
    outs = [jax.ShapeDtypeStruct((n - n_p, att), BF16), jax.ShapeDtypeStruct((n - n_p, sgw), BF16)]
    return pl.pallas_call(
        _sample_kernel,
        grid=(b // bb,),
        in_specs=[
            pl.BlockSpec(memory_space=pltpu.SMEM),
            tok(att), tok(kvw), tok(kvw), past, past, tok(sgw), tok(sgw),
            pl.BlockSpec(wexp.shape, lambda i: (0, 0, 0)),
            pl.BlockSpec(bias.shape, lambda i: (0, 0)),
        ],
        out_specs=[own(att), own(sgw)],
        out_shape=outs,
        compiler_params=_params("parallel"),
        name="sample_mix",
    )(sinks, q, k, v, cache_k, cache_v, vn, u, wexp, bias)


def _merge_kernel(x_ref, a_ref, m_ref, ga_ref, gb_ref, wa_ref, wb_ref, wo_ref, fg_ref, wq_ref, keys_ref,
                  x1_ref, xn_ref, s1_ref, s2_ref):
    ha = jnp.dot(a_ref[...], wa_ref[...], preferred_element_type=F32)
    hb = jnp.dot(m_ref[...], wb_ref[...], preferred_element_type=F32)
    h = ga_ref[...].astype(F32) * ha + gb_ref[...].astype(F32) * hb
    x1 = x_ref[...] + jnp.dot(h.astype(BF16), wo_ref[...], preferred_element_type=F32)
    x1_ref[...] = x1
    xn = _rms(x1, fg_ref[...]).astype(BF16)
    xn_ref[...] = xn
    qp = jnp.dot(xn, wq_ref[...], preferred_element_type=F32).astype(BF16)
    half = keys_ref.shape[2]
    nsub = s1_ref.shape[0]
    for hc in range(keys_ref.shape[0]):
        st = lax.dot_general(keys_ref[hc], qp[:, hc * half:(hc + 1) * half],
                             (((1,), (1,)), ((), ())), preferred_element_type=F32)
        dst = s1_ref if hc % 2 == 0 else s2_ref
        for tl in range(nsub):
            dst[tl, hc // 2] = st[:, tl * LANES:(tl + 1) * LANES]


def _merge(xp, xs, ap, a_s, mp, ms, ga, gb, wa, wb, wo, fg, wq, keys):
    (n_p, d), n_s = xp.shape, xs.shape[0]
    n = n_p + n_s
    tb = _token_block(n_p, n_s)
    nbp = n_p // tb
    nsub = tb // LANES
    hc, nk, half = keys.shape
    row = lambda i: (i, 0)
    const2 = lambda i: (0, 0)
    heads = hc // 2
    outs = [
        jax.ShapeDtypeStruct((n, d), F32),
        jax.ShapeDtypeStruct((n, d), BF16),
        jax.ShapeDtypeStruct((n // LANES, heads, nk, LANES), F32),
        jax.ShapeDtypeStruct((n // LANES, heads, nk, LANES), F32),
    ]

    def body(xp_ref, xs_ref, ap_ref, as_ref, mp_ref, ms_ref, *rest):
        side = lambda p_ref, s_ref: _Value(jnp.where(pl.program_id(0) < nbp, p_ref[...], s_ref[...]))
        _merge_kernel(side(xp_ref, xs_ref), side(ap_ref, as_ref), side(mp_ref, ms_ref), *rest)

    prompt = lambda width: pl.BlockSpec((tb, width), lambda i: (jnp.minimum(i, nbp - 1), 0))
    sample = lambda width: pl.BlockSpec((tb, width), lambda i: (jnp.maximum(i - nbp, 0), 0))
    return pl.pallas_call(
        body,
        grid=(n // tb,),
        in_specs=[
            prompt(d), sample(d),
            prompt(ap.shape[1]), sample(ap.shape[1]),
            prompt(mp.shape[1]), sample(mp.shape[1]),
            pl.BlockSpec((tb, d), row),
            pl.BlockSpec((tb, d), row),
            pl.BlockSpec(wa.shape, const2),
            pl.BlockSpec(wb.shape, const2),
            pl.BlockSpec(wo.shape, const2),
            pl.BlockSpec((1, d), const2),
            pl.BlockSpec(wq.shape, const2),
            pl.BlockSpec(keys.shape, lambda i: (0, 0, 0)),
        ],
        out_specs=[
            pl.BlockSpec((tb, d), row),
            pl.BlockSpec((tb, d), row),
            pl.BlockSpec((nsub, heads, nk, LANES), lambda i: (i, 0, 0, 0)),
            pl.BlockSpec((nsub, heads, nk, LANES), lambda i: (i, 0, 0, 0)),
        ],
        out_shape=outs,
        compiler_params=_params("parallel"),
        name="merge",
    )(xp, xs, ap, a_s, mp, ms, ga, gb, wa, wb, wo, fg, wq, keys)


class _Value:
    def __init__(self, value):
        self._value = value

    def __getitem__(self, idx):
        return self._value[idx]


def _oddeven_merge(lo, hi, r):
    step = r * 2
    if step < hi - lo:
        yield from _oddeven_merge(lo, hi, step)
        yield from _oddeven_merge(lo + r, hi, step)
        yield from [(i, i + r) for i in range(lo + r, hi - r, step)]
    else:
        yield (lo, lo + r)


def _oddeven_merge_sort(lo, hi):
    if hi - lo >= 1:
        mid = lo + (hi - lo) // 2
        yield from _oddeven_merge_sort(lo, mid)
        yield from _oddeven_merge_sort(mid + 1, hi)
        yield from _oddeven_merge(lo, hi, 1)


_SORT_TOPK = tuple(_oddeven_merge_sort(0, PEER_TOPK - 1))


def _cmpx(w, i, j):
    a, b = w[i], w[j]
    if b is None:
        return
    if a is None:
        w[i], w[j] = b, None
        return
    w[i], w[j] = jnp.maximum(a, b), jnp.minimum(a, b)


def _top_values(w):
    k = PEER_TOPK
    w = list(w)
    for i, j in _SORT_TOPK:
        _cmpx(w, i, j)
    shift = SUBLANES // 2
    while shift >= 1:
        y = [None if v is None else pltpu.roll(v, shift, 0) for v in w]
        z = []
        for r in range(k):
            a, b = w[r], y[k - 1 - r]
            z.append(b if a is None else a if b is None else jnp.maximum(a, b))
        stride = k // 2
        while stride >= 1:
            for i in range(k):
                if i & stride == 0:
                    _cmpx(z, i, i + stride)
            stride //= 2
        w = z
        shift //= 2
    return w


def _thresh_kernel(s1_ref, s2_ref, g2_ref, gm_ref, g1_ref):
    k = PEER_TOPK
    nk, lanes = s1_ref.shape[2], s1_ref.shape[3]
    nslot = nk // SUBLANES
    assert nslot == k and k == 2 * SUBLANES
    sub = lax.broadcasted_iota(jnp.int32, (SUBLANES, lanes), 0)

    def pack(vals):
        out = vals[0]
        for j in range(1, SUBLANES):
            out = jnp.where(sub == j, vals[j], out)
        return out

    def head(h, carry):
        w1 = [s1_ref[0, h, r * SUBLANES:(r + 1) * SUBLANES, :] for r in range(nslot)]
        w2 = [s2_ref[0, h, r * SUBLANES:(r + 1) * SUBLANES, :] for r in range(nslot)]
        a = _top_values(w1)
        b = _top_values(w2)
        b_lo, b_hi, a_hi = pack(b[:SUBLANES]), pack(b[SUBLANES:]), pack(a[SUBLANES:])
        cands = ([a[0] + b_lo, a[0] + b_hi] + [a[i] + b_lo for i in range(1, SUBLANES)] + [a_hi + b[0]])
        best = _top_values(cands + [None] * (k - len(cands)))
        tau = best[k - 1]
        z = jnp.ones_like(tau)
        for r in range(1, k):
            z = z + jnp.exp(best[r] - best[0])
        inv_z = 1.0 / z
        eb = [jnp.exp(b[j] - b[0]) for j in range(k)]
        gamma = []
        for i in range(k):
            t = jnp.full_like(tau, jnp.inf)
            for j in range(k // (i + 1)):
                t = jnp.where(a[i] + b[j] >= tau, eb[j], t)
            gamma.append(t)
        for r in range(nslot):
            gm = jnp.full_like(tau, jnp.inf)
            for i in range(k - 1, -1, -1):
                gm = jnp.where(w1[r] >= a[i], gamma[i], gm)
            rows = pl.ds(r * SUBLANES, SUBLANES)
            gm_ref[0, h, rows, :] = gm
            g1_ref[0, h, rows, :] = jnp.exp(w1[r] - a[0]) * (0.5 * inv_z)
            g2_ref[0, h, rows, :] = jnp.exp(w2[r] - b[0])
        return carry

    lax.fori_loop(0, s1_ref.shape[1], head, 0)


def _thresholds(s1, s2):
    nsub, heads, nk, lanes = s1.shape
    spec = pl.BlockSpec((1, heads, nk, lanes), lambda i: (i, 0, 0, 0))
    out = jax.ShapeDtypeStruct(s1.shape, F32)
    return pl.pallas_call(
        _thresh_kernel,
        grid=(nsub,),
        in_specs=[spec, spec],
        out_specs=[spec, spec, spec],
        out_shape=[out, out, out],
        compiler_params=_params("parallel"),
        name="peer_thresholds",
    )(s1, s2)


GATE_ROWS = 64
GELU_C0 = math.sqrt(2.0 / math.pi)
GELU_C1 = 0.044715 * GELU_C0
PEER_CHUNK = 512
PEER_TOKENS = 1024


def _gate_columns(at_ref, wt_ref, j0, tls, g2_ref, gm_ref, g1_ref):
    _, heads, nk, lanes = g2_ref.shape

    def tile(j, tl, r0):
        gate = jnp.zeros((GATE_ROWS, lanes), F32)
        for h in range(heads):
            gm = gm_ref[tl, h, j0 + j:j0 + j + 1, :]
            g1 = g1_ref[tl, h, j0 + j:j0 + j + 1, :]
            g2 = g2_ref[tl, h, r0:r0 + GATE_ROWS, :]
            gate = gate + jnp.where(g2 >= gm, g1 * g2, 0.0)
        rows = slice(j * nk + r0, j * nk + r0 + GATE_ROWS)
        cols = slice(tl * lanes, (tl + 1) * lanes)
        x = at_ref[rows, cols]
        t = jnp.tanh(x * (GELU_C0 + GELU_C1 * (x * x)))
        wt_ref[rows, cols] = ((x + x * t) * gate).astype(BF16)

    for tl in tls:
        for j in range(at_ref.shape[0] // nk):
            for r0 in range(0, nk, GATE_ROWS):
                tile(j, tl, r0)


def _peer_kernel(xn_ref, u0a_ref, u0b_ref, una_ref, unb_ref, va_ref, vb_ref, g2_ref, gm_ref, g1_ref, out_ref,
                 at_a, at_b, wt_a, wt_b, acc):
    s = pl.program_id(1)
    per = at_a.shape[0] // g2_ref.shape[2]
    tb = xn_ref.shape[0]
    lanes = g2_ref.shape[3]
    tables = (g2_ref, gm_ref, g1_ref)
    nt = (((1,), (1,)), ((), ()))
    cw = min(MXU_COLS, tb)
    columns = [(slice(c, c + cw), range(c // lanes, (c + cw) // lanes)) for c in range(0, tb, cw)]

    def act(u_ref, at, cs):
        at[:, cs] = lax.dot_general(u_ref[...], xn_ref[cs, :], nt, preferred_element_type=F32)

    def mix(v_ref, wt, cs):
        acc[:, cs] += jnp.dot(v_ref[...], wt[:, cs], preferred_element_type=F32)

    @pl.when(s == 0)
    def _():
        for cs, _ in columns:
            act(u0a_ref, at_a, cs)
            act(u0b_ref, at_b, cs)
        acc[...] = jnp.zeros_like(acc)

    for at, wt, j0, v_ref, un_ref in ((at_a, wt_a, 0, va_ref, una_ref), (at_b, wt_b, per, vb_ref, unb_ref)):
        for cs, tls in columns:
            _gate_columns(at, wt, j0, tls, *tables)
            mix(v_ref, wt, cs)
            act(un_ref, at, cs)

    @pl.when(s == pl.num_programs(1) - 1)
    def _():
        out_ref[...] = acc[...].T


def _peer(xn, u, vt, g2, gm, g1):
    n, d = xn.shape
    ne = u.shape[0]
    tb = PEER_TOKENS if n % PEER_TOKENS == 0 else _token_block(n)
    nsub = tb // LANES
    _, heads, nk, _ = g2.shape
    ec = PEER_CHUNK
    assert ne % (2 * ec) == 0 and ec % nk == 0 and ne == nk * nk and nk % GATE_ROWS == 0
    nc = ne // ec
    per = ec // nk
    keyed2 = pl.BlockSpec((nsub, heads, nk, LANES), lambda i, s: (i, 0, 0, 0))
    keyed1 = pl.BlockSpec((nsub, heads, 2 * per, LANES), lambda i, s: (i, 0, s, 0))
    return pl.pallas_call(
        _peer_kernel,
        grid=(n // tb, nc // 2),
        in_specs=[
            pl.BlockSpec((tb, d), lambda i, s: (i, 0)),
            pl.BlockSpec((ec, d), lambda i, s: (0, 0)),
            pl.BlockSpec((ec, d), lambda i, s: (1, 0)),
            pl.BlockSpec((ec, d), lambda i, s: (jnp.minimum(2 * s + 2, nc - 2), 0)),
            pl.BlockSpec((ec, d), lambda i, s: (jnp.minimum(2 * s + 3, nc - 1), 0)),
            pl.BlockSpec((d, ec), lambda i, s: (0, 2 * s)),
            pl.BlockSpec((d, ec), lambda i, s: (0, 2 * s + 1)),
            keyed2, keyed1, keyed1,
        ],
        out_specs=pl.BlockSpec((tb, d), lambda i, s: (i, 0)),
        out_shape=jax.ShapeDtypeStruct((n, d), F32),
        scratch_shapes=[
            pltpu.VMEM((ec, tb), F32),
            pltpu.VMEM((ec, tb), F32),
            pltpu.VMEM((ec, tb), BF16),
            pltpu.VMEM((ec, tb), BF16),
            pltpu.VMEM((d, tb), F32),
        ],
        compiler_params=_params("parallel", "arbitrary"),
        name="peer_mix",
    )(xn, u, u, u, u, vt, vt, g2, gm, g1)


def _final_kernel(x1_ref, pe_ref, p_ref, g_ref, wg_ref, wp_ref, y_ref):
    x2 = x1_ref[...] + pe_ref[...]
    xn = _rms(x2, g_ref[...]).astype(BF16)
    gate = jax.nn.sigmoid(jnp.dot(xn, wg_ref[...], preferred_element_type=F32))
    y_ref[...] = x2 + gate * jnp.dot(p_ref[...].astype(BF16), wp_ref[...], preferred_element_type=F32)


def _final(x1, pe, p, g, wg, wp, row0):
    n, pd = p.shape
    d = x1.shape[1]
    tb = _token_block(n, row0)
    first = row0 // tb
    src = lambda i: (first + i, 0)
    row = lambda i: (i, 0)
    const = lambda i: (0, 0)
    return pl.pallas_call(
        _final_kernel,
        grid=(n // tb,),
        in_specs=[
            pl.BlockSpec((tb, d), src),
            pl.BlockSpec((tb, d), src),
            pl.BlockSpec((tb, pd), row),
            pl.BlockSpec((1, d), const),
            pl.BlockSpec(wg.shape, const),
            pl.BlockSpec(wp.shape, const),
        ],
        out_specs=pl.BlockSpec((tb, d), row),
        out_shape=jax.ShapeDtypeStruct((n, d), F32),
        compiler_params=_params("parallel"),
        name="ple_epilogue",
    )(x1, pe, p, g, wg, wp)


def _layer(xp, xs, pp, ps, past_k, past_v, lp):
    b, s, d = xp.shape
    bd, l, _ = xs.shape
    n_p, n_s = b * s, bd * l
    kvw = N_KV_HEADS * HEAD_DIM
    sgw = d // 2
    gd = sgw // SGU_GROUPS
    xp2, xs2 = xp.reshape(n_p, d), xs.reshape(n_s, d)

    q, k, v, u, vn, ga, gb = _inproj(
        xp2, xs2, lp['attn_norm_g'][None], lp['w_in'].astype(BF16),
        jnp.tile(lp['q_norm_g'], N_HEADS)[None], jnp.tile(lp['k_norm_g'], N_KV_HEADS)[None],
        lp['sgu_norm_g'][None], lp['sgu_norm_b'][None])

    sgu_w, sgu_b = lp['sgu_w'], lp['sgu_b']
    bias_p = jnp.repeat(sgu_b.T, gd, axis=1)
    a_p, m_p = _prompt_mix(lp['attn_sinks'], q, k, v, vn, u, sgu_w, bias_p, b, s)
    wexp = jnp.repeat(jnp.transpose(sgu_w[:, :l, :l], (2, 1, 0)), gd, axis=2)
    a_s, m_s = _sample_mix(lp['attn_sinks'], q, k, v, past_k.reshape(bd, -1, kvw),
                           past_v.reshape(bd, -1, kvw), vn, u, wexp, bias_p[:l], n_p)

    keys = lp['peer_sub_keys'].reshape(2 * PEER_HEADS, PEER_N_KEYS, -1).astype(BF16)
    x1, xn1, s1, s2 = _merge(xp2, xs2, a_p, a_s, m_p, m_s, ga, gb, lp['w_branch_a'].astype(BF16),
                             lp['w_branch_b'].astype(BF16), lp['w_out'].astype(BF16), lp['ffn_norm_g'][None],
                             lp['peer_w_q'].astype(BF16), keys)
    g2, gm, g1 = _thresholds(s1, s2)
    pe = _peer(xn1, lp['peer_u'].astype(BF16), lp['peer_v'].astype(BF16).T, g2, gm, g1)
    ple = (lp['ple_norm_g'][None], lp['w_ple_gate'].astype(BF16), lp['w_ple'].astype(BF16))
    y_p = _final(x1, pe, pp.reshape(n_p, -1), *ple, 0)
    y_s = _final(x1, pe, ps.reshape(n_s, -1), *ple, n_p)

    wp = min(WINDOW, s)
    tail = lambda t, rows: t[:n_p].reshape(b, s, -1)[:, s - rows:]
    heads = lambda t: t.reshape(t.shape[0], t.shape[1], N_KV_HEADS, HEAD_DIM)
    return (y_p.reshape(b, s, d), y_s.reshape(bd, l, d),
            heads(tail(k, wp)), heads(tail(v, wp)),
            heads(k[n_p:].reshape(bd, l, kvw)), heads(v[n_p:].reshape(bd, l, kvw)),
            tail(vn, CHUNK), vn[n_p:].reshape(bd, l, sgw))


def kernel(x_prompt, x_sample, cache_k, cache_v, p_prompt, p_sample, attn_norm_g, w_in, q_norm_g, k_norm_g, attn_sinks, sgu_norm_g, sgu_norm_b, sgu_w, sgu_b, w_branch_a, w_branch_b, w_out, ffn_norm_g, peer_w_q, peer_sub_keys, peer_u, peer_v, ple_norm_g, w_ple, w_ple_gate):
    depth = w_in.shape[0]
    hp, hs = x_prompt, x_sample
    outs = [[] for _ in range(6)]
    for i in range(depth):
        lp = dict(attn_norm_g=attn_norm_g[i], w_in=w_in[i], q_norm_g=q_norm_g[i], k_norm_g=k_norm_g[i],
                  attn_sinks=attn_sinks[i], sgu_norm_g=sgu_norm_g[i], sgu_norm_b=sgu_norm_b[i],
                  sgu_w=sgu_w[i], sgu_b=sgu_b[i], w_branch_a=w_branch_a[i], w_branch_b=w_branch_b[i],
                  w_out=w_out[i], ffn_norm_g=ffn_norm_g[i], peer_w_q=peer_w_q[i],
                  peer_sub_keys=peer_sub_keys[i], peer_u=peer_u[i], peer_v=peer_v[i],
                  ple_norm_g=ple_norm_g[i], w_ple=w_ple[i], w_ple_gate=w_ple_gate[i])
        res = _layer(hp, hs, p_prompt[i], p_sample[i], cache_k[i], cache_v[i], lp)
        hp, hs = res[0], res[1]
        for lst, t in zip(outs, res[2:]):
            lst.append(t)
    return (hp, hs) + tuple(jnp.stack(o) for o in outs)
```

```python
import functools
import math

import jax
import jax.numpy as jnp
from jax import lax
from jax.experimental import pallas as pl
from jax.experimental.pallas import tpu as pltpu

F32 = jnp.float32
BF16 = jnp.bfloat16

N_HEADS = 8
N_KV_HEADS = 2
HEAD_DIM = 64
Q_GROUP = N_HEADS // N_KV_HEADS
WINDOW = 128
CHUNK = 128
SGU_GROUPS = 4
PEER_HEADS = 8
PEER_N_KEYS = 128
PEER_TOPK = 16
EPS = 1e-6
NEG_INF = -1e30
ALIBI_SLOPES = tuple(2.0 ** (-8.0 * h / N_HEADS) for h in range(1, N_HEADS + 1))

LANES = 128
SUBLANES = 8
MXU_COLS = 256
VMEM_LIMIT = 56 * 1024 * 1024


def _params(*semantics):
    return pltpu.CompilerParams(dimension_semantics=semantics, vmem_limit_bytes=VMEM_LIMIT)


def _token_block(*counts):
    for tb in (512, 256, 128):
        if all(n % tb == 0 for n in counts):
            return tb
    raise ValueError(f"token counts {counts} must be multiples of 128")


def _rms(x, g):
    return x * lax.rsqrt(jnp.mean(x * x, axis=-1, keepdims=True) + EPS) * g


def _group_rms(t, ones_blk, g):
    t2 = t * t
    hi = t2.astype(BF16)
    lo = (t2 - hi.astype(F32)).astype(BF16)
    ss = (jnp.dot(hi, ones_blk, preferred_element_type=F32)
          + jnp.dot(lo, ones_blk, preferred_element_type=F32))
    return t * lax.rsqrt(ss * (1.0 / HEAD_DIM) + EPS) * g


def _inproj_kernel(xp_ref, xs_ref, g_ref, w_ref, qg_ref, kg_ref, lg_ref, lb_ref, bq_ref, bk_ref,
                   q_ref, k_ref, v_ref, u_ref, vn_ref, ga_ref, gb_ref, *, prompt_blocks):
    x = jnp.where(pl.program_id(0) < prompt_blocks, xp_ref[...], xs_ref[...])
    xn = _rms(x, g_ref[...])
    z = jnp.dot(xn.astype(BF16), w_ref[...], preferred_element_type=F32)
    att = N_HEADS * HEAD_DIM
    kvw = N_KV_HEADS * HEAD_DIM
    sgw = (z.shape[1] - att - 2 * kvw) // 6
    o = 0
    q = z[:, o:o + att]; o += att
    k = z[:, o:o + kvw]; o += kvw
    v = z[:, o:o + kvw]; o += kvw
    su = z[:, o:o + sgw]; o += sgw
    sv = z[:, o:o + sgw]; o += sgw
    g_a = z[:, o:o + 2 * sgw]; o += 2 * sgw
    g_b = z[:, o:o + 2 * sgw]
    qn = _group_rms(q, bq_ref[...], qg_ref[...])
    q_ref[...] = (qn * (HEAD_DIM ** -0.5)).astype(BF16)
    k_ref[...] = _group_rms(k, bk_ref[...], kg_ref[...])
    v_ref[...] = v
    u_ref[...] = jax.nn.gelu(su).astype(BF16)
    gv = jax.nn.gelu(sv)
    mu = jnp.mean(gv, axis=-1, keepdims=True)
    gc = gv - mu
    vn_ref[...] = gc * lax.rsqrt(jnp.mean(gc * gc, axis=-1, keepdims=True) + EPS) * lg_ref[...] + lb_ref[...]
    ga_ref[...] = jax.nn.sigmoid(g_a).astype(BF16)
    gb_ref[...] = jax.nn.sigmoid(g_b).astype(BF16)


def _inproj(xp, xs, g, w_in, qg, kg, lg, lb):
    (n_p, d), n_s = xp.shape, xs.shape[0]
    n = n_p + n_s
    tb = _token_block(n_p, n_s)
    nbp = n_p // tb
    att = N_HEADS * HEAD_DIM
    kvw = N_KV_HEADS * HEAD_DIM
    sgw = d // 2
    hid = jnp.arange(att) // HEAD_DIM
    bq = (hid[:, None] == hid[None, :]).astype(BF16)
    bk = bq[:kvw, :kvw]
    const = lambda i: (0, 0)
    row = lambda i: (i, 0)
    outs = [
        jax.ShapeDtypeStruct((n, att), BF16),
        jax.ShapeDtypeStruct((n, kvw), F32),
        jax.ShapeDtypeStruct((n, kvw), F32),
        jax.ShapeDtypeStruct((n, sgw), BF16),
        jax.ShapeDtypeStruct((n, sgw), F32),
        jax.ShapeDtypeStruct((n, d), BF16),
        jax.ShapeDtypeStruct((n, d), BF16),
    ]
    return pl.pallas_call(
        functools.partial(_inproj_kernel, prompt_blocks=nbp),
        grid=(n // tb,),
        in_specs=[
            pl.BlockSpec((tb, d), lambda i: (jnp.minimum(i, nbp - 1), 0)),
            pl.BlockSpec((tb, d), lambda i: (jnp.maximum(i - nbp, 0), 0)),
            pl.BlockSpec((1, d), const),
            pl.BlockSpec(w_in.shape, const),
            pl.BlockSpec((1, att), const),
            pl.BlockSpec((1, kvw), const),
            pl.BlockSpec((1, sgw), const),
            pl.BlockSpec((1, sgw), const),
            pl.BlockSpec((att, att), const),
            pl.BlockSpec((kvw, kvw), const),
        ],
        out_specs=[pl.BlockSpec((tb, s.shape[1]), row) for s in outs],
        out_shape=outs,
        compiler_params=_params("parallel"),
        name="inproj",
    )(xp, xs, g, w_in, qg, kg, lg, lb, bq, bk)


def _sink_softmax(s, sink):
    mx = jnp.maximum(jnp.max(s, axis=-1, keepdims=True), sink)
    p = jnp.exp(s - mx)
    den = jnp.sum(p, axis=-1, keepdims=True) + jnp.exp(sink - mx)
    return p / den


def _prompt_kernel(sinks_ref, q_ref, kc_ref, kp_ref, vc_ref, vp_ref, vn_ref, u_ref, w_ref, bias_ref,
                   a_ref, m_ref):
    i = pl.program_id(1)
    tq = q_ref.shape[0]
    nblk = tq // WINDOW
    q = q_ref[...]
    kc = kc_ref[...].astype(BF16)
    vc = vc_ref[...].astype(BF16)
    kp = kp_ref[...].astype(BF16)
    vp = vp_ref[...].astype(BF16)
    row = lax.broadcasted_iota(jnp.int32, (WINDOW, 2 * WINDOW), 0)
    col = lax.broadcasted_iota(jnp.int32, (WINDOW, 2 * WINDOW), 1)
    dist = row - col + WINDOW
    in_window = (dist >= 0) & (dist < WINDOW)
    distf = dist.astype(F32)
    for jq in range(nblk):
        r0 = jq * WINDOW
        if jq == 0:
            kprev, vprev = kp, vp
            valid = in_window & (col >= jnp.where(i > 0, 0, WINDOW))
        else:
            kprev, vprev = kc[r0 - WINDOW:r0], vc[r0 - WINDOW:r0]
            valid = in_window
        kcat = jnp.concatenate([kprev, kc[r0:r0 + WINDOW]], axis=0)
        vcat = jnp.concatenate([vprev, vc[r0:r0 + WINDOW]], axis=0)
        for g in range(N_KV_HEADS):
            heads = range(g * Q_GROUP, (g + 1) * Q_GROUP)
            qg = jnp.concatenate([q[r0:r0 + WINDOW, h * HEAD_DIM:(h + 1) * HEAD_DIM] for h in heads], axis=0)
            s_all = lax.dot_general(qg, kcat[:, g * HEAD_DIM:(g + 1) * HEAD_DIM],
                                    (((1,), (1,)), ((), ())), preferred_element_type=F32)
            probs = []
            for hl, h in enumerate(heads):
                s = s_all[hl * WINDOW:(hl + 1) * WINDOW] - ALIBI_SLOPES[h] * distf
                s = jnp.where(valid, s, NEG_INF)
                probs.append(_sink_softmax(s, sinks_ref[h]).astype(BF16))
            o_all = jnp.dot(jnp.concatenate(probs, axis=0), vcat[:, g * HEAD_DIM:(g + 1) * HEAD_DIM],
                            preferred_element_type=F32)
            for hl, h in enumerate(heads):
                a_ref[r0:r0 + WINDOW, h * HEAD_DIM:(h + 1) * HEAD_DIM] = (
                    o_all[hl * WINDOW:(hl + 1) * WINDOW].astype(BF16))
    tr = lax.broadcasted_iota(jnp.int32, (CHUNK, CHUNK), 0)
    tc = lax.broadcasted_iota(jnp.int32, (CHUNK, CHUNK), 1)
    gd = vn_ref.shape[1] // SGU_GROUPS
    wm = [jnp.where(tr >= tc, w_ref[g], 0.0).astype(BF16) for g in range(SGU_GROUPS)]
    for c in range(tq // CHUNK):
        r0 = c * CHUNK
        vnc = vn_ref[r0:r0 + CHUNK, :].astype(BF16)
        for g in range(SGU_GROUPS):
            s = jnp.dot(wm[g], vnc[:, g * gd:(g + 1) * gd], preferred_element_type=F32)
            s = s + bias_ref[:, g * gd:(g + 1) * gd]
            m_ref[r0:r0 + CHUNK, g * gd:(g + 1) * gd] = (
                u_ref[r0:r0 + CHUNK, g * gd:(g + 1) * gd].astype(F32) * s).astype(BF16)


def _prompt_mix(sinks, q, k, v, vn, u, sgu_w, sgu_bias, b, s):
    n, att = b * s, q.shape[1]
    kvw = k.shape[1]
    sgw = vn.shape[1]
    tq = 512 if s % 512 == 0 else WINDOW
    assert s % tq == 0 and tq % WINDOW == 0 and WINDOW == CHUNK
    r = tq // WINDOW
    nq = s // tq
    cur = lambda bi, i: (bi * nq + i, 0)
    prev = lambda bi, i: (jnp.maximum((bi * nq + i) * r - 1, 0), 0)
    outs = [jax.ShapeDtypeStruct((n, att), BF16), jax.ShapeDtypeStruct((n, sgw), BF16)]
    return pl.pallas_call(
        _prompt_kernel,
        grid=(b, nq),
        in_specs=[
            pl.BlockSpec(memory_space=pltpu.SMEM),
            pl.BlockSpec((tq, att), cur),
            pl.BlockSpec((tq, kvw), cur),
            pl.BlockSpec((WINDOW, kvw), prev),
            pl.BlockSpec((tq, kvw), cur),
            pl.BlockSpec((WINDOW, kvw), prev),
            pl.BlockSpec((tq, sgw), cur),
            pl.BlockSpec((tq, sgw), cur),
            pl.BlockSpec(sgu_w.shape, lambda bi, i: (0, 0, 0)),
            pl.BlockSpec(sgu_bias.shape, lambda bi, i: (0, 0)),
        ],
        out_specs=[pl.BlockSpec((tq, att), cur), pl.BlockSpec((tq, sgw), cur)],
        out_shape=outs,
        compiler_params=_params("parallel", "parallel"),
        name="prompt_mix",
    )(sinks, q, k, k, v, v, vn, u, sgu_w, sgu_bias)


def _sample_kernel(sinks_ref, q_ref, kn_ref, vn_new_ref, ck_ref, cv_ref, vn_ref, u_ref, wexp_ref, bias_ref,
                   a_ref, m_ref):
    bb, w, _ = ck_ref.shape
    l = q_ref.shape[0] // bb
    per_seq = lambda ref: ref[...].astype(F32).reshape(bb, l, ref.shape[1])
    q = per_seq(q_ref)
    kcat = jnp.concatenate([ck_ref[...], per_seq(kn_ref)], axis=1).astype(BF16)
    vcat = jnp.concatenate([cv_ref[...], per_seq(vn_new_ref)], axis=1).astype(BF16)
    rows = Q_GROUP * l
    t = lax.broadcasted_iota(jnp.int32, (rows, w + l), 0) % l
    key = lax.broadcasted_iota(jnp.int32, (rows, w + l), 1)
    dist = t - (key - w)
    valid = (dist >= 0) & (dist < WINDOW)
    distf = dist.astype(F32)
    hl_of_row = lax.broadcasted_iota(jnp.int32, (rows, 1), 0) // l
    for g in range(N_KV_HEADS):
        heads = range(g * Q_GROUP, (g + 1) * Q_GROUP)
        qg = jnp.concatenate([q[:, :, h * HEAD_DIM:(h + 1) * HEAD_DIM] for h in heads], axis=1)
        s = jnp.einsum('bqd,bkd->bqk', qg.astype(BF16), kcat[:, :, g * HEAD_DIM:(g + 1) * HEAD_DIM],
                       preferred_element_type=F32)
        slope = jnp.zeros((rows, 1), F32)
        sink = jnp.zeros((rows, 1), F32)
        for hl, h in enumerate(heads):
            slope = jnp.where(hl_of_row == hl, ALIBI_SLOPES[h], slope)
            sink = jnp.where(hl_of_row == hl, sinks_ref[h], sink)
        s = jnp.where(valid[None], s - (slope * distf)[None], NEG_INF)
        p = _sink_softmax(s, sink[None]).astype(BF16)
        o = jnp.einsum('bqk,bkd->bqd', p, vcat[:, :, g * HEAD_DIM:(g + 1) * HEAD_DIM],
                       preferred_element_type=F32)
        for hl, h in enumerate(heads):
            a_ref[:, h * HEAD_DIM:(h + 1) * HEAD_DIM] = (
                o[:, hl * l:(hl + 1) * l, :].reshape(bb * l, HEAD_DIM).astype(BF16))
    vn = per_seq(vn_ref)
    tt = lax.broadcasted_iota(jnp.int32, (l, vn.shape[2]), 0)
    s = jnp.broadcast_to(bias_ref[...][None], vn.shape)
    for sp in range(l):
        wm = jnp.where(tt >= sp, wexp_ref[sp], 0.0)
        s = s + wm[None] * vn[:, sp:sp + 1, :]
    m_ref[...] = (per_seq(u_ref) * s).reshape(bb * l, vn.shape[2]).astype(BF16)


def _sample_mix(sinks, q, k, v, cache_k, cache_v, vn, u, wexp, bias, n_p):
    b, w, kvw = cache_k.shape
    n, att = q.shape
    l = (n - n_p) // b
    sgw = vn.shape[1]
    bb = 16 if b % 16 == 0 else b
    assert n_p % (bb * l) == 0
    first = n_p // (bb * l)
    tok = lambda width: pl.BlockSpec((bb * l, width), lambda i: (first + i, 0))
    own = lambda width: pl.BlockSpec((bb * l, width), lambda i: (i, 0))
    past = pl.BlockSpec((bb, w, kvw), lambda i: (i, 0, 0))
    outs = [jax.ShapeDtypeStruct((n - n_p, att), BF16), jax.ShapeDtypeStruct((n - n_p, sgw), BF16)]
    return pl.pallas_call(
        _sample_kernel,
        grid=(b // bb,),
        in_specs=[
            pl.BlockSpec(memory_space=pltpu.SMEM),
            tok(att), tok(kvw), tok(kvw), past, past, tok(sgw), tok(sgw),
            pl.BlockSpec(wexp.shape, lambda i: (0, 0, 0)),
            pl.BlockSpec(bias.shape, lambda i: (0, 0)),
        ],
        out_specs=[own(att), own(sgw)],
        out_shape=outs,
        compiler_params=_params("parallel"),
        name="sample_mix",
    )(sinks, q, k, v, cache_k, cache_v, vn, u, wexp, bias)


def _merge_kernel(x_ref, a_ref, m_ref, ga_ref, gb_ref, wa_ref, wb_ref, wo_ref, fg_ref, wq_ref, keys_ref,
                  x1_ref, xnt_ref, s1_ref, s2_ref):
    ha = jnp.dot(a_ref[...], wa_ref[...], preferred_element_type=F32)
    hb = jnp.dot(m_ref[...], wb_ref[...], preferred_element_type=F32)
    h = ga_ref[...].astype(F32) * ha + gb_ref[...].astype(F32) * hb
    x1 = x_ref[...] + jnp.dot(h.astype(BF16), wo_ref[...], preferred_element_type=F32)
    x1_ref[...] = x1
    xn32 = _rms(x1, fg_ref[...])
    xnt_ref[...] = xn32.T.astype(BF16)
    xn = xn32.astype(BF16)
    qp = jnp.dot(xn, wq_ref[...], preferred_element_type=F32).astype(BF16)
    half = keys_ref.shape[2]
    nsub = s1_ref.shape[0]
    for hc in range(keys_ref.shape[0]):
        st = lax.dot_general(keys_ref[hc], qp[:, hc * half:(hc + 1) * half],
                             (((1,), (1,)), ((), ())), preferred_element_type=F32)
        dst = s1_ref if hc % 2 == 0 else s2_ref
        for tl in range(nsub):
            dst[tl, hc // 2] = st[:, tl * LANES:(tl + 1) * LANES]


def _merge(xp, xs, ap, a_s, mp, ms, ga, gb, wa, wb, wo, fg, wq, keys):
    (n_p, d), n_s = xp.shape, xs.shape[0]
    n = n_p + n_s
    tb = _token_block(n_p, n_s)
    nbp = n_p // tb
    nsub = tb // LANES
    hc, nk, half = keys.shape
    row = lambda i: (i, 0)
    const2 = lambda i: (0, 0)
    heads = hc // 2
    outs = [
        jax.ShapeDtypeStruct((n, d), F32),
        jax.ShapeDtypeStruct((d, n), BF16),
        jax.ShapeDtypeStruct((n // LANES, heads, nk, LANES), F32),
        jax.ShapeDtypeStruct((n // LANES, heads, nk, LANES), F32),
    ]

    def body(xp_ref, xs_ref, ap_ref, as_ref, mp_ref, ms_ref, *rest):
        side = lambda p_ref, s_ref: _Value(jnp.where(pl.program_id(0) < nbp, p_ref[...], s_ref[...]))
        _merge_kernel(side(xp_ref, xs_ref), side(ap_ref, as_ref), side(mp_ref, ms_ref), *rest)

    prompt = lambda width: pl.BlockSpec((tb, width), lambda i: (jnp.minimum(i, nbp - 1), 0))
    sample = lambda width: pl.BlockSpec((tb, width), lambda i: (jnp.maximum(i - nbp, 0), 0))
    return pl.pallas_call(
        body,
        grid=(n // tb,),
        in_specs=[
            prompt(d), sample(d),
            prompt(ap.shape[1]), sample(ap.shape[1]),
            prompt(mp.shape[1]), sample(mp.shape[1]),
            pl.BlockSpec((tb, d), row),
            pl.BlockSpec((tb, d), row),
            pl.BlockSpec(wa.shape, const2),
            pl.BlockSpec(wb.shape, const2),
            pl.BlockSpec(wo.shape, const2),
            pl.BlockSpec((1, d), const2),
            pl.BlockSpec(wq.shape, const2),
            pl.BlockSpec(keys.shape, lambda i: (0, 0, 0)),
        ],
        out_specs=[
            pl.BlockSpec((tb, d), row),
            pl.BlockSpec((d, tb), lambda i: (0, i)),
            pl.BlockSpec((nsub, heads, nk, LANES), lambda i: (i, 0, 0, 0)),
            pl.BlockSpec((nsub, heads, nk, LANES), lambda i: (i, 0, 0, 0)),
        ],
        out_shape=outs,
        compiler_params=_params("parallel"),
        name="merge",
    )(xp, xs, ap, a_s, mp, ms, ga, gb, wa, wb, wo, fg, wq, keys)


class _Value:
    def __init__(self, value):
        self._value = value

    def __getitem__(self, idx):
        return self._value[idx]


def _oddeven_merge(lo, hi, r):
    step = r * 2
    if step < hi - lo:
        yield from _oddeven_merge(lo, hi, step)
        yield from _oddeven_merge(lo + r, hi, step)
        yield from [(i, i + r) for i in range(lo + r, hi - r, step)]
    else:
        yield (lo, lo + r)


def _oddeven_merge_sort(lo, hi):
    if hi - lo >= 1:
        mid = lo + (hi - lo) // 2
        yield from _oddeven_merge_sort(lo, mid)
        yield from _oddeven_merge_sort(mid + 1, hi)
        yield from _oddeven_merge(lo, hi, 1)


_SORT_TOPK = tuple(_oddeven_merge_sort(0, PEER_TOPK - 1))


def _cmpx(w, i, j):
    a, b = w[i], w[j]
    if b is None:
        return
    if a is None:
        w[i], w[j] = b, None
        return
    w[i], w[j] = jnp.maximum(a, b), jnp.minimum(a, b)


def _top_values(w):
    k = PEER_TOPK
    w = list(w)
    for i, j in _SORT_TOPK:
        _cmpx(w, i, j)
    shift = SUBLANES // 2
    while shift >= 1:
        y = [None if v is None else pltpu.roll(v, shift, 0) for v in w]
        z = []
        for r in range(k):
            a, b = w[r], y[k - 1 - r]
            z.append(b if a is None else a if b is None else jnp.maximum(a, b))
        stride = k // 2
        while stride >= 1:
            for i in range(k):
                if i & stride == 0:
                    _cmpx(z, i, i + stride)
            stride //= 2
        w = z
        shift //= 2
    return w


def _thresh_kernel(s1_ref, s2_ref, g2_ref, gm_ref, g1_ref):
    k = PEER_TOPK
    nk, lanes = s1_ref.shape[2], s1_ref.shape[3]
    nslot = nk // SUBLANES
    assert nslot == k and k == 2 * SUBLANES
    sub = lax.broadcasted_iota(jnp.int32, (SUBLANES, lanes), 0)

    def pack(vals):
        out = vals[0]
        for j in range(1, SUBLANES):
            out = jnp.where(sub == j, vals[j], out)
        return out

    def head(h, carry):
        w1 = [s1_ref[0, h, r * SUBLANES:(r + 1) * SUBLANES, :] for r in range(nslot)]
        w2 = [s2_ref[0, h, r * SUBLANES:(r + 1) * SUBLANES, :] for r in range(nslot)]
        a = _top_values(w1)
        b = _top_values(w2)
        b_lo, b_hi, a_hi = pack(b[:SUBLANES]), pack(b[SUBLANES:]), pack(a[SUBLANES:])
        cands = ([a[0] + b_lo, a[0] + b_hi] + [a[i] + b_lo for i in range(1, SUBLANES)] + [a_hi + b[0]])
        best = _top_values(cands + [None] * (k - len(cands)))
        tau = best[k - 1]
        z = jnp.ones_like(tau)
        for r in range(1, k):
            z = z + jnp.exp(best[r] - best[0])
        inv_z = 1.0 / z
        eb = [jnp.exp(b[j] - b[0]) for j in range(k)]
        gamma = []
        for i in range(k):
            t = jnp.full_like(tau, jnp.inf)
            for j in range(k // (i + 1)):
                t = jnp.where(a[i] + b[j] >= tau, eb[j], t)
            gamma.append(t)
        for r in range(nslot):
            gm = jnp.full_like(tau, jnp.inf)
            for i in range(k - 1, -1, -1):
                gm = jnp.where(w1[r] >= a[i], gamma[i], gm)
            rows = pl.ds(r * SUBLANES, SUBLANES)
            gm_ref[0, h, rows, :] = gm
            g1_ref[0, h, rows, :] = jnp.exp(w1[r] - a[0]) * (0.5 * inv_z)
            g2_ref[0, h, rows, :] = jnp.exp(w2[r] - b[0])
        return carry

    lax.fori_loop(0, s1_ref.shape[1], head, 0)


def _thresholds(s1, s2):
    nsub, heads, nk, lanes = s1.shape
    spec = pl.BlockSpec((1, heads, nk, lanes), lambda i: (i, 0, 0, 0))
    out = jax.ShapeDtypeStruct(s1.shape, F32)
    return pl.pallas_call(
        _thresh_kernel,
        grid=(nsub,),
        in_specs=[spec, spec],
        out_specs=[spec, spec, spec],
        out_shape=[out, out, out],
        compiler_params=_params("parallel"),
        name="peer_thresholds",
    )(s1, s2)


GATE_ROWS = 64
GELU_C0 = math.sqrt(2.0 / math.pi)
GELU_C1 = 0.044715 * GELU_C0
PEER_CHUNK = 512
PEER_TOKENS = 1024


def _gate_columns(at_ref, wt_ref, j0, tls, g2_ref, gm_ref, g1_ref):
    _, heads, nk, lanes = g2_ref.shape

    def tile(j, tl, r0):
        gate = jnp.zeros((GATE_ROWS, lanes), F32)
        for h in range(heads):
            gm = gm_ref[tl, h, j0 + j:j0 + j + 1, :]
            g1 = g1_ref[tl, h, j0 + j:j0 + j + 1, :]
            g2 = g2_ref[tl, h, r0:r0 + GATE_ROWS, :]
            gate = gate + jnp.where(g2 >= gm, g1 * g2, 0.0)
        rows = slice(j * nk + r0, j * nk + r0 + GATE_ROWS)
        cols = slice(tl * lanes, (tl + 1) * lanes)
        x = at_ref[rows, cols]
        t = jnp.tanh(x * (GELU_C0 + GELU_C1 * (x * x)))
        wt_ref[rows, cols] = ((x + x * t) * gate).astype(BF16)

    for tl in tls:
        for j in range(at_ref.shape[0] // nk):
            for r0 in range(0, nk, GATE_ROWS):
                tile(j, tl, r0)


def _peer_kernel(xnt_ref, u0a_ref, u0b_ref, una_ref, unb_ref, va_ref, vb_ref, g2_ref, gm_ref, g1_ref, out_ref,
                 at_a, at_b, wt_a, wt_b, acc, xnt):
    s = pl.program_id(1)
    per = at_a.shape[0] // g2_ref.shape[2]
    tb = xnt_ref.shape[1]
    lanes = g2_ref.shape[3]
    tables = (g2_ref, gm_ref, g1_ref)
    cw = min(MXU_COLS, tb)
    columns = [(slice(c, c + cw), range(c // lanes, (c + cw) // lanes)) for c in range(0, tb, cw)]

    def act(u_ref, at, cs):
        at[:, cs] = jnp.dot(u_ref[...], xnt[:, cs], preferred_element_type=F32)

    def mix(v_ref, wt, cs):
        acc[:, cs] += jnp.dot(v_ref[...], wt[:, cs], preferred_element_type=F32)

    @pl.when(s == 0)
    def _():
        xnt[...] = xnt_ref[...]
        for cs, _ in columns:
            act(u0a_ref, at_a, cs)
            act(u0b_ref, at_b, cs)
        acc[...] = jnp.zeros_like(acc)

    for at, wt, j0, v_ref, un_ref in ((at_a, wt_a, 0, va_ref, una_ref), (at_b, wt_b, per, vb_ref, unb_ref)):
        for cs, tls in columns:
            _gate_columns(at, wt, j0, tls, *tables)
            mix(v_ref, wt, cs)
            act(un_ref, at, cs)

    @pl.when(s == pl.num_programs(1) - 1)
    def _():
        out_ref[...] = acc[...].T


def _peer(xnt, u, vt, g2, gm, g1):
    d, n = xnt.shape
    ne = u.shape[0]
    tb = PEER_TOKENS if n % PEER_TOKENS == 0 else _token_block(n)
    nsub = tb // LANES
    _, heads, nk, _ = g2.shape
    ec = PEER_CHUNK
    assert ne % (2 * ec) == 0 and ec % nk == 0 and ne == nk * nk and nk % GATE_ROWS == 0
    nc = ne // ec
    per = ec // nk
    keyed2 = pl.BlockSpec((nsub, heads, nk, LANES), lambda i, s: (i, 0, 0, 0))
    keyed1 = pl.BlockSpec((nsub, heads, 2 * per, LANES), lambda i, s: (i, 0, s, 0))
    return pl.pallas_call(
        _peer_kernel,
        grid=(n // tb, nc // 2),
        in_specs=[
            pl.BlockSpec((d, tb), lambda i, s: (0, i)),
            pl.BlockSpec((ec, d), lambda i, s: (0, 0)),
            pl.BlockSpec((ec, d), lambda i, s: (1, 0)),
            pl.BlockSpec((ec, d), lambda i, s: (jnp.minimum(2 * s + 2, nc - 2), 0)),
            pl.BlockSpec((ec, d), lambda i, s: (jnp.minimum(2 * s + 3, nc - 1), 0)),
            pl.BlockSpec((d, ec), lambda i, s: (0, 2 * s)),
            pl.BlockSpec((d, ec), lambda i, s: (0, 2 * s + 1)),
            keyed2, keyed1, keyed1,
        ],
        out_specs=pl.BlockSpec((tb, d), lambda i, s: (i, 0)),
        out_shape=jax.ShapeDtypeStruct((n, d), F32),
        scratch_shapes=[
            pltpu.VMEM((ec, tb), F32),
            pltpu.VMEM((ec, tb), F32),
            pltpu.VMEM((ec, tb), BF16),
            pltpu.VMEM((ec, tb), BF16),
            pltpu.VMEM((d, tb), F32),
            pltpu.VMEM((d, tb), BF16),
        ],
        compiler_params=_params("parallel", "arbitrary"),
        name="peer_mix",
    )(xnt, u, u, u, u, vt, vt, g2, gm, g1)


def _final_kernel(x1_ref, pe_ref, p_ref, g_ref, wg_ref, wp_ref, y_ref):
    x2 = x1_ref[...] + pe_ref[...]
    xn = _rms(x2, g_ref[...]).astype(BF16)
    gate = jax.nn.sigmoid(jnp.dot(xn, wg_ref[...], preferred_element_type=F32))
    y_ref[...] = x2 + gate * jnp.dot(p_ref[...].astype(BF16), wp_ref[...], preferred_element_type=F32)


def _final(x1, pe, p, g, wg, wp, row0):
    n, pd = p.shape
    d = x1.shape[1]
    tb = _token_block(n, row0)
    first = row0 // tb
    src = lambda i: (first + i, 0)
    row = lambda i: (i, 0)
    const = lambda i: (0, 0)
    return pl.pallas_call(
        _final_kernel,
        grid=(n // tb,),
        in_specs=[
            pl.BlockSpec((tb, d), src),
            pl.BlockSpec((tb, d), src),
            pl.BlockSpec((tb, pd), row),
            pl.BlockSpec((1, d), const),
            pl.BlockSpec(wg.shape, const),
            pl.BlockSpec(wp.shape, const),
        ],
        out_specs=pl.BlockSpec((tb, d), row),
        out_shape=jax.ShapeDtypeStruct((n, d), F32),
        compiler_params=_params("parallel"),
        name="ple_epilogue",
    )(x1, pe, p, g, wg, wp)


def _layer(xp, xs, pp, ps, past_k, past_v, lp):
    b, s, d = xp.shape
    bd, l, _ = xs.shape
    n_p, n_s = b * s, bd * l
    kvw = N_KV_HEADS * HEAD_DIM
    sgw = d // 2
    gd = sgw // SGU_GROUPS
    xp2, xs2 = xp.reshape(n_p, d), xs.reshape(n_s, d)

    q, k, v, u, vn, ga, gb = _inproj(
        xp2, xs2, lp['attn_norm_g'][None], lp['w_in'].astype(BF16),
        jnp.tile(lp['q_norm_g'], N_HEADS)[None], jnp.tile(lp['k_norm_g'], N_KV_HEADS)[None],
        lp['sgu_norm_g'][None], lp['sgu_norm_b'][None])

    sgu_w, sgu_b = lp['sgu_w'], lp['sgu_b']
    bias_p = jnp.repeat(sgu_b.T, gd, axis=1)
    a_p, m_p = _prompt_mix(lp['attn_sinks'], q, k, v, vn, u, sgu_w, bias_p, b, s)
    wexp = jnp.repeat(jnp.transpose(sgu_w[:, :l, :l], (2, 1, 0)), gd, axis=2)
    a_s, m_s = _sample_mix(lp['attn_sinks'], q, k, v, past_k.reshape(bd, -1, kvw),
                           past_v.reshape(bd, -1, kvw), vn, u, wexp, bias_p[:l], n_p)

    keys = lp['peer_sub_keys'].reshape(2 * PEER_HEADS, PEER_N_KEYS, -1).astype(BF16)
    x1, xn1, s1, s2 = _merge(xp2, xs2, a_p, a_s, m_p, m_s, ga, gb, lp['w_branch_a'].astype(BF16),
                             lp['w_branch_b'].astype(BF16), lp['w_out'].astype(BF16), lp['ffn_norm_g'][None],
                             lp['peer_w_q'].astype(BF16), keys)
    g2, gm, g1 = _thresholds(s1, s2)
    pe = _peer(xn1, lp['peer_u'].astype(BF16), lp['peer_v'].astype(BF16).T, g2, gm, g1)
    ple = (lp['ple_norm_g'][None], lp['w_ple_gate'].astype(BF16), lp['w_ple'].astype(BF16))
    y_p = _final(x1, pe, pp.reshape(n_p, -1), *ple, 0)
    y_s = _final(x1, pe, ps.reshape(n_s, -1), *ple, n_p)

    wp = min(WINDOW, s)
    tail = lambda t, rows: t[:n_p].reshape(b, s, -1)[:, s - rows:]
    heads = lambda t: t.reshape(t.shape[0], t.shape[1], N_KV_HEADS, HEAD_DIM)
    return (y_p.reshape(b, s, d), y_s.reshape(bd, l, d),
            heads(tail(k, wp)), heads(tail(v, wp)),
            heads(k[n_p:].reshape(bd, l, kvw)), heads(v[n_p:].reshape(bd, l, kvw)),
            tail(vn, CHUNK), vn[n_p:].reshape(bd, l, sgw))


def kernel(x_prompt, x_sample, cache_k, cache_v, p_prompt, p_sample, attn_norm_g, w_in, q_norm_g, k_norm_g, attn_sinks, sgu_norm_g, sgu_norm_b, sgu_w, sgu_b, w_branch_a, w_branch_b, w_out, ffn_norm_g, peer_w_q, peer_sub_keys, peer_u, peer_v, ple_norm_g, w_ple, w_ple_gate):
    depth = w_in.shape[0]
    hp, hs = x_prompt, x_sample
    outs = [[] for _ in range(6)]
    for i in range(depth):
        lp = dict(attn_norm_g=attn_norm_g[i], w_in=w_in[i], q_norm_g=q_norm_g[i], k_norm_g=k_norm_g[i],
                  attn_sinks=attn_sinks[i], sgu_norm_g=sgu_norm_g[i], sgu_norm_b=sgu_norm_b[i],
                  sgu_w=sgu_w[i], sgu_b=sgu_b[i], w_branch_a=w_branch_a[i], w_branch_b=w_branch_b[i],
                  w_out=w_out[i], ffn_norm_g=ffn_norm_g[i], peer_w_q=peer_w_q[i],
                  peer_sub_keys=peer_sub_keys[i], peer_u=peer_u[i], peer_v=peer_v[i],
                  ple_norm_g=ple_norm_g[i], w_ple=w_ple[i], w_ple_gate=w_ple_gate[i])
        res = _layer(hp, hs, p_prompt[i], p_sample[i], cache_k[i], cache_v[i], lp)
        hp, hs = res[0], res[1]
        for lst, t in zip(outs, res[2:]):
            lst.append(t)
    return (hp, hs) + tuple(jnp.stack(o) for o in outs)
```

```python
import functools
import math

import jax
import jax.numpy as jnp
from jax import lax
from jax.experimental import pallas as pl
from jax.experimental.pallas import tpu as pltpu

F32 = jnp.float32
BF16 = jnp.bfloat16

N_HEADS = 8
N_KV_HEADS = 2
HEAD_DIM = 64
Q_GROUP = N_HEADS // N_KV_HEADS
WINDOW = 128
CHUNK = 128
SGU_GROUPS = 4
PEER_HEADS = 8
PEER_N_KEYS = 128
PEER_TOPK = 16
EPS = 1e-6
NEG_INF = -1e30
ALIBI_SLOPES = tuple(2.0 ** (-8.0 * h / N_HEADS) for h in range(1, N_HEADS + 1))

LANES = 128
SUBLANES = 8
MXU_COLS = 256
VMEM_LIMIT = 56 * 1024 * 1024


def _params(*semantics):
    return pltpu.CompilerParams(dimension_semantics=semantics, vmem_limit_bytes=VMEM_LIMIT)


def _token_block(*counts):
    for tb in (512, 256, 128):
        if all(n % tb == 0 for n in counts):
            return tb
    raise ValueError(f"token counts {counts} must be multiples of 128")


def _rms(x, g):
    return x * lax.rsqrt(jnp.mean(x * x, axis=-1, keepdims=True) + EPS) * g


def _group_rms(t, ones_blk, g):
    t2 = t * t
    hi = t2.astype(BF16)
    lo = (t2 - hi.astype(F32)).astype(BF16)
    ss = (jnp.dot(hi, ones_blk, preferred_element_type=F32)
          + jnp.dot(lo, ones_blk, preferred_element_type=F32))
    return t * lax.rsqrt(ss * (1.0 / HEAD_DIM) + EPS) * g


def _inproj_kernel(xp_ref, xs_ref, g_ref, w_ref, qg_ref, kg_ref, lg_ref, lb_ref, bq_ref, bk_ref,
                   q_ref, k_ref, v_ref, u_ref, vn_ref, ga_ref, gb_ref, *, prompt_blocks):
    x = jnp.where(pl.program_id(0) < prompt_blocks, xp_ref[...], xs_ref[...])
    xn = _rms(x, g_ref[...])
    z = jnp.dot(xn.astype(BF16), w_ref[...], preferred_element_type=F32)
    att = N_HEADS * HEAD_DIM
    kvw = N_KV_HEADS * HEAD_DIM
    sgw = (z.shape[1] - att - 2 * kvw) // 6
    o = 0
    q = z[:, o:o + att]; o += att
    k = z[:, o:o + kvw]; o += kvw
    v = z[:, o:o + kvw]; o += kvw
    su = z[:, o:o + sgw]; o += sgw
    sv = z[:, o:o + sgw]; o += sgw
    g_a = z[:, o:o + 2 * sgw]; o += 2 * sgw
    g_b = z[:, o:o + 2 * sgw]
    qn = _group_rms(q, bq_ref[...], qg_ref[...])
    q_ref[...] = (qn * (HEAD_DIM ** -0.5)).astype(BF16)
    k_ref[...] = _group_rms(k, bk_ref[...], kg_ref[...])
    v_ref[...] = v
    u_ref[...] = jax.nn.gelu(su).astype(BF16)
    gv = jax.nn.gelu(sv)
    mu = jnp.mean(gv, axis=-1, keepdims=True)
    gc = gv - mu
    vn_ref[...] = gc * lax.rsqrt(jnp.mean(gc * gc, axis=-1, keepdims=True) + EPS) * lg_ref[...] + lb_ref[...]
    ga_ref[...] = jax.nn.sigmoid(g_a).astype(BF16)
    gb_ref[...] = jax.nn.sigmoid(g_b).astype(BF16)


def _inproj(xp, xs, g, w_in, qg, kg, lg, lb):
    (n_p, d), n_s = xp.shape, xs.shape[0]
    n = n_p + n_s
    tb = _token_block(n_p, n_s)
    nbp = n_p // tb
    att = N_HEADS * HEAD_DIM
    kvw = N_KV_HEADS * HEAD_DIM
    sgw = d // 2
    hid = jnp.arange(att) // HEAD_DIM
    bq = (hid[:, None] == hid[None, :]).astype(BF16)
    bk = bq[:kvw, :kvw]
    const = lambda i: (0, 0)
    row = lambda i: (i, 0)
    outs = [
        jax.ShapeDtypeStruct((n, att), BF16),
        jax.ShapeDtypeStruct((n, kvw), F32),
        jax.ShapeDtypeStruct((n, kvw), F32),
        jax.ShapeDtypeStruct((n, sgw), BF16),
        jax.ShapeDtypeStruct((n, sgw), F32),
        jax.ShapeDtypeStruct((n, d), BF16),
        jax.ShapeDtypeStruct((n, d), BF16),
    ]
    return pl.pallas_call(
        functools.partial(_inproj_kernel, prompt_blocks=nbp),
        grid=(n // tb,),
        in_specs=[
            pl.BlockSpec((tb, d), lambda i: (jnp.minimum(i, nbp - 1), 0)),
            pl.BlockSpec((tb, d), lambda i: (jnp.maximum(i - nbp, 0), 0)),
            pl.BlockSpec((1, d), const),
            pl.BlockSpec(w_in.shape, const),
            pl.BlockSpec((1, att), const),
            pl.BlockSpec((1, kvw), const),
            pl.BlockSpec((1, sgw), const),
            pl.BlockSpec((1, sgw), const),
            pl.BlockSpec((att, att), const),
            pl.BlockSpec((kvw, kvw), const),
        ],
        out_specs=[pl.BlockSpec((tb, s.shape[1]), row) for s in outs],
        out_shape=outs,
        compiler_params=_params("parallel"),
        name="inproj",
    )(xp, xs, g, w_in, qg, kg, lg, lb, bq, bk)


def _sink_softmax(s, sink):
    mx = jnp.maximum(jnp.max(s, axis=-1, keepdims=True), sink)
    p = jnp.exp(s - mx)
    den = jnp.sum(p, axis=-1, keepdims=True) + jnp.exp(sink - mx)
    return p / den


def _prompt_kernel(sinks_ref, q_ref, kc_ref, kp_ref, vc_ref, vp_ref, vn_ref, u_ref, w_ref, bias_ref,
                   a_ref, m_ref):
    i = pl.program_id(1)
    tq = q_ref.shape[0]
    nblk = tq // WINDOW
    q = q_ref[...]
    kc = kc_ref[...].astype(BF16)
    vc = vc_ref[...].astype(BF16)
    kp = kp_ref[...].astype(BF16)
    vp = vp_ref[...].astype(BF16)
    row = lax.broadcasted_iota(jnp.int32, (WINDOW, 2 * WINDOW), 0)
    col = lax.broadcasted_iota(jnp.int32, (WINDOW, 2 * WINDOW), 1)
    dist = row - col + WINDOW
    in_window = (dist >= 0) & (dist < WINDOW)
    distf = dist.astype(F32)
    for jq in range(nblk):
        r0 = jq * WINDOW
        if jq == 0:
            kprev, vprev = kp, vp
            valid = in_window & (col >= jnp.where(i > 0, 0, WINDOW))
        else:
            kprev, vprev = kc[r0 - WINDOW:r0], vc[r0 - WINDOW:r0]
            valid = in_window
        kcat = jnp.concatenate([kprev, kc[r0:r0 + WINDOW]], axis=0)
        vcat = jnp.concatenate([vprev, vc[r0:r0 + WINDOW]], axis=0)
        for g in range(N_KV_HEADS):
            heads = range(g * Q_GROUP, (g + 1) * Q_GROUP)
            qg = jnp.concatenate([q[r0:r0 + WINDOW, h * HEAD_DIM:(h + 1) * HEAD_DIM] for h in heads], axis=0)
            s_all = lax.dot_general(qg, kcat[:, g * HEAD_DIM:(g + 1) * HEAD_DIM],
                                    (((1,), (1,)), ((), ())), preferred_element_type=F32)
            probs = []
            for hl, h in enumerate(heads):
                s = s_all[hl * WINDOW:(hl + 1) * WINDOW] - ALIBI_SLOPES[h] * distf
                s = jnp.where(valid, s, NEG_INF)
                probs.append(_sink_softmax(s, sinks_ref[h]).astype(BF16))
            o_all = jnp.dot(jnp.concatenate(probs, axis=0), vcat[:, g * HEAD_DIM:(g + 1) * HEAD_DIM],
                            preferred_element_type=F32)
            for hl, h in enumerate(heads):
                a_ref[r0:r0 + WINDOW, h * HEAD_DIM:(h + 1) * HEAD_DIM] = (
                    o_all[hl * WINDOW:(hl + 1) * WINDOW].astype(BF16))
    tr = lax.broadcasted_iota(jnp.int32, (CHUNK, CHUNK), 0)
    tc = lax.broadcasted_iota(jnp.int32, (CHUNK, CHUNK), 1)
    gd = vn_ref.shape[1] // SGU_GROUPS
    wm = [jnp.where(tr >= tc, w_ref[g], 0.0).astype(BF16) for g in range(SGU_GROUPS)]
    for c in range(tq // CHUNK):
        r0 = c * CHUNK
        vnc = vn_ref[r0:r0 + CHUNK, :].astype(BF16)
        for g in range(SGU_GROUPS):
            s = jnp.dot(wm[g], vnc[:, g * gd:(g + 1) * gd], preferred_element_type=F32)
            s = s + bias_ref[:, g * gd:(g + 1) * gd]
            m_ref[r0:r0 + CHUNK, g * gd:(g + 1) * gd] = (
                u_ref[r0:r0 + CHUNK, g * gd:(g + 1) * gd].astype(F32) * s).astype(BF16)


def _prompt_mix(sinks, q, k, v, vn, u, sgu_w, sgu_bias, b, s):
    n, att = b * s, q.shape[1]
    kvw = k.shape[1]
    sgw = vn.shape[1]
    tq = 512 if s % 512 == 0 else WINDOW
    assert s % tq == 0 and tq % WINDOW == 0 and WINDOW == CHUNK
    r = tq // WINDOW
    nq = s // tq
    cur = lambda bi, i: (bi * nq + i, 0)
    prev = lambda bi, i: (jnp.maximum((bi * nq + i) * r - 1, 0), 0)
    outs = [jax.ShapeDtypeStruct((n, att), BF16), jax.ShapeDtypeStruct((n, sgw), BF16)]
    return pl.pallas_call(
        _prompt_kernel,
        grid=(b, nq),
        in_specs=[
            pl.BlockSpec(memory_space=pltpu.SMEM),
            pl.BlockSpec((tq, att), cur),
            pl.BlockSpec((tq, kvw), cur),
            pl.BlockSpec((WINDOW, kvw), prev),
            pl.BlockSpec((tq, kvw), cur),
            pl.BlockSpec((WINDOW, kvw), prev),
            pl.BlockSpec((tq, sgw), cur),
            pl.BlockSpec((tq, sgw), cur),
            pl.BlockSpec(sgu_w.shape, lambda bi, i: (0, 0, 0)),
            pl.BlockSpec(sgu_bias.shape, lambda bi, i: (0, 0)),
        ],
        out_specs=[pl.BlockSpec((tq, att), cur), pl.BlockSpec((tq, sgw), cur)],
        out_shape=outs,
        compiler_params=_params("parallel", "parallel"),
        name="prompt_mix",
    )(sinks, q, k, k, v, v, vn, u, sgu_w, sgu_bias)


def _sample_kernel(sinks_ref, q_ref, kn_ref, vn_new_ref, ck_ref, cv_ref, vn_ref, u_ref, wexp_ref, bias_ref,
                   a_ref, m_ref):
    bb, w, _ = ck_ref.shape
    l = q_ref.shape[0] // bb
    per_seq = lambda ref: ref[...].astype(F32).reshape(bb, l, ref.shape[1])
    q = per_seq(q_ref)
    kcat = jnp.concatenate([ck_ref[...], per_seq(kn_ref)], axis=1).astype(BF16)
    vcat = jnp.concatenate([cv_ref[...], per_seq(vn_new_ref)], axis=1).astype(BF16)
    rows = Q_GROUP * l
    t = lax.broadcasted_iota(jnp.int32, (rows, w + l), 0) % l
    key = lax.broadcasted_iota(jnp.int32, (rows, w + l), 1)
    dist = t - (key - w)
    valid = (dist >= 0) & (dist < WINDOW)
    distf = dist.astype(F32)
    hl_of_row = lax.broadcasted_iota(jnp.int32, (rows, 1), 0) // l
    for g in range(N_KV_HEADS):
        heads = range(g * Q_GROUP, (g + 1) * Q_GROUP)
        qg = jnp.concatenate([q[:, :, h * HEAD_DIM:(h + 1) * HEAD_DIM] for h in heads], axis=1)
        s = jnp.einsum('bqd,bkd->bqk', qg.astype(BF16), kcat[:, :, g * HEAD_DIM:(g + 1) * HEAD_DIM],
                       preferred_element_type=F32)
        slope = jnp.zeros((rows, 1), F32)
        sink = jnp.zeros((rows, 1), F32)
        for hl, h in enumerate(heads):
            slope = jnp.where(hl_of_row == hl, ALIBI_SLOPES[h], slope)
            sink = jnp.where(hl_of_row == hl, sinks_ref[h], sink)
        s = jnp.where(valid[None], s - (slope * distf)[None], NEG_INF)
        p = _sink_softmax(s, sink[None]).astype(BF16)
        o = jnp.einsum('bqk,bkd->bqd', p, vcat[:, :, g * HEAD_DIM:(g + 1) * HEAD_DIM],
                       preferred_element_type=F32)
        for hl, h in enumerate(heads):
            a_ref[:, h * HEAD_DIM:(h + 1) * HEAD_DIM] = (
                o[:, hl * l:(hl + 1) * l, :].reshape(bb * l, HEAD_DIM).astype(BF16))
    vn = per_seq(vn_ref)
    tt = lax.broadcasted_iota(jnp.int32, (l, vn.shape[2]), 0)
    s = jnp.broadcast_to(bias_ref[...][None], vn.shape)
    for sp in range(l):
        wm = jnp.where(tt >= sp, wexp_ref[sp], 0.0)
        s = s + wm[None] * vn[:, sp:sp + 1, :]
    m_ref[...] = (per_seq(u_ref) * s).reshape(bb * l, vn.shape[2]).astype(BF16)


def _sample_mix(sinks, q, k, v, cache_k, cache_v, vn, u, wexp, bias, n_p):
    b, w, kvw = cache_k.shape
    n, att = q.shape
    l = (n - n_p) // b
    sgw = vn.shape[1]
    bb = 16 if b % 16 == 0 else b
    assert n_p % (bb * l) == 0
    first = n_p // (bb * l)
    tok = lambda width: pl.BlockSpec((bb * l, width), lambda i: (first + i, 0))
    own = lambda width: pl.BlockSpec((bb * l, width), lambda i: (i, 0))
    past = pl.BlockSpec((bb, w, kvw), lambda i: (i, 0, 0))
    outs = [jax.ShapeDtypeStruct((n - n_p, att), BF16), jax.ShapeDtypeStruct((n - n_p, sgw), BF16)]
    return pl.pallas_call(
        _sample_kernel,
        grid=(b // bb,),
        in_specs=[
            pl.BlockSpec(memory_space=pltpu.SMEM),
            tok(att), tok(kvw), tok(kvw), past, past, tok(sgw), tok(sgw),
            pl.BlockSpec(wexp.shape, lambda i: (0, 0, 0)),
            pl.BlockSpec(bias.shape, lambda i: (0, 0)),
        ],
        out_specs=[own(att), own(sgw)],
        out_shape=outs,
        compiler_params=_params("parallel"),
        name="sample_mix",
    )(sinks, q, k, v, cache_k, cache_v, vn, u, wexp, bias)


def _merge_kernel(x_ref, a_ref, m_ref, ga_ref, gb_ref, wa_ref, wb_ref, wo_ref, fg_ref, wq_ref, keys_ref,
                  x1_ref, xnt_ref, s1_ref, s2_ref):
    ha = jnp.dot(a_ref[...], wa_ref[...], preferred_element_type=F32)
    hb = jnp.dot(m_ref[...], wb_ref[...], preferred_element_type=F32)
    h = ga_ref[...].astype(F32) * ha + gb_ref[...].astype(F32) * hb
    x1 = x_ref[...] + jnp.dot(h.astype(BF16), wo_ref[...], preferred_element_type=F32)
    x1_ref[...] = x1
    xn32 = _rms(x1, fg_ref[...])
    xnt_ref[...] = xn32.T.astype(BF16)
    xn = xn32.astype(BF16)
    qp = jnp.dot(xn, wq_ref[...], preferred_element_type=F32).astype(BF16)
    half = keys_ref.shape[2]
    nsub = s1_ref.shape[0]
    for hc in range(keys_ref.shape[0]):
        st = lax.dot_general(keys_ref[hc], qp[:, hc * half:(hc + 1) * half],
                             (((1,), (1,)), ((), ())), preferred_element_type=F32)
        dst = s1_ref if hc % 2 == 0 else s2_ref
        for tl in range(nsub):
            dst[tl, hc // 2] = st[:, tl * LANES:(tl + 1) * LANES]


def _merge(xp, xs, ap, a_s, mp, ms, ga, gb, wa, wb, wo, fg, wq, keys):
    (n_p, d), n_s = xp.shape, xs.shape[0]
    n = n_p + n_s
    tb = _token_block(n_p, n_s)
    nbp = n_p // tb
    nsub = tb // LANES
    hc, nk, half = keys.shape
    row = lambda i: (i, 0)
    const2 = lambda i: (0, 0)
    heads = hc // 2
    outs = [
        jax.ShapeDtypeStruct((n, d), F32),
        jax.ShapeDtypeStruct((d, n), BF16),
        jax.ShapeDtypeStruct((n // LANES, heads, nk, LANES), F32),
        jax.ShapeDtypeStruct((n // LANES, heads, nk, LANES), F32),
    ]

    def body(xp_ref, xs_ref, ap_ref, as_ref, mp_ref, ms_ref, *rest):
        side = lambda p_ref, s_ref: _Value(jnp.where(pl.program_id(0) < nbp, p_ref[...], s_ref[...]))
        _merge_kernel(side(xp_ref, xs_ref), side(ap_ref, as_ref), side(mp_ref, ms_ref), *rest)

    prompt = lambda width: pl.BlockSpec((tb, width), lambda i: (jnp.minimum(i, nbp - 1), 0))
    sample = lambda width: pl.BlockSpec((tb, width), lambda i: (jnp.maximum(i - nbp, 0), 0))
    return pl.pallas_call(
        body,
        grid=(n // tb,),
        in_specs=[
            prompt(d), sample(d),
            prompt(ap.shape[1]), sample(ap.shape[1]),
            prompt(mp.shape[1]), sample(mp.shape[1]),
            pl.BlockSpec((tb, d), row),
            pl.BlockSpec((tb, d), row),
            pl.BlockSpec(wa.shape, const2),
            pl.BlockSpec(wb.shape, const2),
            pl.BlockSpec(wo.shape, const2),
            pl.BlockSpec((1, d), const2),
            pl.BlockSpec(wq.shape, const2),
            pl.BlockSpec(keys.shape, lambda i: (0, 0, 0)),
        ],
        out_specs=[
            pl.BlockSpec((tb, d), row),
            pl.BlockSpec((d, tb), lambda i: (0, i)),
            pl.BlockSpec((nsub, heads, nk, LANES), lambda i: (i, 0, 0, 0)),
            pl.BlockSpec((nsub, heads, nk, LANES), lambda i: (i, 0, 0, 0)),
        ],
        out_shape=outs,
        compiler_params=_params("parallel"),
        name="merge",
    )(xp, xs, ap, a_s, mp, ms, ga, gb, wa, wb, wo, fg, wq, keys)


class _Value:
    def __init__(self, value):
        self._value = value

    def __getitem__(self, idx):
        return self._value[idx]


def _oddeven_merge(lo, hi, r):
    step = r * 2
    if step < hi - lo:
        yield from _oddeven_merge(lo, hi, step)
        yield from _oddeven_merge(lo + r, hi, step)
        yield from [(i, i + r) for i in range(lo + r, hi - r, step)]
    else:
        yield (lo, lo + r)


def _oddeven_merge_sort(lo, hi):
    if hi - lo >= 1:
        mid = lo + (hi - lo) // 2
        yield from _oddeven_merge_sort(lo, mid)
        yield from _oddeven_merge_sort(mid + 1, hi)
        yield from _oddeven_merge(lo, hi, 1)


_SORT_TOPK = tuple(_oddeven_merge_sort(0, PEER_TOPK - 1))


def _cmpx(w, i, j):
    a, b = w[i], w[j]
    if b is None:
        return
    if a is None:
        w[i], w[j] = b, None
        return
    w[i], w[j] = jnp.maximum(a, b), jnp.minimum(a, b)


def _top_values(w):
    k = PEER_TOPK
    w = list(w)
    for i, j in _SORT_TOPK:
        _cmpx(w, i, j)
    shift = SUBLANES // 2
    while shift >= 1:
        y = [None if v is None else pltpu.roll(v, shift, 0) for v in w]
        z = []
        for r in range(k):
            a, b = w[r], y[k - 1 - r]
            z.append(b if a is None else a if b is None else jnp.maximum(a, b))
        stride = k // 2
        while stride >= 1:
            for i in range(k):
                if i & stride == 0:
                    _cmpx(z, i, i + stride)
            stride //= 2
        w = z
        shift //= 2
    return w


def _thresh_kernel(s1_ref, s2_ref, g2_ref, gm_ref, g1_ref):
    k = PEER_TOPK
    nk, lanes = s1_ref.shape[2], s1_ref.shape[3]
    nslot = nk // SUBLANES
    assert nslot == k and k == 2 * SUBLANES
    sub = lax.broadcasted_iota(jnp.int32, (SUBLANES, lanes), 0)

    def pack(vals):
        out = vals[0]
        for j in range(1, SUBLANES):
            out = jnp.where(sub == j, vals[j], out)
        return out

    def head(h, carry):
        w1 = [s1_ref[0, h, r * SUBLANES:(r + 1) * SUBLANES, :] for r in range(nslot)]
        w2 = [s2_ref[0, h, r * SUBLANES:(r + 1) * SUBLANES, :] for r in range(nslot)]
        a = _top_values(w1)
        b = _top_values(w2)
        b_lo, b_hi, a_hi = pack(b[:SUBLANES]), pack(b[SUBLANES:]), pack(a[SUBLANES:])
        cands = ([a[0] + b_lo, a[0] + b_hi] + [a[i] + b_lo for i in range(1, SUBLANES)] + [a_hi + b[0]])
        best = _top_values(cands + [None] * (k - len(cands)))
        tau = best[k - 1]
        z = jnp.ones_like(tau)
        for r in range(1, k):
            z = z + jnp.exp(best[r] - best[0])
        inv_z = 1.0 / z
        eb = [jnp.exp(b[j] - b[0]) for j in range(k)]
        gamma = []
        for i in range(k):
            t = jnp.full_like(tau, jnp.inf)
            for j in range(k // (i + 1)):
                t = jnp.where(a[i] + b[j] >= tau, eb[j], t)
            gamma.append(t)
        for r in range(nslot):
            gm = jnp.full_like(tau, jnp.inf)
            for i in range(k - 1, -1, -1):
                gm = jnp.where(w1[r] >= a[i], gamma[i], gm)
            rows = pl.ds(r * SUBLANES, SUBLANES)
            gm_ref[0, h, rows, :] = gm
            g1_ref[0, h, rows, :] = jnp.exp(w1[r] - a[0]) * (0.5 * inv_z)
            g2_ref[0, h, rows, :] = jnp.exp(w2[r] - b[0])
        return carry

    lax.fori_loop(0, s1_ref.shape[1], head, 0)


def _thresholds(s1, s2):
    nsub, heads, nk, lanes = s1.shape
    spec = pl.BlockSpec((1, heads, nk, lanes), lambda i: (i, 0, 0, 0))
    out = jax.ShapeDtypeStruct(s1.shape, F32)
    return pl.pallas_call(
        _thresh_kernel,
        grid=(nsub,),
        in_specs=[spec, spec],
        out_specs=[spec, spec, spec],
        out_shape=[out, out, out],
        compiler_params=_params("parallel"),
        name="peer_thresholds",
    )(s1, s2)


GATE_ROWS = 64
GELU_C0 = math.sqrt(2.0 / math.pi)
GELU_C1 = 0.044715 * GELU_C0
PEER_CHUNK = 512
PEER_TOKENS = 1024


def _gate_columns(at_ref, wt_ref, j0, tls, g2_ref, gm_ref, g1_ref):
    _, heads, nk, lanes = g2_ref.shape

    def tile(j, tl, r0):
        gate = jnp.zeros((GATE_ROWS, lanes), F32)
        for h in range(heads):
            gm = gm_ref[tl, h, j0 + j:j0 + j + 1, :]
            g1 = g1_ref[tl, h, j0 + j:j0 + j + 1, :]
            g2 = g2_ref[tl, h, r0:r0 + GATE_ROWS, :]
            gate = gate + jnp.where(g2 >= gm, g1 * g2, 0.0)
        rows = slice(j * nk + r0, j * nk + r0 + GATE_ROWS)
        cols = slice(tl * lanes, (tl + 1) * lanes)
        x = at_ref[rows, cols]
        t = jnp.tanh(x * (GELU_C0 + GELU_C1 * (x * x)))
        wt_ref[rows, cols] = ((x + x * t) * gate).astype(BF16)

    for tl in tls:
        for j in range(at_ref.shape[0] // nk):
            for r0 in range(0, nk, GATE_ROWS):
                tile(j, tl, r0)


def _peer_kernel(xnt_ref, u0a_ref, u0b_ref, una_ref, unb_ref, va_ref, vb_ref, g2_ref, gm_ref, g1_ref, out_ref,
                 at_a, at_b, wt_a, wt_b, acc, xnt):
    s = pl.program_id(1)
    per = at_a.shape[0] // g2_ref.shape[2]
    tb = xnt_ref.shape[1]
    lanes = g2_ref.shape[3]
    tables = (g2_ref, gm_ref, g1_ref)
    cw = min(MXU_COLS, tb)
    columns = [(slice(c, c + cw), range(c // lanes, (c + cw) // lanes)) for c in range(0, tb, cw)]

    def act(u_ref, at, cs):
        at[:, cs] = jnp.dot(u_ref[...], xnt[:, cs], preferred_element_type=F32)

    def mix(v_ref, wt, cs):
        acc[:, cs] += jnp.dot(v_ref[...], wt[:, cs], preferred_element_type=F32)

    @pl.when(s == 0)
    def _():
        xnt[...] = xnt_ref[...]
        for cs, _ in columns:
            act(u0a_ref, at_a, cs)
            act(u0b_ref, at_b, cs)
        acc[...] = jnp.zeros_like(acc)

    for at, wt, j0, v_ref, un_ref in ((at_a, wt_a, 0, va_ref, una_ref), (at_b, wt_b, per, vb_ref, unb_ref)):
        for cs, tls in columns:
            _gate_columns(at, wt, j0, tls, *tables)
            mix(v_ref, wt, cs)
            act(un_ref, at, cs)

    @pl.when(s == pl.num_programs(1) - 1)
    def _():
        out_ref[...] = acc[...].T


def _peer(xnt, u, vt, g2, gm, g1):
    d, n = xnt.shape
    ne = u.shape[0]
    tb = PEER_TOKENS if n % PEER_TOKENS == 0 else _token_block(n)
    nsub = tb // LANES
    _, heads, nk, _ = g2.shape
    ec = PEER_CHUNK
    assert ne % (2 * ec) == 0 and ec % nk == 0 and ne == nk * nk and nk % GATE_ROWS == 0
    nc = ne // ec
    per = ec // nk
    keyed2 = pl.BlockSpec((nsub, heads, nk, LANES), lambda i, s: (i, 0, 0, 0))
    keyed1 = pl.BlockSpec((nsub, heads, 2 * per, LANES), lambda i, s: (i, 0, s, 0))
    return pl.pallas_call(
        _peer_kernel,
        grid=(n // tb, nc // 2),
        in_specs=[
            pl.BlockSpec((d, tb), lambda i, s: (0, i)),
            pl.BlockSpec((ec, d), lambda i, s: (0, 0)),
            pl.BlockSpec((ec, d), lambda i, s: (1, 0)),
            pl.BlockSpec((ec, d), lambda i, s: (jnp.minimum(2 * s + 2, nc - 2), 0)),
            pl.BlockSpec((ec, d), lambda i, s: (jnp.minimum(2 * s + 3, nc - 1), 0)),
            pl.BlockSpec((d, ec), lambda i, s: (0, 2 * s)),
            pl.BlockSpec((d, ec), lambda i, s: (0, 2 * s + 1)),
            keyed2, keyed1, keyed1,
        ],
        out_specs=pl.BlockSpec((tb, d), lambda i, s: (i, 0)),
        out_shape=jax.ShapeDtypeStruct((n, d), F32),
        scratch_shapes=[
            pltpu.VMEM((ec, tb), F32),
            pltpu.VMEM((ec, tb), F32),
            pltpu.VMEM((ec, tb), BF16),
            pltpu.VMEM((ec, tb), BF16),
            pltpu.VMEM((d, tb), F32),
            pltpu.VMEM((d, tb), BF16),
        ],
        compiler_params=_params("parallel", "arbitrary"),
        name="peer_mix",
    )(xnt, u, u, u, u, vt, vt, g2, gm, g1)


def _final_kernel(x1_ref, pe_ref, p_ref, g_ref, wg_ref, wp_ref, y_ref):
    x2 = x1_ref[...] + pe_ref[...]
    xn = _rms(x2, g_ref[...]).astype(BF16)
    gate = jax.nn.sigmoid(jnp.dot(xn, wg_ref[...], preferred_element_type=F32))
    y_ref[...] = x2 + gate * jnp.dot(p_ref[...].astype(BF16), wp_ref[...], preferred_element_type=F32)


def _final(x1, pe, p, g, wg, wp, row0):
    n, pd = p.shape
    d = x1.shape[1]
    tb = _token_block(n, row0)
    first = row0 // tb
    src = lambda i: (first + i, 0)
    row = lambda i: (i, 0)
    const = lambda i: (0, 0)
    return pl.pallas_call(
        _final_kernel,
        grid=(n // tb,),
        in_specs=[
            pl.BlockSpec((tb, d), src),
            pl.BlockSpec((tb, d), src),
            pl.BlockSpec((tb, pd), row),
            pl.BlockSpec((1, d), const),
            pl.BlockSpec(wg.shape, const),
            pl.BlockSpec(wp.shape, const),
        ],
        out_specs=pl.BlockSpec((tb, d), row),
        out_shape=jax.ShapeDtypeStruct((n, d), F32),
        compiler_params=_params("parallel"),
        name="ple_epilogue",
    )(x1, pe, p, g, wg, wp)


def _layer(xp, xs, pp, ps, past_k, past_v, lp):
    b, s, d = xp.shape
    bd, l, _ = xs.shape
    n_p, n_s = b * s, bd * l
    kvw = N_KV_HEADS * HEAD_DIM
    sgw = d // 2
    gd = sgw // SGU_GROUPS
    xp2, xs2 = xp.reshape(n_p, d), xs.reshape(n_s, d)

    q, k, v, u, vn, ga, gb = _inproj(
        xp2, xs2, lp['attn_norm_g'][None], lp['w_in'].astype(BF16),
        jnp.tile(lp['q_norm_g'], N_HEADS)[None], jnp.tile(lp['k_norm_g'], N_KV_HEADS)[None],
        lp['sgu_norm_g'][None], lp['sgu_norm_b'][None])

    sgu_w, sgu_b = lp['sgu_w'], lp['sgu_b']
    bias_p = jnp.repeat(sgu_b.T, gd, axis=1)
    a_p, m_p = _prompt_mix(lp['attn_sinks'], q, k, v, vn, u, sgu_w, bias_p, b, s)
    wexp = jnp.repeat(jnp.transpose(sgu_w[:, :l, :l], (2, 1, 0)), gd, axis=2)
    a_s, m_s = _sample_mix(lp['attn_sinks'], q, k, v, past_k.reshape(bd, -1, kvw),
                           past_v.reshape(bd, -1, kvw), vn, u, wexp, bias_p[:l], n_p)

    keys = lp['peer_sub_keys'].reshape(2 * PEER_HEADS, PEER_N_KEYS, -1).astype(BF16)
    x1, xn1, s1, s2 = _merge(xp2, xs2, a_p, a_s, m_p, m_s, ga, gb, lp['w_branch_a'].astype(BF16),
                             lp['w_branch_b'].astype(BF16), lp['w_out'].astype(BF16), lp['ffn_norm_g'][None],
                             lp['peer_w_q'].astype(BF16), keys)
    g2, gm, g1 = _thresholds(s1, s2)
    pe = _peer(xn1, lp['peer_u'].astype(BF16), lp['peer_v'].astype(BF16).T, g2, gm, g1)
    ple = (lp['ple_norm_g'][None], lp['w_ple_gate'].astype(BF16), lp['w_ple'].astype(BF16))
    y_p = _final(x1, pe, pp.reshape(n_p, -1), *ple, 0)
    y_s = _final(x1, pe, ps.reshape(n_s, -1), *ple, n_p)

    wp = min(WINDOW, s)
    tail = lambda t, rows: jnp.stack([t[(bi + 1) * s - rows:(bi + 1) * s] for bi in range(b)])
    heads = lambda t: t.reshape(t.shape[0], t.shape[1], N_KV_HEADS, HEAD_DIM)
    return (y_p.reshape(b, s, d), y_s.reshape(bd, l, d),
            heads(tail(k, wp)), heads(tail(v, wp)),
            heads(k[n_p:].reshape(bd, l, kvw)), heads(v[n_p:].reshape(bd, l, kvw)),
            tail(vn, CHUNK), vn[n_p:].reshape(bd, l, sgw))


def kernel(x_prompt, x_sample, cache_k, cache_v, p_prompt, p_sample, attn_norm_g, w_in, q_norm_g, k_norm_g, attn_sinks, sgu_norm_g, sgu_norm_b, sgu_w, sgu_b, w_branch_a, w_branch_b, w_out, ffn_norm_g, peer_w_q, peer_sub_keys, peer_u, peer_v, ple_norm_g, w_ple, w_ple_gate):
    depth = w_in.shape[0]
    hp, hs = x_prompt, x_sample
    outs = [[] for _ in range(6)]
    for i in range(depth):
        lp = dict(attn_norm_g=attn_norm_g[i], w_in=w_in[i], q_norm_g=q_norm_g[i], k_norm_g=k_norm_g[i],
                  attn_sinks=attn_sinks[i], sgu_norm_g=sgu_norm_g[i], sgu_norm_b=sgu_norm_b[i],
                  sgu_w=sgu_w[i], sgu_b=sgu_b[i], w_branch_a=w_branch_a[i], w_branch_b=w_branch_b[i],
                  w_out=w_out[i], ffn_norm_g=ffn_norm_g[i], peer_w_q=peer_w_q[i],
                  peer_sub_keys=peer_sub_keys[i], peer_u=peer_u[i], peer_v=peer_v[i],
                  ple_norm_g=ple_norm_g[i], w_ple=w_ple[i], w_ple_gate=w_ple_gate[i])
        res = _layer(hp, hs, p_prompt[i], p_sample[i], cache_k[i], cache_v[i], lp)
        hp, hs = res[0], res[1]
        for lst, t in zip(outs, res[2:]):
            lst.append(t)
    return (hp, hs) + tuple(jnp.stack(o) for o in outs)
```

```python
import functools
import math

import jax
import jax.numpy as jnp
from jax import lax
from jax.experimental import pallas as pl
from jax.experimental.pallas import tpu as pltpu

F32 = jnp.float32
BF16 = jnp.bfloat16

N_HEADS = 8
N_KV_HEADS = 2
HEAD_DIM = 64
Q_GROUP = N_HEADS // N_KV_HEADS
WINDOW = 128
CHUNK = 128
SGU_GROUPS = 4
PEER_HEADS = 8
PEER_N_KEYS = 128
PEER_TOPK = 16
EPS = 1e-6
NEG_INF = -1e30
ALIBI_SLOPES = tuple(2.0 ** (-8.0 * h / N_HEADS) for h in range(1, N_HEADS + 1))

LANES = 128
SUBLANES = 8
MXU_COLS = 256
VMEM_LIMIT = 56 * 1024 * 1024


def _params(*semantics):
    return pltpu.CompilerParams(dimension_semantics=semantics, vmem_limit_bytes=VMEM_LIMIT)


def _token_block(*counts):
    for tb in (512, 256, 128):
        if all(n % tb == 0 for n in counts):
            return tb
    raise ValueError(f"token counts {counts} must be multiples of 128")


def _rms(x, g):
    return x * lax.rsqrt(jnp.mean(x * x, axis=-1, keepdims=True) + EPS) * g


def _group_rms(t, ones_blk, g):
    t2 = t * t
    hi = t2.astype(BF16)
    lo = (t2 - hi.astype(F32)).astype(BF16)
    ss = (jnp.dot(hi, ones_blk, preferred_element_type=F32)
          + jnp.dot(lo, ones_blk, preferred_element_type=F32))
    return t * lax.rsqrt(ss * (1.0 / HEAD_DIM) + EPS) * g


def _inproj_kernel(xp_ref, xs_ref, g_ref, w_ref, qg_ref, kg_ref, lg_ref, lb_ref, bq_ref, bk_ref,
                   q_ref, k_ref, v_ref, u_ref, vn_ref, ga_ref, gb_ref, *, prompt_blocks):
    x = jnp.where(pl.program_id(0) < prompt_blocks, xp_ref[...], xs_ref[...])
    xn = _rms(x, g_ref[...])
    z = jnp.dot(xn.astype(BF16), w_ref[...], preferred_element_type=F32)
    att = N_HEADS * HEAD_DIM
    kvw = N_KV_HEADS * HEAD_DIM
    sgw = (z.shape[1] - att - 2 * kvw) // 6
    o = 0
    q = z[:, o:o + att]; o += att
    k = z[:, o:o + kvw]; o += kvw
    v = z[:, o:o + kvw]; o += kvw
    su = z[:, o:o + sgw]; o += sgw
    sv = z[:, o:o + sgw]; o += sgw
    g_a = z[:, o:o + 2 * sgw]; o += 2 * sgw
    g_b = z[:, o:o + 2 * sgw]
    qn = _group_rms(q, bq_ref[...], qg_ref[...])
    q_ref[...] = (qn * (HEAD_DIM ** -0.5)).astype(BF16)
    k_ref[...] = _group_rms(k, bk_ref[...], kg_ref[...])
    v_ref[...] = v
    u_ref[...] = jax.nn.gelu(su).astype(BF16)
    gv = jax.nn.gelu(sv)
    mu = jnp.mean(gv, axis=-1, keepdims=True)
    gc = gv - mu
    vn_ref[...] = gc * lax.rsqrt(jnp.mean(gc * gc, axis=-1, keepdims=True) + EPS) * lg_ref[...] + lb_ref[...]
    ga_ref[...] = jax.nn.sigmoid(g_a).astype(BF16)
    gb_ref[...] = jax.nn.sigmoid(g_b).astype(BF16)


def _inproj(xp, xs, g, w_in, qg, kg, lg, lb):
    (n_p, d), n_s = xp.shape, xs.shape[0]
    n = n_p + n_s
    tb = _token_block(n_p, n_s)
    nbp = n_p // tb
    att = N_HEADS * HEAD_DIM
    kvw = N_KV_HEADS * HEAD_DIM
    sgw = d // 2
    hid = jnp.arange(att) // HEAD_DIM
    bq = (hid[:, None] == hid[None, :]).astype(BF16)
    bk = bq[:kvw, :kvw]
    const = lambda i: (0, 0)
    row = lambda i: (i, 0)
    outs = [
        jax.ShapeDtypeStruct((n, att), BF16),
        jax.ShapeDtypeStruct((n, kvw), F32),
        jax.ShapeDtypeStruct((n, kvw), F32),
        jax.ShapeDtypeStruct((n, sgw), BF16),
        jax.ShapeDtypeStruct((n, sgw), F32),
        jax.ShapeDtypeStruct((n, d), BF16),
        jax.ShapeDtypeStruct((n, d), BF16),
    ]
    return pl.pallas_call(
        functools.partial(_inproj_kernel, prompt_blocks=nbp),
        grid=(n // tb,),
        in_specs=[
            pl.BlockSpec((tb, d), lambda i: (jnp.minimum(i, nbp - 1), 0)),
            pl.BlockSpec((tb, d), lambda i: (jnp.maximum(i - nbp, 0), 0)),
            pl.BlockSpec((1, d), const),
            pl.BlockSpec(w_in.shape, const),
            pl.BlockSpec((1, att), const),
            pl.BlockSpec((1, kvw), const),
            pl.BlockSpec((1, sgw), const),
            pl.BlockSpec((1, sgw), const),
            pl.BlockSpec((att, att), const),
            pl.BlockSpec((kvw, kvw), const),
        ],
        out_specs=[pl.BlockSpec((tb, s.shape[1]), row) for s in outs],
        out_shape=outs,
        compiler_params=_params("parallel"),
        name="inproj",
    )(xp, xs, g, w_in, qg, kg, lg, lb, bq, bk)


def _sink_softmax(s, sink):
    mx = jnp.maximum(jnp.max(s, axis=-1, keepdims=True), sink)
    p = jnp.exp(s - mx)
    den = jnp.sum(p, axis=-1, keepdims=True) + jnp.exp(sink - mx)
    return p / den


def _prompt_kernel(sinks_ref, q_ref, kc_ref, kp_ref, vc_ref, vp_ref, vn_ref, u_ref, w_ref, bias_ref,
                   a_ref, m_ref):
    i = pl.program_id(1)
    tq = q_ref.shape[0]
    nblk = tq // WINDOW
    q = q_ref[...]
    kc = kc_ref[...].astype(BF16)
    vc = vc_ref[...].astype(BF16)
    kp = kp_ref[...].astype(BF16)
    vp = vp_ref[...].astype(BF16)
    row = lax.broadcasted_iota(jnp.int32, (WINDOW, 2 * WINDOW), 0)
    col = lax.broadcasted_iota(jnp.int32, (WINDOW, 2 * WINDOW), 1)
    dist = row - col + WINDOW
    in_window = (dist >= 0) & (dist < WINDOW)
    distf = dist.astype(F32)
    for jq in range(nblk):
        r0 = jq * WINDOW
        if jq == 0:
            kprev, vprev = kp, vp
            valid = in_window & (col >= jnp.where(i > 0, 0, WINDOW))
        else:
            kprev, vprev = kc[r0 - WINDOW:r0], vc[r0 - WINDOW:r0]
            valid = in_window
        kcat = jnp.concatenate([kprev, kc[r0:r0 + WINDOW]], axis=0)
        vcat = jnp.concatenate([vprev, vc[r0:r0 + WINDOW]], axis=0)
        for g in range(N_KV_HEADS):
            heads = range(g * Q_GROUP, (g + 1) * Q_GROUP)
            qg = jnp.concatenate([q[r0:r0 + WINDOW, h * HEAD_DIM:(h + 1) * HEAD_DIM] for h in heads], axis=0)
            s_all = lax.dot_general(qg, kcat[:, g * HEAD_DIM:(g + 1) * HEAD_DIM],
                                    (((1,), (1,)), ((), ())), preferred_element_type=F32)
            probs = []
            for hl, h in enumerate(heads):
                s = s_all[hl * WINDOW:(hl + 1) * WINDOW] - ALIBI_SLOPES[h] * distf
                s = jnp.where(valid, s, NEG_INF)
                probs.append(_sink_softmax(s, sinks_ref[h]).astype(BF16))
            o_all = jnp.dot(jnp.concatenate(probs, axis=0), vcat[:, g * HEAD_DIM:(g + 1) * HEAD_DIM],
                            preferred_element_type=F32)
            for hl, h in enumerate(heads):
                a_ref[r0:r0 + WINDOW, h * HEAD_DIM:(h + 1) * HEAD_DIM] = (
                    o_all[hl * WINDOW:(hl + 1) * WINDOW].astype(BF16))
    tr = lax.broadcasted_iota(jnp.int32, (CHUNK, CHUNK), 0)
    tc = lax.broadcasted_iota(jnp.int32, (CHUNK, CHUNK), 1)
    gd = vn_ref.shape[1] // SGU_GROUPS
    wm = [jnp.where(tr >= tc, w_ref[g], 0.0).astype(BF16) for g in range(SGU_GROUPS)]
    for c in range(tq // CHUNK):
        r0 = c * CHUNK
        vnc = vn_ref[r0:r0 + CHUNK, :].astype(BF16)
        for g in range(SGU_GROUPS):
            s = jnp.dot(wm[g], vnc[:, g * gd:(g + 1) * gd], preferred_element_type=F32)
            s = s + bias_ref[:, g * gd:(g + 1) * gd]
            m_ref[r0:r0 + CHUNK, g * gd:(g + 1) * gd] = (
                u_ref[r0:r0 + CHUNK, g * gd:(g + 1) * gd].astype(F32) * s).astype(BF16)


def _prompt_mix(sinks, q, k, v, vn, u, sgu_w, sgu_bias, b, s):
    n, att = b * s, q.shape[1]
    kvw = k.shape[1]
    sgw = vn.shape[1]
    tq = 512 if s % 512 == 0 else WINDOW
    assert s % tq == 0 and tq % WINDOW == 0 and WINDOW == CHUNK
    r = tq // WINDOW
    nq = s // tq
    cur = lambda bi, i: (bi * nq + i, 0)
    prev = lambda bi, i: (jnp.maximum((bi * nq + i) * r - 1, 0), 0)
    outs = [jax.ShapeDtypeStruct((n, att), BF16), jax.ShapeDtypeStruct((n, sgw), BF16)]
    return pl.pallas_call(
        _prompt_kernel,
        grid=(b, nq),
        in_specs=[
            pl.BlockSpec(memory_space=pltpu.SMEM),
            pl.BlockSpec((tq, att), cur),
            pl.BlockSpec((tq, kvw), cur),
            pl.BlockSpec((WINDOW, kvw), prev),
            pl.BlockSpec((tq, kvw), cur),
            pl.BlockSpec((WINDOW, kvw), prev),
            pl.BlockSpec((tq, sgw), cur),
            pl.BlockSpec((tq, sgw), cur),
            pl.BlockSpec(sgu_w.shape, lambda bi, i: (0, 0, 0)),
            pl.BlockSpec(sgu_bias.shape, lambda bi, i: (0, 0)),
        ],
        out_specs=[pl.BlockSpec((tq, att), cur), pl.BlockSpec((tq, sgw), cur)],
        out_shape=outs,
        compiler_params=_params("parallel", "parallel"),
        name="prompt_mix",
    )(sinks, q, k, k, v, v, vn, u, sgu_w, sgu_bias)


def _sample_kernel(sinks_ref, q_ref, kn_ref, vn_new_ref, ck_ref, cv_ref, vn_ref, u_ref, wexp_ref, bias_ref,
                   a_ref, m_ref):
    bb, w, _ = ck_ref.shape
    l = q_ref.shape[0] // bb
    per_seq = lambda ref: ref[...].astype(F32).reshape(bb, l, ref.shape[1])
    q = per_seq(q_ref)
    kcat = jnp.concatenate([ck_ref[...], per_seq(kn_ref)], axis=1).astype(BF16)
    vcat = jnp.concatenate([cv_ref[...], per_seq(vn_new_ref)], axis=1).astype(BF16)
    rows = Q_GROUP * l
    t = lax.broadcasted_iota(jnp.int32, (rows, w + l), 0) % l
    key = lax.broadcasted_iota(jnp.int32, (rows, w + l), 1)
    dist = t - (key - w)
    valid = (dist >= 0) & (dist < WINDOW)
    distf = dist.astype(F32)
    hl_of_row = lax.broadcasted_iota(jnp.int32, (rows, 1), 0) // l
    for g in range(N_KV_HEADS):
        heads = range(g * Q_GROUP, (g + 1) * Q_GROUP)
        qg = jnp.concatenate([q[:, :, h * HEAD_DIM:(h + 1) * HEAD_DIM] for h in heads], axis=1)
        s = jnp.einsum('bqd,bkd->bqk', qg.astype(BF16), kcat[:, :, g * HEAD_DIM:(g + 1) * HEAD_DIM],
                       preferred_element_type=F32)
        slope = jnp.zeros((rows, 1), F32)
        sink = jnp.zeros((rows, 1), F32)
        for hl, h in enumerate(heads):
            slope = jnp.where(hl_of_row == hl, ALIBI_SLOPES[h], slope)
            sink = jnp.where(hl_of_row == hl, sinks_ref[h], sink)
        s = jnp.where(valid[None], s - (slope * distf)[None], NEG_INF)
        p = _sink_softmax(s, sink[None]).astype(BF16)
        o = jnp.einsum('bqk,bkd->bqd', p, vcat[:, :, g * HEAD_DIM:(g + 1) * HEAD_DIM],
                       preferred_element_type=F32)
        for hl, h in enumerate(heads):
            a_ref[:, h * HEAD_DIM:(h + 1) * HEAD_DIM] = (
                o[:, hl * l:(hl + 1) * l, :].reshape(bb * l, HEAD_DIM).astype(BF16))
    vn = per_seq(vn_ref)
    tt = lax.broadcasted_iota(jnp.int32, (l, vn.shape[2]), 0)
    s = jnp.broadcast_to(bias_ref[...][None], vn.shape)
    for sp in range(l):
        wm = jnp.where(tt >= sp, wexp_ref[sp], 0.0)
        s = s + wm[None] * vn[:, sp:sp + 1, :]
    m_ref[...] = (per_seq(u_ref) * s).reshape(bb * l, vn.shape[2]).astype(BF16)


def _sample_mix(sinks, q, k, v, cache_k, cache_v, vn, u, wexp, bias, n_p):
    b, w, kvw = cache_k.shape
    n, att = q.shape
    l = (n - n_p) // b
    sgw = vn.shape[1]
    bb = 16 if b % 16 == 0 else b
    assert n_p % (bb * l) == 0
    first = n_p // (bb * l)
    tok = lambda width: pl.BlockSpec((bb * l, width), lambda i: (first + i, 0))
    own = lambda width: pl.BlockSpec((bb * l, width), lambda i: (i, 0))
    past = pl.BlockSpec((bb, w, kvw), lambda i: (i, 0, 0))
    outs = [jax.ShapeDtypeStruct((n - n_p, att), BF16), jax.ShapeDtypeStruct((n - n_p, sgw), BF16)]
    return pl.pallas_call(
        _sample_kernel,
        grid=(b // bb,),
        in_specs=[
            pl.BlockSpec(memory_space=pltpu.SMEM),
            tok(att), tok(kvw), tok(kvw), past, past, tok(sgw), tok(sgw),
            pl.BlockSpec(wexp.shape, lambda i: (0, 0, 0)),
            pl.BlockSpec(bias.shape, lambda i: (0, 0)),
        ],
        out_specs=[own(att), own(sgw)],
        out_shape=outs,
        compiler_params=_params("parallel"),
        name="sample_mix",
    )(sinks, q, k, v, cache_k, cache_v, vn, u, wexp, bias)


def _merge_kernel(x_ref, a_ref, m_ref, ga_ref, gb_ref, wa_ref, wb_ref, wo_ref, fg_ref, wq_ref, keys_ref,
                  x1_ref, xnt_ref, s1_ref, s2_ref):
    ha = jnp.dot(a_ref[...], wa_ref[...], preferred_element_type=F32)
    hb = jnp.dot(m_ref[...], wb_ref[...], preferred_element_type=F32)
    h = ga_ref[...].astype(F32) * ha + gb_ref[...].astype(F32) * hb
    x1 = x_ref[...] + jnp.dot(h.astype(BF16), wo_ref[...], preferred_element_type=F32)
    x1_ref[...] = x1
    xn32 = _rms(x1, fg_ref[...])
    xnt_ref[...] = xn32.T.astype(BF16)
    xn = xn32.astype(BF16)
    qp = jnp.dot(xn, wq_ref[...], preferred_element_type=F32).astype(BF16)
    half = keys_ref.shape[2]
    nsub = s1_ref.shape[0]
    for hc in range(keys_ref.shape[0]):
        st = lax.dot_general(keys_ref[hc], qp[:, hc * half:(hc + 1) * half],
                             (((1,), (1,)), ((), ())), preferred_element_type=F32)
        dst = s1_ref if hc % 2 == 0 else s2_ref
        for tl in range(nsub):
            dst[tl, hc // 2] = st[:, tl * LANES:(tl + 1) * LANES]


def _merge(xp, xs, ap, a_s, mp, ms, ga, gb, wa, wb, wo, fg, wq, keys):
    (n_p, d), n_s = xp.shape, xs.shape[0]
    n = n_p + n_s
    tb = _token_block(n_p, n_s)
    nbp = n_p // tb
    nsub = tb // LANES
    hc, nk, half = keys.shape
    row = lambda i: (i, 0)
    const2 = lambda i: (0, 0)
    heads = hc // 2
    outs = [
        jax.ShapeDtypeStruct((n, d), F32),
        jax.ShapeDtypeStruct((d, n), BF16),
        jax.ShapeDtypeStruct((n // LANES, heads, nk, LANES), F32),
        jax.ShapeDtypeStruct((n // LANES, heads, nk, LANES), F32),
    ]

    def body(xp_ref, xs_ref, ap_ref, as_ref, mp_ref, ms_ref, *rest):
        side = lambda p_ref, s_ref: _Value(jnp.where(pl.program_id(0) < nbp, p_ref[...], s_ref[...]))
        _merge_kernel(side(xp_ref, xs_ref), side(ap_ref, as_ref), side(mp_ref, ms_ref), *rest)

    prompt = lambda width: pl.BlockSpec((tb, width), lambda i: (jnp.minimum(i, nbp - 1), 0))
    sample = lambda width: pl.BlockSpec((tb, width), lambda i: (jnp.maximum(i - nbp, 0), 0))
    return pl.pallas_call(
        body,
        grid=(n // tb,),
        in_specs=[
            prompt(d), sample(d),
            prompt(ap.shape[1]), sample(ap.shape[1]),
            prompt(mp.shape[1]), sample(mp.shape[1]),
            pl.BlockSpec((tb, d), row),
            pl.BlockSpec((tb, d), row),
            pl.BlockSpec(wa.shape, const2),
            pl.BlockSpec(wb.shape, const2),
            pl.BlockSpec(wo.shape, const2),
            pl.BlockSpec((1, d), const2),
            pl.BlockSpec(wq.shape, const2),
            pl.BlockSpec(keys.shape, lambda i: (0, 0, 0)),
        ],
        out_specs=[
            pl.BlockSpec((tb, d), row),
            pl.BlockSpec((d, tb), lambda i: (0, i)),
            pl.BlockSpec((nsub, heads, nk, LANES), lambda i: (i, 0, 0, 0)),
            pl.BlockSpec((nsub, heads, nk, LANES), lambda i: (i, 0, 0, 0)),
        ],
        out_shape=outs,
        compiler_params=_params("parallel"),
        name="merge",
    )(xp, xs, ap, a_s, mp, ms, ga, gb, wa, wb, wo, fg, wq, keys)


class _Value:
    def __init__(self, value):
        self._value = value

    def __getitem__(self, idx):
        return self._value[idx]


def _oddeven_merge(lo, hi, r):
    step = r * 2
    if step < hi - lo:
        yield from _oddeven_merge(lo, hi, step)
        yield from _oddeven_merge(lo + r, hi, step)
        yield from [(i, i + r) for i in range(lo + r, hi - r, step)]
    else:
        yield (lo, lo + r)


def _oddeven_merge_sort(lo, hi):
    if hi - lo >= 1:
        mid = lo + (hi - lo) // 2
        yield from _oddeven_merge_sort(lo, mid)
        yield from _oddeven_merge_sort(mid + 1, hi)
        yield from _oddeven_merge(lo, hi, 1)


_SORT_TOPK = tuple(_oddeven_merge_sort(0, PEER_TOPK - 1))


def _cmpx(w, i, j):
    a, b = w[i], w[j]
    if b is None:
        return
    if a is None:
        w[i], w[j] = b, None
        return
    w[i], w[j] = jnp.maximum(a, b), jnp.minimum(a, b)


def _top_values(w):
    k = PEER_TOPK
    w = list(w)
    for i, j in _SORT_TOPK:
        _cmpx(w, i, j)
    shift = SUBLANES // 2
    while shift >= 1:
        y = [None if v is None else pltpu.roll(v, shift, 0) for v in w]
        z = []
        for r in range(k):
            a, b = w[r], y[k - 1 - r]
            z.append(b if a is None else a if b is None else jnp.maximum(a, b))
        stride = k // 2
        while stride >= 1:
            for i in range(k):
                if i & stride == 0:
                    _cmpx(z, i, i + stride)
            stride //= 2
        w = z
        shift //= 2
    return w


def _thresh_kernel(s1_ref, s2_ref, g2_ref, gm_ref, g1_ref):
    k = PEER_TOPK
    nk, lanes = s1_ref.shape[2], s1_ref.shape[3]
    nslot = nk // SUBLANES
    assert nslot == k and k == 2 * SUBLANES
    sub = lax.broadcasted_iota(jnp.int32, (SUBLANES, lanes), 0)

    def pack(vals):
        out = vals[0]
        for j in range(1, SUBLANES):
            out = jnp.where(sub == j, vals[j], out)
        return out

    def head(h, carry):
        w1 = [s1_ref[0, h, r * SUBLANES:(r + 1) * SUBLANES, :] for r in range(nslot)]
        w2 = [s2_ref[0, h, r * SUBLANES:(r + 1) * SUBLANES, :] for r in range(nslot)]
        a = _top_values(w1)
        b = _top_values(w2)
        b_lo, b_hi, a_hi = pack(b[:SUBLANES]), pack(b[SUBLANES:]), pack(a[SUBLANES:])
        cands = ([a[0] + b_lo, a[0] + b_hi] + [a[i] + b_lo for i in range(1, SUBLANES)] + [a_hi + b[0]])
        best = _top_values(cands + [None] * (k - len(cands)))
        tau = best[k - 1]
        z = jnp.ones_like(tau)
        for r in range(1, k):
            z = z + jnp.exp(best[r] - best[0])
        inv_z = 1.0 / z
        eb = [jnp.exp(b[j] - b[0]) for j in range(k)]
        gamma = []
        for i in range(k):
            t = jnp.full_like(tau, jnp.inf)
            for j in range(k // (i + 1)):
                t = jnp.where(a[i] + b[j] >= tau, eb[j], t)
            gamma.append(t)
        for r in range(nslot):
            gm = jnp.full_like(tau, jnp.inf)
            for i in range(k - 1, -1, -1):
                gm = jnp.where(w1[r] >= a[i], gamma[i], gm)
            rows = pl.ds(r * SUBLANES, SUBLANES)
            gm_ref[0, h, rows, :] = gm
            g1_ref[0, h, rows, :] = jnp.exp(w1[r] - a[0]) * (0.5 * inv_z)
            g2_ref[0, h, rows, :] = jnp.exp(w2[r] - b[0])
        return carry

    lax.fori_loop(0, s1_ref.shape[1], head, 0)


def _thresholds(s1, s2):
    nsub, heads, nk, lanes = s1.shape
    spec = pl.BlockSpec((1, heads, nk, lanes), lambda i: (i, 0, 0, 0))
    out = jax.ShapeDtypeStruct(s1.shape, F32)
    return pl.pallas_call(
        _thresh_kernel,
        grid=(nsub,),
        in_specs=[spec, spec],
        out_specs=[spec, spec, spec],
        out_shape=[out, out, out],
        compiler_params=_params("parallel"),
        name="peer_thresholds",
    )(s1, s2)


GATE_ROWS = 128
GELU_C0 = math.sqrt(2.0 / math.pi)
GELU_C1 = 0.044715 * GELU_C0
PEER_CHUNK = 512
PEER_TOKENS = 1024


def _gate_columns(at_ref, wt_ref, j0, tls, g2_ref, gm_ref, g1_ref):
    _, heads, nk, lanes = g2_ref.shape

    def tile(j, tl, r0):
        gate = jnp.zeros((GATE_ROWS, lanes), F32)
        for h in range(heads):
            gm = gm_ref[tl, h, j0 + j:j0 + j + 1, :]
            g1 = g1_ref[tl, h, j0 + j:j0 + j + 1, :]
            g2 = g2_ref[tl, h, r0:r0 + GATE_ROWS, :]
            gate = gate + jnp.where(g2 >= gm, g1 * g2, 0.0)
        rows = slice(j * nk + r0, j * nk + r0 + GATE_ROWS)
        cols = slice(tl * lanes, (tl + 1) * lanes)
        x = at_ref[rows, cols]
        t = jnp.tanh(x * (GELU_C0 + GELU_C1 * (x * x)))
        wt_ref[rows, cols] = ((x + x * t) * gate).astype(BF16)

    for tl in tls:
        for j in range(at_ref.shape[0] // nk):
            for r0 in range(0, nk, GATE_ROWS):
                tile(j, tl, r0)


def _peer_kernel(xnt_ref, u0a_ref, u0b_ref, una_ref, unb_ref, va_ref, vb_ref, g2_ref, gm_ref, g1_ref, out_ref,
                 at_a, at_b, wt_a, wt_b, acc, xnt):
    s = pl.program_id(1)
    per = at_a.shape[0] // g2_ref.shape[2]
    tb = xnt_ref.shape[1]
    lanes = g2_ref.shape[3]
    tables = (g2_ref, gm_ref, g1_ref)
    cw = min(MXU_COLS, tb)
    columns = [(slice(c, c + cw), range(c // lanes, (c + cw) // lanes)) for c in range(0, tb, cw)]

    def act(u_ref, at, cs):
        at[:, cs] = jnp.dot(u_ref[...], xnt[:, cs], preferred_element_type=F32)

    def mix(v_ref, wt, cs):
        acc[:, cs] += jnp.dot(v_ref[...], wt[:, cs], preferred_element_type=F32)

    @pl.when(s == 0)
    def _():
        xnt[...] = xnt_ref[...]
        for cs, _ in columns:
            act(u0a_ref, at_a, cs)
            act(u0b_ref, at_b, cs)
        acc[...] = jnp.zeros_like(acc)

    for at, wt, j0, v_ref, un_ref in ((at_a, wt_a, 0, va_ref, una_ref), (at_b, wt_b, per, vb_ref, unb_ref)):
        for cs, tls in columns:
            _gate_columns(at, wt, j0, tls, *tables)
            mix(v_ref, wt, cs)
            act(un_ref, at, cs)

    @pl.when(s == pl.num_programs(1) - 1)
    def _():
        out_ref[...] = acc[...].T


def _peer(xnt, u, vt, g2, gm, g1):
    d, n = xnt.shape
    ne = u.shape[0]
    tb = PEER_TOKENS if n % PEER_TOKENS == 0 else _token_block(n)
    nsub = tb // LANES
    _, heads, nk, _ = g2.shape
    ec = PEER_CHUNK
    assert ne % (2 * ec) == 0 and ec % nk == 0 and ne == nk * nk and nk % GATE_ROWS == 0
    nc = ne // ec
    per = ec // nk
    keyed2 = pl.BlockSpec((nsub, heads, nk, LANES), lambda i, s: (i, 0, 0, 0))
    keyed1 = pl.BlockSpec((nsub, heads, 2 * per, LANES), lambda i, s: (i, 0, s, 0))
    return pl.pallas_call(
        _peer_kernel,
        grid=(n // tb, nc // 2),
        in_specs=[
            pl.BlockSpec((d, tb), lambda i, s: (0, i)),
            pl.BlockSpec((ec, d), lambda i, s: (0, 0)),
            pl.BlockSpec((ec, d), lambda i, s: (1, 0)),
            pl.BlockSpec((ec, d), lambda i, s: (jnp.minimum(2 * s + 2, nc - 2), 0)),
            pl.BlockSpec((ec, d), lambda i, s: (jnp.minimum(2 * s + 3, nc - 1), 0)),
            pl.BlockSpec((d, ec), lambda i, s: (0, 2 * s)),
            pl.BlockSpec((d, ec), lambda i, s: (0, 2 * s + 1)),
            keyed2, keyed1, keyed1,
        ],
        out_specs=pl.BlockSpec((tb, d), lambda i, s: (i, 0)),
        out_shape=jax.ShapeDtypeStruct((n, d), F32),
        scratch_shapes=[
            pltpu.VMEM((ec, tb), F32),
            pltpu.VMEM((ec, tb), F32),
            pltpu.VMEM((ec, tb), BF16),
            pltpu.VMEM((ec, tb), BF16),
            pltpu.VMEM((d, tb), F32),
            pltpu.VMEM((d, tb), BF16),
        ],
        compiler_params=_params("parallel", "arbitrary"),
        name="peer_mix",
    )(xnt, u, u, u, u, vt, vt, g2, gm, g1)


def _final_kernel(x1_ref, pe_ref, p_ref, g_ref, wg_ref, wp_ref, y_ref):
    x2 = x1_ref[...] + pe_ref[...]
    xn = _rms(x2, g_ref[...]).astype(BF16)
    gate = jax.nn.sigmoid(jnp.dot(xn, wg_ref[...], preferred_element_type=F32))
    y_ref[...] = x2 + gate * jnp.dot(p_ref[...].astype(BF16), wp_ref[...], preferred_element_type=F32)


def _final(x1, pe, p, g, wg, wp, row0):
    n, pd = p.shape
    d = x1.shape[1]
    tb = _token_block(n, row0)
    first = row0 // tb
    src = lambda i: (first + i, 0)
    row = lambda i: (i, 0)
    const = lambda i: (0, 0)
    return pl.pallas_call(
        _final_kernel,
        grid=(n // tb,),
        in_specs=[
            pl.BlockSpec((tb, d), src),
            pl.BlockSpec((tb, d), src),
            pl.BlockSpec((tb, pd), row),
            pl.BlockSpec((1, d), const),
            pl.BlockSpec(wg.shape, const),
            pl.BlockSpec(wp.shape, const),
        ],
        out_specs=pl.BlockSpec((tb, d), row),
        out_shape=jax.ShapeDtypeStruct((n, d), F32),
        compiler_params=_params("parallel"),
        name="ple_epilogue",
    )(x1, pe, p, g, wg, wp)


def _layer(xp, xs, pp, ps, past_k, past_v, lp):
    b, s, d = xp.shape
    bd, l, _ = xs.shape
    n_p, n_s = b * s, bd * l
    kvw = N_KV_HEADS * HEAD_DIM
    sgw = d // 2
    gd = sgw // SGU_GROUPS
    xp2, xs2 = xp.reshape(n_p, d), xs.reshape(n_s, d)

    q, k, v, u, vn, ga, gb = _inproj(
        xp2, xs2, lp['attn_norm_g'][None], lp['w_in'].astype(BF16),
        jnp.tile(lp['q_norm_g'], N_HEADS)[None], jnp.tile(lp['k_norm_g'], N_KV_HEADS)[None],
        lp['sgu_norm_g'][None], lp['sgu_norm_b'][None])

    sgu_w, sgu_b = lp['sgu_w'], lp['sgu_b']
    bias_p = jnp.repeat(sgu_b.T, gd, axis=1)
    a_p, m_p = _prompt_mix(lp['attn_sinks'], q, k, v, vn, u, sgu_w, bias_p, b, s)
    wexp = jnp.repeat(jnp.transpose(sgu_w[:, :l, :l], (2, 1, 0)), gd, axis=2)
    a_s, m_s = _sample_mix(lp['attn_sinks'], q, k, v, past_k.reshape(bd, -1, kvw),
                           past_v.reshape(bd, -1, kvw), vn, u, wexp, bias_p[:l], n_p)

    keys = lp['peer_sub_keys'].reshape(2 * PEER_HEADS, PEER_N_KEYS, -1).astype(BF16)
    x1, xn1, s1, s2 = _merge(xp2, xs2, a_p, a_s, m_p, m_s, ga, gb, lp['w_branch_a'].astype(BF16),
                             lp['w_branch_b'].astype(BF16), lp['w_out'].astype(BF16), lp['ffn_norm_g'][None],
                             lp['peer_w_q'].astype(BF16), keys)
    g2, gm, g1 = _thresholds(s1, s2)
    pe = _peer(xn1, lp['peer_u'].astype(BF16), lp['peer_v'].astype(BF16).T, g2, gm, g1)
    ple = (lp['ple_norm_g'][None], lp['w_ple_gate'].astype(BF16), lp['w_ple'].astype(BF16))
    y_p = _final(x1, pe, pp.reshape(n_p, -1), *ple, 0)
    y_s = _final(x1, pe, ps.reshape(n_s, -1), *ple, n_p)

    wp = min(WINDOW, s)
    tail = lambda t, rows: jnp.stack([t[(bi + 1) * s - rows:(bi + 1) * s] for bi in range(b)])
    heads = lambda t: t.reshape(t.shape[0], t.shape[1], N_KV_HEADS, HEAD_DIM)
    return (y_p.reshape(b, s, d), y_s.reshape(bd, l, d),
            heads(tail(k, wp)), heads(tail(v, wp)),
            heads(k[n_p:].reshape(bd, l, kvw)), heads(v[n_p:].reshape(bd, l, kvw)),
            tail(vn, CHUNK), vn[n_p:].reshape(bd, l, sgw))


def kernel(x_prompt, x_sample, cache_k, cache_v, p_prompt, p_sample, attn_norm_g, w_in, q_norm_g, k_norm_g, attn_sinks, sgu_norm_g, sgu_norm_b, sgu_w, sgu_b, w_branch_a, w_branch_b, w_out, ffn_norm_g, peer_w_q, peer_sub_keys, peer_u, peer_v, ple_norm_g, w_ple, w_ple_gate):
    depth = w_in.shape[0]
    hp, hs = x_prompt, x_sample
    outs = [[] for _ in range(6)]
    for i in range(depth):
        lp = dict(attn_norm_g=attn_norm_g[i], w_in=w_in[i], q_norm_g=q_norm_g[i], k_norm_g=k_norm_g[i],
                  attn_sinks=attn_sinks[i], sgu_norm_g=sgu_norm_g[i], sgu_norm_b=sgu_norm_b[i],
                  sgu_w=sgu_w[i], sgu_b=sgu_b[i], w_branch_a=w_branch_a[i], w_branch_b=w_branch_b[i],
                  w_out=w_out[i], ffn_norm_g=ffn_norm_g[i], peer_w_q=peer_w_q[i],
                  peer_sub_keys=peer_sub_keys[i], peer_u=peer_u[i], peer_v=peer_v[i],
                  ple_norm_g=ple_norm_g[i], w_ple=w_ple[i], w_ple_gate=w_ple_gate[i])
        res = _layer(hp, hs, p_prompt[i], p_sample[i], cache_k[i], cache_v[i], lp)
        hp, hs = res[0], res[1]
        for lst, t in zip(outs, res[2:]):
            lst.append(t)
    return (hp, hs) + tuple(jnp.stack(o) for o in outs)
```

```python
import functools
import math

import jax
import jax.numpy as jnp
from jax import lax
from jax.experimental import pallas as pl
from jax.experimental.pallas import tpu as pltpu

F32 = jnp.float32
BF16 = jnp.bfloat16

N_HEADS = 8
N_KV_HEADS = 2
HEAD_DIM = 64
Q_GROUP = N_HEADS // N_KV_HEADS
WINDOW = 128
CHUNK = 128
SGU_GROUPS = 4
PEER_HEADS = 8
PEER_N_KEYS = 128
PEER_TOPK = 16
EPS = 1e-6
NEG_INF = -1e30
ALIBI_SLOPES = tuple(2.0 ** (-8.0 * h / N_HEADS) for h in range(1, N_HEADS + 1))

LANES = 128
SUBLANES = 8
MXU_COLS = 256
VMEM_LIMIT = 56 * 1024 * 1024


def _params(*semantics):
    return pltpu.CompilerParams(dimension_semantics=semantics, vmem_limit_bytes=VMEM_LIMIT)


def _token_block(*counts):
    for tb in (512, 256, 128):
        if all(n % tb == 0 for n in counts):
            return tb
    raise ValueError(f"token counts {counts} must be multiples of 128")


def _rms(x, g):
    return x * lax.rsqrt(jnp.mean(x * x, axis=-1, keepdims=True) + EPS) * g


def _group_rms(t, ones_blk, g):
    t2 = t * t
    hi = t2.astype(BF16)
    lo = (t2 - hi.astype(F32)).astype(BF16)
    ss = (jnp.dot(hi, ones_blk, preferred_element_type=F32)
          + jnp.dot(lo, ones_blk, preferred_element_type=F32))
    return t * lax.rsqrt(ss * (1.0 / HEAD_DIM) + EPS) * g


def _inproj_kernel(xp_ref, xs_ref, g_ref, w_ref, qg_ref, kg_ref, lg_ref, lb_ref, bq_ref, bk_ref,
                   q_ref, k_ref, v_ref, u_ref, vn_ref, ga_ref, gb_ref, *, prompt_blocks):
    x = jnp.where(pl.program_id(0) < prompt_blocks, xp_ref[...], xs_ref[...])
    xn = _rms(x, g_ref[...])
    z = jnp.dot(xn.astype(BF16), w_ref[...], preferred_element_type=F32)
    att = N_HEADS * HEAD_DIM
    kvw = N_KV_HEADS * HEAD_DIM
    sgw = (z.shape[1] - att - 2 * kvw) // 6
    o = 0
    q = z[:, o:o + att]; o += att
    k = z[:, o:o + kvw]; o += kvw
    v = z[:, o:o + kvw]; o += kvw
    su = z[:, o:o + sgw]; o += sgw
    sv = z[:, o:o + sgw]; o += sgw
    g_a = z[:, o:o + 2 * sgw]; o += 2 * sgw
    g_b = z[:, o:o + 2 * sgw]
    qn = _group_rms(q, bq_ref[...], qg_ref[...])
    q_ref[...] = (qn * (HEAD_DIM ** -0.5)).astype(BF16)
    k_ref[...] = _group_rms(k, bk_ref[...], kg_ref[...])
    v_ref[...] = v
    u_ref[...] = jax.nn.gelu(su).astype(BF16)
    gv = jax.nn.gelu(sv)
    mu = jnp.mean(gv, axis=-1, keepdims=True)
    gc = gv - mu
    vn_ref[...] = gc * lax.rsqrt(jnp.mean(gc * gc, axis=-1, keepdims=True) + EPS) * lg_ref[...] + lb_ref[...]
    ga_ref[...] = jax.nn.sigmoid(g_a).astype(BF16)
    gb_ref[...] = jax.nn.sigmoid(g_b).astype(BF16)


def _inproj(xp, xs, g, w_in, qg, kg, lg, lb):
    (n_p, d), n_s = xp.shape, xs.shape[0]
    n = n_p + n_s
    tb = _token_block(n_p, n_s)
    nbp = n_p // tb
    att = N_HEADS * HEAD_DIM
    kvw = N_KV_HEADS * HEAD_DIM
    sgw = d // 2
    hid = jnp.arange(att) // HEAD_DIM
    bq = (hid[:, None] == hid[None, :]).astype(BF16)
    bk = bq[:kvw, :kvw]
    const = lambda i: (0, 0)
    row = lambda i: (i, 0)
    outs = [
        jax.ShapeDtypeStruct((n, att), BF16),
        jax.ShapeDtypeStruct((n, kvw), F32),
        jax.ShapeDtypeStruct((n, kvw), F32),
        jax.ShapeDtypeStruct((n, sgw), BF16),
        jax.ShapeDtypeStruct((n, sgw), F32),
        jax.ShapeDtypeStruct((n, d), BF16),
        jax.ShapeDtypeStruct((n, d), BF16),
    ]
    return pl.pallas_call(
        functools.partial(_inproj_kernel, prompt_blocks=nbp),
        grid=(n // tb,),
        in_specs=[
            pl.BlockSpec((tb, d), lambda i: (jnp.minimum(i, nbp - 1), 0)),
            pl.BlockSpec((tb, d), lambda i: (jnp.maximum(i - nbp, 0), 0)),
            pl.BlockSpec((1, d), const),
            pl.BlockSpec(w_in.shape, const),
            pl.BlockSpec((1, att), const),
            pl.BlockSpec((1, kvw), const),
            pl.BlockSpec((1, sgw), const),
            pl.BlockSpec((1, sgw), const),
            pl.BlockSpec((att, att), const),
            pl.BlockSpec((kvw, kvw), const),
        ],
        out_specs=[pl.BlockSpec((tb, s.shape[1]), row) for s in outs],
        out_shape=outs,
        compiler_params=_params("parallel"),
        name="inproj",
    )(xp, xs, g, w_in, qg, kg, lg, lb, bq, bk)


def _sink_softmax(s, sink):
    mx = jnp.maximum(jnp.max(s, axis=-1, keepdims=True), sink)
    p = jnp.exp(s - mx)
    den = jnp.sum(p, axis=-1, keepdims=True) + jnp.exp(sink - mx)
    return p / den


def _prompt_kernel(sinks_ref, q_ref, kc_ref, kp_ref, vc_ref, vp_ref, vn_ref, u_ref, w_ref, bias_ref,
                   a_ref, m_ref):
    i = pl.program_id(1)
    tq = q_ref.shape[0]
    nblk = tq // WINDOW
    q = q_ref[...]
    kc = kc_ref[...].astype(BF16)
    vc = vc_ref[...].astype(BF16)
    kp = kp_ref[...].astype(BF16)
    vp = vp_ref[...].astype(BF16)
    row = lax.broadcasted_iota(jnp.int32, (WINDOW, 2 * WINDOW), 0)
    col = lax.broadcasted_iota(jnp.int32, (WINDOW, 2 * WINDOW), 1)
    dist = row - col + WINDOW
    in_window = (dist >= 0) & (dist < WINDOW)
    distf = dist.astype(F32)
    for jq in range(nblk):
        r0 = jq * WINDOW
        if jq == 0:
            kprev, vprev = kp, vp
            valid = in_window & (col >= jnp.where(i > 0, 0, WINDOW))
        else:
            kprev, vprev = kc[r0 - WINDOW:r0], vc[r0 - WINDOW:r0]
            valid = in_window
        kcat = jnp.concatenate([kprev, kc[r0:r0 + WINDOW]], axis=0)
        vcat = jnp.concatenate([vprev, vc[r0:r0 + WINDOW]], axis=0)
        for g in range(N_KV_HEADS):
            heads = range(g * Q_GROUP, (g + 1) * Q_GROUP)
            qg = jnp.concatenate([q[r0:r0 + WINDOW, h * HEAD_DIM:(h + 1) * HEAD_DIM] for h in heads], axis=0)
            s_all = lax.dot_general(qg, kcat[:, g * HEAD_DIM:(g + 1) * HEAD_DIM],
                                    (((1,), (1,)), ((), ())), preferred_element_type=F32)
            probs = []
            for hl, h in enumerate(heads):
                s = s_all[hl * WINDOW:(hl + 1) * WINDOW] - ALIBI_SLOPES[h] * distf
                s = jnp.where(valid, s, NEG_INF)
                probs.append(_sink_softmax(s, sinks_ref[h]).astype(BF16))
            o_all = jnp.dot(jnp.concatenate(probs, axis=0), vcat[:, g * HEAD_DIM:(g + 1) * HEAD_DIM],
                            preferred_element_type=F32)
            for hl, h in enumerate(heads):
                a_ref[r0:r0 + WINDOW, h * HEAD_DIM:(h + 1) * HEAD_DIM] = (
                    o_all[hl * WINDOW:(hl + 1) * WINDOW].astype(BF16))
    tr = lax.broadcasted_iota(jnp.int32, (CHUNK, CHUNK), 0)
    tc = lax.broadcasted_iota(jnp.int32, (CHUNK, CHUNK), 1)
    gd = vn_ref.shape[1] // SGU_GROUPS
    wm = [jnp.where(tr >= tc, w_ref[g], 0.0).astype(BF16) for g in range(SGU_GROUPS)]
    for c in range(tq // CHUNK):
        r0 = c * CHUNK
        vnc = vn_ref[r0:r0 + CHUNK, :].astype(BF16)
        for g in range(SGU_GROUPS):
            s = jnp.dot(wm[g], vnc[:, g * gd:(g + 1) * gd], preferred_element_type=F32)
            s = s + bias_ref[:, g * gd:(g + 1) * gd]
            m_ref[r0:r0 + CHUNK, g * gd:(g + 1) * gd] = (
                u_ref[r0:r0 + CHUNK, g * gd:(g + 1) * gd].astype(F32) * s).astype(BF16)


def _prompt_mix(sinks, q, k, v, vn, u, sgu_w, sgu_bias, b, s):
    n, att = b * s, q.shape[1]
    kvw = k.shape[1]
    sgw = vn.shape[1]
    tq = 512 if s % 512 == 0 else WINDOW
    assert s % tq == 0 and tq % WINDOW == 0 and WINDOW == CHUNK
    r = tq // WINDOW
    nq = s // tq
    cur = lambda bi, i: (bi * nq + i, 0)
    prev = lambda bi, i: (jnp.maximum((bi * nq + i) * r - 1, 0), 0)
    outs = [jax.ShapeDtypeStruct((n, att), BF16), jax.ShapeDtypeStruct((n, sgw), BF16)]
    return pl.pallas_call(
        _prompt_kernel,
        grid=(b, nq),
        in_specs=[
            pl.BlockSpec(memory_space=pltpu.SMEM),
            pl.BlockSpec((tq, att), cur),
            pl.BlockSpec((tq, kvw), cur),
            pl.BlockSpec((WINDOW, kvw), prev),
            pl.BlockSpec((tq, kvw), cur),
            pl.BlockSpec((WINDOW, kvw), prev),
            pl.BlockSpec((tq, sgw), cur),
            pl.BlockSpec((tq, sgw), cur),
            pl.BlockSpec(sgu_w.shape, lambda bi, i: (0, 0, 0)),
            pl.BlockSpec(sgu_bias.shape, lambda bi, i: (0, 0)),
        ],
        out_specs=[pl.BlockSpec((tq, att), cur), pl.BlockSpec((tq, sgw), cur)],
        out_shape=outs,
        compiler_params=_params("parallel", "parallel"),
        name="prompt_mix",
    )(sinks, q, k, k, v, v, vn, u, sgu_w, sgu_bias)


def _sample_kernel(sinks_ref, q_ref, kn_ref, vn_new_ref, ck_ref, cv_ref, vn_ref, u_ref, wexp_ref, bias_ref,
                   a_ref, m_ref):
    bb, w, _ = ck_ref.shape
    l = q_ref.shape[0] // bb
    per_seq = lambda ref: ref[...].astype(F32).reshape(bb, l, ref.shape[1])
    q = per_seq(q_ref)
    kcat = jnp.concatenate([ck_ref[...], per_seq(kn_ref)], axis=1).astype(BF16)
    vcat = jnp.concatenate([cv_ref[...], per_seq(vn_new_ref)], axis=1).astype(BF16)
    rows = Q_GROUP * l
    t = lax.broadcasted_iota(jnp.int32, (rows, w + l), 0) % l
    key = lax.broadcasted_iota(jnp.int32, (rows, w + l), 1)
    dist = t - (key - w)
    valid = (dist >= 0) & (dist < WINDOW)
    distf = dist.astype(F32)
    hl_of_row = lax.broadcasted_iota(jnp.int32, (rows, 1), 0) // l
    for g in range(N_KV_HEADS):
        heads = range(g * Q_GROUP, (g + 1) * Q_GROUP)
        qg = jnp.concatenate([q[:, :, h * HEAD_DIM:(h + 1) * HEAD_DIM] for h in heads], axis=1)
        s = jnp.einsum('bqd,bkd->bqk', qg.astype(BF16), kcat[:, :, g * HEAD_DIM:(g + 1) * HEAD_DIM],
                       preferred_element_type=F32)
        slope = jnp.zeros((rows, 1), F32)
        sink = jnp.zeros((rows, 1), F32)
        for hl, h in enumerate(heads):
            slope = jnp.where(hl_of_row == hl, ALIBI_SLOPES[h], slope)
            sink = jnp.where(hl_of_row == hl, sinks_ref[h], sink)
        s = jnp.where(valid[None], s - (slope * distf)[None], NEG_INF)
        p = _sink_softmax(s, sink[None]).astype(BF16)
        o = jnp.einsum('bqk,bkd->bqd', p, vcat[:, :, g * HEAD_DIM:(g + 1) * HEAD_DIM],
                       preferred_element_type=F32)
        for hl, h in enumerate(heads):
            a_ref[:, h * HEAD_DIM:(h + 1) * HEAD_DIM] = (
                o[:, hl * l:(hl + 1) * l, :].reshape(bb * l, HEAD_DIM).astype(BF16))
    vn = per_seq(vn_ref)
    tt = lax.broadcasted_iota(jnp.int32, (l, vn.shape[2]), 0)
    s = jnp.broadcast_to(bias_ref[...][None], vn.shape)
    for sp in range(l):
        wm = jnp.where(tt >= sp, wexp_ref[sp], 0.0)
        s = s + wm[None] * vn[:, sp:sp + 1, :]
    m_ref[...] = (per_seq(u_ref) * s).reshape(bb * l, vn.shape[2]).astype(BF16)


def _sample_mix(sinks, q, k, v, cache_k, cache_v, vn, u, wexp, bias, n_p):
    b, w, kvw = cache_k.shape
    n, att = q.shape
    l = (n - n_p) // b
    sgw = vn.shape[1]
    bb = 16 if b % 16 == 0 else b
    assert n_p % (bb * l) == 0
    first = n_p // (bb * l)
    tok = lambda width: pl.BlockSpec((bb * l, width), lambda i: (first + i, 0))
    own = lambda width: pl.BlockSpec((bb * l, width), lambda i: (i, 0))
    past = pl.BlockSpec((bb, w, kvw), lambda i: (i, 0, 0))
    outs = [jax.ShapeDtypeStruct((n - n_p, att), BF16), jax.ShapeDtypeStruct((n - n_p, sgw), BF16)]
    return pl.pallas_call(
        _sample_kernel,
        grid=(b // bb,),
        in_specs=[
            pl.BlockSpec(memory_space=pltpu.SMEM),
            tok(att), tok(kvw), tok(kvw), past, past, tok(sgw), tok(sgw),
            pl.BlockSpec(wexp.shape, lambda i: (0, 0, 0)),
            pl.BlockSpec(bias.shape, lambda i: (0, 0)),
        ],
        out_specs=[own(att), own(sgw)],
        out_shape=outs,
        compiler_params=_params("parallel"),
        name="sample_mix",
    )(sinks, q, k, v, cache_k, cache_v, vn, u, wexp, bias)


def _merge_kernel(x_ref, a_ref, m_ref, ga_ref, gb_ref, wa_ref, wb_ref, wo_ref, fg_ref, wq_ref, keys_ref,
                  x1_ref, xnt_ref, s1_ref, s2_ref):
    ha = jnp.dot(a_ref[...], wa_ref[...], preferred_element_type=F32)
    hb = jnp.dot(m_ref[...], wb_ref[...], preferred_element_type=F32)
    h = ga_ref[...].astype(F32) * ha + gb_ref[...].astype(F32) * hb
    x1 = x_ref[...] + jnp.dot(h.astype(BF16), wo_ref[...], preferred_element_type=F32)
    x1_ref[...] = x1
    xn32 = _rms(x1, fg_ref[...])
    xnt_ref[...] = xn32.T.astype(BF16)
    xn = xn32.astype(BF16)
    qp = jnp.dot(xn, wq_ref[...], preferred_element_type=F32).astype(BF16)
    half = keys_ref.shape[2]
    nsub = s1_ref.shape[0]
    for hc in range(keys_ref.shape[0]):
        st = lax.dot_general(keys_ref[hc], qp[:, hc * half:(hc + 1) * half],
                             (((1,), (1,)), ((), ())), preferred_element_type=F32)
        dst = s1_ref if hc % 2 == 0 else s2_ref
        for tl in range(nsub):
            dst[tl, hc // 2] = st[:, tl * LANES:(tl + 1) * LANES]


def _merge(xp, xs, ap, a_s, mp, ms, ga, gb, wa, wb, wo, fg, wq, keys):
    (n_p, d), n_s = xp.shape, xs.shape[0]
    n = n_p + n_s
    tb = _token_block(n_p, n_s)
    nbp = n_p // tb
    nsub = tb // LANES
    hc, nk, half = keys.shape
    row = lambda i: (i, 0)
    const2 = lambda i: (0, 0)
    heads = hc // 2
    outs = [
        jax.ShapeDtypeStruct((n, d), F32),
        jax.ShapeDtypeStruct((d, n), BF16),
        jax.ShapeDtypeStruct((n // LANES, heads, nk, LANES), F32),
        jax.ShapeDtypeStruct((n // LANES, heads, nk, LANES), F32),
    ]

    def body(xp_ref, xs_ref, ap_ref, as_ref, mp_ref, ms_ref, *rest):
        side = lambda p_ref, s_ref: _Value(jnp.where(pl.program_id(0) < nbp, p_ref[...], s_ref[...]))
        _merge_kernel(side(xp_ref, xs_ref), side(ap_ref, as_ref), side(mp_ref, ms_ref), *rest)

    prompt = lambda width: pl.BlockSpec((tb, width), lambda i: (jnp.minimum(i, nbp - 1), 0))
    sample = lambda width: pl.BlockSpec((tb, width), lambda i: (jnp.maximum(i - nbp, 0), 0))
    return pl.pallas_call(
        body,
        grid=(n // tb,),
        in_specs=[
            prompt(d), sample(d),
            prompt(ap.shape[1]), sample(ap.shape[1]),
            prompt(mp.shape[1]), sample(mp.shape[1]),
            pl.BlockSpec((tb, d), row),
            pl.BlockSpec((tb, d), row),
            pl.BlockSpec(wa.shape, const2),
            pl.BlockSpec(wb.shape, const2),
            pl.BlockSpec(wo.shape, const2),
            pl.BlockSpec((1, d), const2),
            pl.BlockSpec(wq.shape, const2),
            pl.BlockSpec(keys.shape, lambda i: (0, 0, 0)),
        ],
        out_specs=[
            pl.BlockSpec((tb, d), row),
            pl.BlockSpec((d, tb), lambda i: (0, i)),
            pl.BlockSpec((nsub, heads, nk, LANES), lambda i: (i, 0, 0, 0)),
            pl.BlockSpec((nsub, heads, nk, LANES), lambda i: (i, 0, 0, 0)),
        ],
        out_shape=outs,
        compiler_params=_params("parallel"),
        name="merge",
    )(xp, xs, ap, a_s, mp, ms, ga, gb, wa, wb, wo, fg, wq, keys)


class _Value:
    def __init__(self, value):
        self._value = value

    def __getitem__(self, idx):
        return self._value[idx]


def _oddeven_merge(lo, hi, r):
    step = r * 2
    if step < hi - lo:
        yield from _oddeven_merge(lo, hi, step)
        yield from _oddeven_merge(lo + r, hi, step)
        yield from [(i, i + r) for i in range(lo + r, hi - r, step)]
    else:
        yield (lo, lo + r)


def _oddeven_merge_sort(lo, hi):
    if hi - lo >= 1:
        mid = lo + (hi - lo) // 2
        yield from _oddeven_merge_sort(lo, mid)
        yield from _oddeven_merge_sort(mid + 1, hi)
        yield from _oddeven_merge(lo, hi, 1)


_SORT_TOPK = tuple(_oddeven_merge_sort(0, PEER_TOPK - 1))


def _cmpx(w, i, j):
    a, b = w[i], w[j]
    if b is None:
        return
    if a is None:
        w[i], w[j] = b, None
        return
    w[i], w[j] = jnp.maximum(a, b), jnp.minimum(a, b)


def _top_values(w):
    k = PEER_TOPK
    w = list(w)
    for i, j in _SORT_TOPK:
        _cmpx(w, i, j)
    shift = SUBLANES // 2
    while shift >= 1:
        y = [None if v is None else pltpu.roll(v, shift, 0) for v in w]
        z = []
        for r in range(k):
            a, b = w[r], y[k - 1 - r]
            z.append(b if a is None else a if b is None else jnp.maximum(a, b))
        stride = k // 2
        while stride >= 1:
            for i in range(k):
                if i & stride == 0:
                    _cmpx(z, i, i + stride)
            stride //= 2
        w = z
        shift //= 2
    return w


def _thresh_kernel(s1_ref, s2_ref, g2_ref, gm_ref, g1_ref):
    k = PEER_TOPK
    nk, lanes = s1_ref.shape[2], s1_ref.shape[3]
    nslot = nk // SUBLANES
    assert nslot == k and k == 2 * SUBLANES
    sub = lax.broadcasted_iota(jnp.int32, (SUBLANES, lanes), 0)

    def pack(vals):
        out = vals[0]
        for j in range(1, SUBLANES):
            out = jnp.where(sub == j, vals[j], out)
        return out

    def head(h, carry):
        w1 = [s1_ref[0, h, r * SUBLANES:(r + 1) * SUBLANES, :] for r in range(nslot)]
        w2 = [s2_ref[0, h, r * SUBLANES:(r + 1) * SUBLANES, :] for r in range(nslot)]
        a = _top_values(w1)
        b = _top_values(w2)
        b_lo, b_hi, a_hi = pack(b[:SUBLANES]), pack(b[SUBLANES:]), pack(a[SUBLANES:])
        cands = ([a[0] + b_lo, a[0] + b_hi] + [a[i] + b_lo for i in range(1, SUBLANES)] + [a_hi + b[0]])
        best = _top_values(cands + [None] * (k - len(cands)))
        tau = best[k - 1]
        z = jnp.ones_like(tau)
        for r in range(1, k):
            z = z + jnp.exp(best[r] - best[0])
        inv_z = 1.0 / z
        eb = [jnp.exp(b[j] - b[0]) for j in range(k)]
        gamma = []
        for i in range(k):
            t = jnp.full_like(tau, jnp.inf)
            for j in range(k // (i + 1)):
                t = jnp.where(a[i] + b[j] >= tau, eb[j], t)
            gamma.append(t)
        for r in range(nslot):
            gm = jnp.full_like(tau, jnp.inf)
            for i in range(k - 1, -1, -1):
                gm = jnp.where(w1[r] >= a[i], gamma[i], gm)
            rows = pl.ds(r * SUBLANES, SUBLANES)
            gm_ref[0, h, rows, :] = gm
            g1_ref[0, h, rows, :] = jnp.exp(w1[r] - a[0]) * (0.5 * inv_z)
            g2_ref[0, h, rows, :] = jnp.exp(w2[r] - b[0])
        return carry

    lax.fori_loop(0, s1_ref.shape[1], head, 0)


def _thresholds(s1, s2):
    nsub, heads, nk, lanes = s1.shape
    spec = pl.BlockSpec((1, heads, nk, lanes), lambda i: (i, 0, 0, 0))
    out = jax.ShapeDtypeStruct(s1.shape, F32)
    return pl.pallas_call(
        _thresh_kernel,
        grid=(nsub,),
        in_specs=[spec, spec],
        out_specs=[spec, spec, spec],
        out_shape=[out, out, out],
        compiler_params=_params("parallel"),
        name="peer_thresholds",
    )(s1, s2)


GATE_ROWS = 128
GELU_C0 = math.sqrt(2.0 / math.pi)
GELU_C1 = 0.044715 * GELU_C0
PEER_CHUNK = 512
PEER_TOKENS = 1024


def _gate_columns(at_ref, wt_ref, j0, tls, g2_ref, gm_ref, g1_ref):
    _, heads, nk, lanes = g2_ref.shape

    def tile(j, tl, r0):
        gate = jnp.zeros((GATE_ROWS, lanes), BF16)
        for h in range(heads):
            gm = gm_ref[tl, h, j0 + j:j0 + j + 1, :]
            g1 = jnp.broadcast_to(g1_ref[tl, h, j0 + j:j0 + j + 1, :], (GATE_ROWS, lanes)).astype(BF16)
            g2 = g2_ref[tl, h, r0:r0 + GATE_ROWS, :]
            gate = gate + g1 * jnp.where(g2 >= gm, g2, 0.0).astype(BF16)
        rows = slice(j * nk + r0, j * nk + r0 + GATE_ROWS)
        cols = slice(tl * lanes, (tl + 1) * lanes)
        x = at_ref[rows, cols].astype(BF16)
        t = jnp.tanh(x * (GELU_C0 + GELU_C1 * (x * x)))
        wt_ref[rows, cols] = (x + x * t) * gate

    for tl in tls:
        for j in range(at_ref.shape[0] // nk):
            for r0 in range(0, nk, GATE_ROWS):
                tile(j, tl, r0)


def _peer_kernel(xnt_ref, u0a_ref, u0b_ref, una_ref, unb_ref, va_ref, vb_ref, g2_ref, gm_ref, g1_ref, out_ref,
                 at_a, at_b, wt_a, wt_b, acc, xnt):
    s = pl.program_id(1)
    per = at_a.shape[0] // g2_ref.shape[2]
    tb = xnt_ref.shape[1]
    lanes = g2_ref.shape[3]
    tables = (g2_ref, gm_ref, g1_ref)
    cw = min(MXU_COLS, tb)
    columns = [(slice(c, c + cw), range(c // lanes, (c + cw) // lanes)) for c in range(0, tb, cw)]

    def act(u_ref, at, cs):
        at[:, cs] = jnp.dot(u_ref[...], xnt[:, cs], preferred_element_type=F32)

    def mix(v_ref, wt, cs):
        acc[:, cs] += jnp.dot(v_ref[...], wt[:, cs], preferred_element_type=F32)

    @pl.when(s == 0)
    def _():
        xnt[...] = xnt_ref[...]
        for cs, _ in columns:
            act(u0a_ref, at_a, cs)
            act(u0b_ref, at_b, cs)
        acc[...] = jnp.zeros_like(acc)

    for at, wt, j0, v_ref, un_ref in ((at_a, wt_a, 0, va_ref, una_ref), (at_b, wt_b, per, vb_ref, unb_ref)):
        for cs, tls in columns:
            _gate_columns(at, wt, j0, tls, *tables)
            mix(v_ref, wt, cs)
            act(un_ref, at, cs)

    @pl.when(s == pl.num_programs(1) - 1)
    def _():
        out_ref[...] = acc[...].T


def _peer(xnt, u, vt, g2, gm, g1):
    d, n = xnt.shape
    ne = u.shape[0]
    tb = PEER_TOKENS if n % PEER_TOKENS == 0 else _token_block(n)
    nsub = tb // LANES
    _, heads, nk, _ = g2.shape
    ec = PEER_CHUNK
    assert ne % (2 * ec) == 0 and ec % nk == 0 and ne == nk * nk and nk % GATE_ROWS == 0
    nc = ne // ec
    per = ec // nk
    keyed2 = pl.BlockSpec((nsub, heads, nk, LANES), lambda i, s: (i, 0, 0, 0))
    keyed1 = pl.BlockSpec((nsub, heads, 2 * per, LANES), lambda i, s: (i, 0, s, 0))
    return pl.pallas_call(
        _peer_kernel,
        grid=(n // tb, nc // 2),
        in_specs=[
            pl.BlockSpec((d, tb), lambda i, s: (0, i)),
            pl.BlockSpec((ec, d), lambda i, s: (0, 0)),
            pl.BlockSpec((ec, d), lambda i, s: (1, 0)),
            pl.BlockSpec((ec, d), lambda i, s: (jnp.minimum(2 * s + 2, nc - 2), 0)),
            pl.BlockSpec((ec, d), lambda i, s: (jnp.minimum(2 * s + 3, nc - 1), 0)),
            pl.BlockSpec((d, ec), lambda i, s: (0, 2 * s)),
            pl.BlockSpec((d, ec), lambda i, s: (0, 2 * s + 1)),
            keyed2, keyed1, keyed1,
        ],
        out_specs=pl.BlockSpec((tb, d), lambda i, s: (i, 0)),
        out_shape=jax.ShapeDtypeStruct((n, d), F32),
        scratch_shapes=[
            pltpu.VMEM((ec, tb), F32),
            pltpu.VMEM((ec, tb), F32),
            pltpu.VMEM((ec, tb), BF16),
            pltpu.VMEM((ec, tb), BF16),
            pltpu.VMEM((d, tb), F32),
            pltpu.VMEM((d, tb), BF16),
        ],
        compiler_params=_params("parallel", "arbitrary"),
        name="peer_mix",
    )(xnt, u, u, u, u, vt, vt, g2, gm, g1)


def _final_kernel(x1_ref, pe_ref, p_ref, g_ref, wg_ref, wp_ref, y_ref):
    x2 = x1_ref[...] + pe_ref[...]
    xn = _rms(x2, g_ref[...]).astype(BF16)
    gate = jax.nn.sigmoid(jnp.dot(xn, wg_ref[...], preferred_element_type=F32))
    y_ref[...] = x2 + gate * jnp.dot(p_ref[...].astype(BF16), wp_ref[...], preferred_element_type=F32)


def _final(x1, pe, p, g, wg, wp, row0):
    n, pd = p.shape
    d = x1.shape[1]
    tb = _token_block(n, row0)
    first = row0 // tb
    src = lambda i: (first + i, 0)
    row = lambda i: (i, 0)
    const = lambda i: (0, 0)
    return pl.pallas_call(
        _final_kernel,
        grid=(n // tb,),
        in_specs=[
            pl.BlockSpec((tb, d), src),
            pl.BlockSpec((tb, d), src),
            pl.BlockSpec((tb, pd), row),
            pl.BlockSpec((1, d), const),
            pl.BlockSpec(wg.shape, const),
            pl.BlockSpec(wp.shape, const),
        ],
        out_specs=pl.BlockSpec((tb, d), row),
        out_shape=jax.ShapeDtypeStruct((n, d), F32),
        compiler_params=_params("parallel"),
        name="ple_epilogue",
    )(x1, pe, p, g, wg, wp)


def _layer(xp, xs, pp, ps, past_k, past_v, lp):
    b, s, d = xp.shape
    bd, l, _ = xs.shape
    n_p, n_s = b * s, bd * l
    kvw = N_KV_HEADS * HEAD_DIM
    sgw = d // 2
    gd = sgw // SGU_GROUPS
    xp2, xs2 = xp.reshape(n_p, d), xs.reshape(n_s, d)

    q, k, v, u, vn, ga, gb = _inproj(
        xp2, xs2, lp['attn_norm_g'][None], lp['w_in'].astype(BF16),
        jnp.tile(lp['q_norm_g'], N_HEADS)[None], jnp.tile(lp['k_norm_g'], N_KV_HEADS)[None],
        lp['sgu_norm_g'][None], lp['sgu_norm_b'][None])

    sgu_w, sgu_b = lp['sgu_w'], lp['sgu_b']
    bias_p = jnp.repeat(sgu_b.T, gd, axis=1)
    a_p, m_p = _prompt_mix(lp['attn_sinks'], q, k, v, vn, u, sgu_w, bias_p, b, s)
    wexp = jnp.repeat(jnp.transpose(sgu_w[:, :l, :l], (2, 1, 0)), gd, axis=2)
    a_s, m_s = _sample_mix(lp['attn_sinks'], q, k, v, past_k.reshape(bd, -1, kvw),
                           past_v.reshape(bd, -1, kvw), vn, u, wexp, bias_p[:l], n_p)

    keys = lp['peer_sub_keys'].reshape(2 * PEER_HEADS, PEER_N_KEYS, -1).astype(BF16)
    x1, xn1, s1, s2 = _merge(xp2, xs2, a_p, a_s, m_p, m_s, ga, gb, lp['w_branch_a'].astype(BF16),
                             lp['w_branch_b'].astype(BF16), lp['w_out'].astype(BF16), lp['ffn_norm_g'][None],
                             lp['peer_w_q'].astype(BF16), keys)
    g2, gm, g1 = _thresholds(s1, s2)
    pe = _peer(xn1, lp['peer_u'].astype(BF16), lp['peer_v'].astype(BF16).T, g2, gm, g1)
    ple = (lp['ple_norm_g'][None], lp['w_ple_gate'].astype(BF16), lp['w_ple'].astype(BF16))
    y_p = _final(x1, pe, pp.reshape(n_p, -1), *ple, 0)
    y_s = _final(x1, pe, ps.reshape(n_s, -1), *ple, n_p)

    wp = min(WINDOW, s)
    tail = lambda t, rows: jnp.stack([t[(bi + 1) * s - rows:(bi + 1) * s] for bi in range(b)])
    heads = lambda t: t.reshape(t.shape[0], t.shape[1], N_KV_HEADS, HEAD_DIM)
    return (y_p.reshape(b, s, d), y_s.reshape(bd, l, d),
            heads(tail(k, wp)), heads(tail(v, wp)),
            heads(k[n_p:].reshape(bd, l, kvw)), heads(v[n_p:].reshape(bd, l, kvw)),
            tail(vn, CHUNK), vn[n_p:].reshape(bd, l, sgw))


def kernel(x_prompt, x_sample, cache_k, cache_v, p_prompt, p_sample, attn_norm_g, w_in, q_norm_g, k_norm_g, attn_sinks, sgu_norm_g, sgu_norm_b, sgu_w, sgu_b, w_branch_a, w_branch_b, w_out, ffn_norm_g, peer_w_q, peer_sub_keys, peer_u, peer_v, ple_norm_g, w_ple, w_ple_gate):
    depth = w_in.shape[0]
    hp, hs = x_prompt, x_sample
    outs = [[] for _ in range(6)]
    for i in range(depth):
        lp = dict(attn_norm_g=attn_norm_g[i], w_in=w_in[i], q_norm_g=q_norm_g[i], k_norm_g=k_norm_g[i],
                  attn_sinks=attn_sinks[i], sgu_norm_g=sgu_norm_g[i], sgu_norm_b=sgu_norm_b[i],
                  sgu_w=sgu_w[i], sgu_b=sgu_b[i], w_branch_a=w_branch_a[i], w_branch_b=w_branch_b[i],
                  w_out=w_out[i], ffn_norm_g=ffn_norm_g[i], peer_w_q=peer_w_q[i],
                  peer_sub_keys=peer_sub_keys[i], peer_u=peer_u[i], peer_v=peer_v[i],
                  ple_norm_g=ple_norm_g[i], w_ple=w_ple[i], w_ple_gate=w_ple_gate[i])
        res = _layer(hp, hs, p_prompt[i], p_sample[i], cache_k[i], cache_v[i], lp)
        hp, hs = res[0], res[1]
        for lst, t in zip(outs, res[2:]):
            lst.append(t)
    return (hp, hs) + tuple(jnp.stack(o) for o in outs)
```

```python
import functools
import math

import jax
import jax.numpy as jnp
from jax import lax
from jax.experimental import pallas as pl
from jax.experimental.pallas import tpu as pltpu

F32 = jnp.float32
BF16 = jnp.bfloat16

N_HEADS = 8
N_KV_HEADS = 2
HEAD_DIM = 64
Q_GROUP = N_HEADS // N_KV_HEADS
WINDOW = 128
CHUNK = 128
SGU_GROUPS = 4
PEER_HEADS = 8
PEER_N_KEYS = 128
PEER_TOPK = 16
EPS = 1e-6
NEG_INF = -1e30
ALIBI_SLOPES = tuple(2.0 ** (-8.0 * h / N_HEADS) for h in range(1, N_HEADS + 1))

LANES = 128
SUBLANES = 8
MXU_COLS = 256
VMEM_LIMIT = 56 * 1024 * 1024


def _params(*semantics):
    return pltpu.CompilerParams(dimension_semantics=semantics, vmem_limit_bytes=VMEM_LIMIT)


def _token_block(*counts):
    for tb in (512, 256, 128):
        if all(n % tb == 0 for n in counts):
            return tb
    raise ValueError(f"token counts {counts} must be multiples of 128")


def _rms(x, g):
    return x * lax.rsqrt(jnp.mean(x * x, axis=-1, keepdims=True) + EPS) * g


def _group_rms(t, ones_blk, g):
    t2 = t * t
    hi = t2.astype(BF16)
    lo = (t2 - hi.astype(F32)).astype(BF16)
    ss = (jnp.dot(hi, ones_blk, preferred_element_type=F32)
          + jnp.dot(lo, ones_blk, preferred_element_type=F32))
    return t * lax.rsqrt(ss * (1.0 / HEAD_DIM) + EPS) * g


def _inproj_kernel(xp_ref, xs_ref, g_ref, w_ref, qg_ref, kg_ref, lg_ref, lb_ref, bq_ref, bk_ref,
                   q_ref, k_ref, v_ref, u_ref, vn_ref, ga_ref, gb_ref, *, prompt_blocks):
    x = jnp.where(pl.program_id(0) < prompt_blocks, xp_ref[...], xs_ref[...])
    xn = _rms(x, g_ref[...])
    z = jnp.dot(xn.astype(BF16), w_ref[...], preferred_element_type=F32)
    att = N_HEADS * HEAD_DIM
    kvw = N_KV_HEADS * HEAD_DIM
    sgw = (z.shape[1] - att - 2 * kvw) // 6
    o = 0
    q = z[:, o:o + att]; o += att
    k = z[:, o:o + kvw]; o += kvw
    v = z[:, o:o + kvw]; o += kvw
    su = z[:, o:o + sgw]; o += sgw
    sv = z[:, o:o + sgw]; o += sgw
    g_a = z[:, o:o + 2 * sgw]; o += 2 * sgw
    g_b = z[:, o:o + 2 * sgw]
    qn = _group_rms(q, bq_ref[...], qg_ref[...])
    q_ref[...] = (qn * (HEAD_DIM ** -0.5)).astype(BF16)
    k_ref[...] = _group_rms(k, bk_ref[...], kg_ref[...])
    v_ref[...] = v
    u_ref[...] = jax.nn.gelu(su).astype(BF16)
    gv = jax.nn.gelu(sv)
    mu = jnp.mean(gv, axis=-1, keepdims=True)
    gc = gv - mu
    vn_ref[...] = gc * lax.rsqrt(jnp.mean(gc * gc, axis=-1, keepdims=True) + EPS) * lg_ref[...] + lb_ref[...]
    ga_ref[...] = jax.nn.sigmoid(g_a).astype(BF16)
    gb_ref[...] = jax.nn.sigmoid(g_b).astype(BF16)


def _inproj(xp, xs, g, w_in, qg, kg, lg, lb):
    (n_p, d), n_s = xp.shape, xs.shape[0]
    n = n_p + n_s
    tb = _token_block(n_p, n_s)
    nbp = n_p // tb
    att = N_HEADS * HEAD_DIM
    kvw = N_KV_HEADS * HEAD_DIM
    sgw = d // 2
    hid = jnp.arange(att) // HEAD_DIM
    bq = (hid[:, None] == hid[None, :]).astype(BF16)
    bk = bq[:kvw, :kvw]
    const = lambda i: (0, 0)
    row = lambda i: (i, 0)
    outs = [
        jax.ShapeDtypeStruct((n, att), BF16),
        jax.ShapeDtypeStruct((n, kvw), F32),
        jax.ShapeDtypeStruct((n, kvw), F32),
        jax.ShapeDtypeStruct((n, sgw), BF16),
        jax.ShapeDtypeStruct((n, sgw), F32),
        jax.ShapeDtypeStruct((n, d), BF16),
        jax.ShapeDtypeStruct((n, d), BF16),
    ]
    return pl.pallas_call(
        functools.partial(_inproj_kernel, prompt_blocks=nbp),
        grid=(n // tb,),
        in_specs=[
            pl.BlockSpec((tb, d), lambda i: (jnp.minimum(i, nbp - 1), 0)),
            pl.BlockSpec((tb, d), lambda i: (jnp.maximum(i - nbp, 0), 0)),
            pl.BlockSpec((1, d), const),
            pl.BlockSpec(w_in.shape, const),
            pl.BlockSpec((1, att), const),
            pl.BlockSpec((1, kvw), const),
            pl.BlockSpec((1, sgw), const),
            pl.BlockSpec((1, sgw), const),
            pl.BlockSpec((att, att), const),
            pl.BlockSpec((kvw, kvw), const),
        ],
        out_specs=[pl.BlockSpec((tb, s.shape[1]), row) for s in outs],
        out_shape=outs,
        compiler_params=_params("parallel"),
        name="inproj",
    )(xp, xs, g, w_in, qg, kg, lg, lb, bq, bk)


def _sink_softmax(s, sink):
    mx = jnp.maximum(jnp.max(s, axis=-1, keepdims=True), sink)
    p = jnp.exp(s - mx)
    den = jnp.sum(p, axis=-1, keepdims=True) + jnp.exp(sink - mx)
    return p / den


def _prompt_kernel(sinks_ref, q_ref, kc_ref, kp_ref, vc_ref, vp_ref, vn_ref, u_ref, w_ref, bias_ref,
                   a_ref, m_ref):
    i = pl.program_id(1)
    tq = q_ref.shape[0]
    nblk = tq // WINDOW
    q = q_ref[...]
    kc = kc_ref[...].astype(BF16)
    vc = vc_ref[...].astype(BF16)
    kp = kp_ref[...].astype(BF16)
    vp = vp_ref[...].astype(BF16)
    row = lax.broadcasted_iota(jnp.int32, (WINDOW, 2 * WINDOW), 0)
    col = lax.broadcasted_iota(jnp.int32, (WINDOW, 2 * WINDOW), 1)
    dist = row - col + WINDOW
    in_window = (dist >= 0) & (dist < WINDOW)
    distf = dist.astype(F32)
    for jq in range(nblk):
        r0 = jq * WINDOW
        if jq == 0:
            kprev, vprev = kp, vp
            valid = in_window & (col >= jnp.where(i > 0, 0, WINDOW))
        else:
            kprev, vprev = kc[r0 - WINDOW:r0], vc[r0 - WINDOW:r0]
            valid = in_window
        kcat = jnp.concatenate([kprev, kc[r0:r0 + WINDOW]], axis=0)
        vcat = jnp.concatenate([vprev, vc[r0:r0 + WINDOW]], axis=0)
        for g in range(N_KV_HEADS):
            heads = range(g * Q_GROUP, (g + 1) * Q_GROUP)
            qg = jnp.concatenate([q[r0:r0 + WINDOW, h * HEAD_DIM:(h + 1) * HEAD_DIM] for h in heads], axis=0)
            s_all = lax.dot_general(qg, kcat[:, g * HEAD_DIM:(g + 1) * HEAD_DIM],
                                    (((1,), (1,)), ((), ())), preferred_element_type=F32)
            probs = []
            for hl, h in enumerate(heads):
                s = s_all[hl * WINDOW:(hl + 1) * WINDOW] - ALIBI_SLOPES[h] * distf
                s = jnp.where(valid, s, NEG_INF)
                probs.append(_sink_softmax(s, sinks_ref[h]).astype(BF16))
            o_all = jnp.dot(jnp.concatenate(probs, axis=0), vcat[:, g * HEAD_DIM:(g + 1) * HEAD_DIM],
                            preferred_element_type=F32)
            for hl, h in enumerate(heads):
                a_ref[r0:r0 + WINDOW, h * HEAD_DIM:(h + 1) * HEAD_DIM] = (
                    o_all[hl * WINDOW:(hl + 1) * WINDOW].astype(BF16))
    tr = lax.broadcasted_iota(jnp.int32, (CHUNK, CHUNK), 0)
    tc = lax.broadcasted_iota(jnp.int32, (CHUNK, CHUNK), 1)
    gd = vn_ref.shape[1] // SGU_GROUPS
    wm = [jnp.where(tr >= tc, w_ref[g], 0.0).astype(BF16) for g in range(SGU_GROUPS)]
    for c in range(tq // CHUNK):
        r0 = c * CHUNK
        vnc = vn_ref[r0:r0 + CHUNK, :].astype(BF16)
        for g in range(SGU_GROUPS):
            s = jnp.dot(wm[g], vnc[:, g * gd:(g + 1) * gd], preferred_element_type=F32)
            s = s + bias_ref[:, g * gd:(g + 1) * gd]
            m_ref[r0:r0 + CHUNK, g * gd:(g + 1) * gd] = (
                u_ref[r0:r0 + CHUNK, g * gd:(g + 1) * gd].astype(F32) * s).astype(BF16)


def _prompt_mix(sinks, q, k, v, vn, u, sgu_w, sgu_bias, b, s):
    n, att = b * s, q.shape[1]
    kvw = k.shape[1]
    sgw = vn.shape[1]
    tq = 512 if s % 512 == 0 else WINDOW
    assert s % tq == 0 and tq % WINDOW == 0 and WINDOW == CHUNK
    r = tq // WINDOW
    nq = s // tq
    cur = lambda bi, i: (bi * nq + i, 0)
    prev = lambda bi, i: (jnp.maximum((bi * nq + i) * r - 1, 0), 0)
    outs = [jax.ShapeDtypeStruct((n, att), BF16), jax.ShapeDtypeStruct((n, sgw), BF16)]
    return pl.pallas_call(
        _prompt_kernel,
        grid=(b, nq),
        in_specs=[
            pl.BlockSpec(memory_space=pltpu.SMEM),
            pl.BlockSpec((tq, att), cur),
            pl.BlockSpec((tq, kvw), cur),
            pl.BlockSpec((WINDOW, kvw), prev),
            pl.BlockSpec((tq, kvw), cur),
            pl.BlockSpec((WINDOW, kvw), prev),
            pl.BlockSpec((tq, sgw), cur),
            pl.BlockSpec((tq, sgw), cur),
            pl.BlockSpec(sgu_w.shape, lambda bi, i: (0, 0, 0)),
            pl.BlockSpec(sgu_bias.shape, lambda bi, i: (0, 0)),
        ],
        out_specs=[pl.BlockSpec((tq, att), cur), pl.BlockSpec((tq, sgw), cur)],
        out_shape=outs,
        compiler_params=_params("parallel", "parallel"),
        name="prompt_mix",
    )(sinks, q, k, k, v, v, vn, u, sgu_w, sgu_bias)


def _sample_kernel(sinks_ref, q_ref, kn_ref, vn_new_ref, ck_ref, cv_ref, vn_ref, u_ref, wexp_ref, bias_ref,
                   a_ref, m_ref):
    bb, w, _ = ck_ref.shape
    l = q_ref.shape[0] // bb
    per_seq = lambda ref: ref[...].astype(F32).reshape(bb, l, ref.shape[1])
    q = per_seq(q_ref)
    kcat = jnp.concatenate([ck_ref[...], per_seq(kn_ref)], axis=1).astype(BF16)
    vcat = jnp.concatenate([cv_ref[...], per_seq(vn_new_ref)], axis=1).astype(BF16)
    rows = Q_GROUP * l
    t = lax.broadcasted_iota(jnp.int32, (rows, w + l), 0) % l
    key = lax.broadcasted_iota(jnp.int32, (rows, w + l), 1)
    dist = t - (key - w)
    valid = (dist >= 0) & (dist < WINDOW)
    distf = dist.astype(F32)
    hl_of_row = lax.broadcasted_iota(jnp.int32, (rows, 1), 0) // l
    for g in range(N_KV_HEADS):
        heads = range(g * Q_GROUP, (g + 1) * Q_GROUP)
        qg = jnp.concatenate([q[:, :, h * HEAD_DIM:(h + 1) * HEAD_DIM] for h in heads], axis=1)
        s = jnp.einsum('bqd,bkd->bqk', qg.astype(BF16), kcat[:, :, g * HEAD_DIM:(g + 1) * HEAD_DIM],
                       preferred_element_type=F32)
        slope = jnp.zeros((rows, 1), F32)
        sink = jnp.zeros((rows, 1), F32)
        for hl, h in enumerate(heads):
            slope = jnp.where(hl_of_row == hl, ALIBI_SLOPES[h], slope)
            sink = jnp.where(hl_of_row == hl, sinks_ref[h], sink)
        s = jnp.where(valid[None], s - (slope * distf)[None], NEG_INF)
        p = _sink_softmax(s, sink[None]).astype(BF16)
        o = jnp.einsum('bqk,bkd->bqd', p, vcat[:, :, g * HEAD_DIM:(g + 1) * HEAD_DIM],
                       preferred_element_type=F32)
        for hl, h in enumerate(heads):
            a_ref[:, h * HEAD_DIM:(h + 1) * HEAD_DIM] = (
                o[:, hl * l:(hl + 1) * l, :].reshape(bb * l, HEAD_DIM).astype(BF16))
    vn = per_seq(vn_ref)
    tt = lax.broadcasted_iota(jnp.int32, (l, vn.shape[2]), 0)
    s = jnp.broadcast_to(bias_ref[...][None], vn.shape)
    for sp in range(l):
        wm = jnp.where(tt >= sp, wexp_ref[sp], 0.0)
        s = s + wm[None] * vn[:, sp:sp + 1, :]
    m_ref[...] = (per_seq(u_ref) * s).reshape(bb * l, vn.shape[2]).astype(BF16)


def _sample_mix(sinks, q, k, v, cache_k, cache_v, vn, u, wexp, bias, n_p):
    b, w, kvw = cache_k.shape
    n, att = q.shape
    l = (n - n_p) // b
    sgw = vn.shape[1]
    bb = 16 if b % 16 == 0 else b
    assert n_p % (bb * l) == 0
    first = n_p // (bb * l)
    tok = lambda width: pl.BlockSpec((bb * l, width), lambda i: (first + i, 0))
    own = lambda width: pl.BlockSpec((bb * l, width), lambda i: (i, 0))
    past = pl.BlockSpec((bb, w, kvw), lambda i: (i, 0, 0))
    outs = [jax.ShapeDtypeStruct((n - n_p, att), BF16), jax.ShapeDtypeStruct((n - n_p, sgw), BF16)]
    return pl.pallas_call(
        _sample_kernel,
        grid=(b // bb,),
        in_specs=[
            pl.BlockSpec(memory_space=pltpu.SMEM),
            tok(att), tok(kvw), tok(kvw), past, past, tok(sgw), tok(sgw),
            pl.BlockSpec(wexp.shape, lambda i: (0, 0, 0)),
            pl.BlockSpec(bias.shape, lambda i: (0, 0)),
        ],
        out_specs=[own(att), own(sgw)],
        out_shape=outs,
        compiler_params=_params("parallel"),
        name="sample_mix",
    )(sinks, q, k, v, cache_k, cache_v, vn, u, wexp, bias)


def _merge_kernel(x_ref, a_ref, m_ref, ga_ref, gb_ref, wa_ref, wb_ref, wo_ref, fg_ref, wq_ref, keys_ref,
                  x1_ref, xnt_ref, s1_ref, s2_ref):
    ha = jnp.dot(a_ref[...], wa_ref[...], preferred_element_type=F32)
    hb = jnp.dot(m_ref[...], wb_ref[...], preferred_element_type=F32)
    h = ga_ref[...].astype(F32) * ha + gb_ref[...].astype(F32) * hb
    x1 = x_ref[...] + jnp.dot(h.astype(BF16), wo_ref[...], preferred_element_type=F32)
    x1_ref[...] = x1
    xn32 = _rms(x1, fg_ref[...])
    xnt_ref[...] = xn32.T.astype(BF16)
    xn = xn32.astype(BF16)
    qp = jnp.dot(xn, wq_ref[...], preferred_element_type=F32).astype(BF16)
    half = keys_ref.shape[2]
    nsub = s1_ref.shape[0]
    for hc in range(keys_ref.shape[0]):
        st = lax.dot_general(keys_ref[hc], qp[:, hc * half:(hc + 1) * half],
                             (((1,), (1,)), ((), ())), preferred_element_type=F32)
        dst = s1_ref if hc % 2 == 0 else s2_ref
        for tl in range(nsub):
            dst[tl, hc // 2] = st[:, tl * LANES:(tl + 1) * LANES]


def _merge(xp, xs, ap, a_s, mp, ms, ga, gb, wa, wb, wo, fg, wq, keys):
    (n_p, d), n_s = xp.shape, xs.shape[0]
    n = n_p + n_s
    tb = _token_block(n_p, n_s)
    nbp = n_p // tb
    nsub = tb // LANES
    hc, nk, half = keys.shape
    row = lambda i: (i, 0)
    const2 = lambda i: (0, 0)
    heads = hc // 2
    outs = [
        jax.ShapeDtypeStruct((n, d), F32),
        jax.ShapeDtypeStruct((d, n), BF16),
        jax.ShapeDtypeStruct((n // LANES, heads, nk, LANES), F32),
        jax.ShapeDtypeStruct((n // LANES, heads, nk, LANES), F32),
    ]

    def body(xp_ref, xs_ref, ap_ref, as_ref, mp_ref, ms_ref, *rest):
        side = lambda p_ref, s_ref: _Value(jnp.where(pl.program_id(0) < nbp, p_ref[...], s_ref[...]))
        _merge_kernel(side(xp_ref, xs_ref), side(ap_ref, as_ref), side(mp_ref, ms_ref), *rest)

    prompt = lambda width: pl.BlockSpec((tb, width), lambda i: (jnp.minimum(i, nbp - 1), 0))
    sample = lambda width: pl.BlockSpec((tb, width), lambda i: (jnp.maximum(i - nbp, 0), 0))
    return pl.pallas_call(
        body,
        grid=(n // tb,),
        in_specs=[
            prompt(d), sample(d),
            prompt(ap.shape[1]), sample(ap.shape[1]),
            prompt(mp.shape[1]), sample(mp.shape[1]),
            pl.BlockSpec((tb, d), row),
            pl.BlockSpec((tb, d), row),
            pl.BlockSpec(wa.shape, const2),
            pl.BlockSpec(wb.shape, const2),
            pl.BlockSpec(wo.shape, const2),
            pl.BlockSpec((1, d), const2),
            pl.BlockSpec(wq.shape, const2),
            pl.BlockSpec(keys.shape, lambda i: (0, 0, 0)),
        ],
        out_specs=[
            pl.BlockSpec((tb, d), row),
            pl.BlockSpec((d, tb), lambda i: (0, i)),
            pl.BlockSpec((nsub, heads, nk, LANES), lambda i: (i, 0, 0, 0)),
            pl.BlockSpec((nsub, heads, nk, LANES), lambda i: (i, 0, 0, 0)),
        ],
        out_shape=outs,
        compiler_params=_params("parallel"),
        name="merge",
    )(xp, xs, ap, a_s, mp, ms, ga, gb, wa, wb, wo, fg, wq, keys)


class _Value:
    def __init__(self, value):
        self._value = value

    def __getitem__(self, idx):
        return self._value[idx]


def _oddeven_merge(lo, hi, r):
    step = r * 2
    if step < hi - lo:
        yield from _oddeven_merge(lo, hi, step)
        yield from _oddeven_merge(lo + r, hi, step)
        yield from [(i, i + r) for i in range(lo + r, hi - r, step)]
    else:
        yield (lo, lo + r)


def _oddeven_merge_sort(lo, hi):
    if hi - lo >= 1:
        mid = lo + (hi - lo) // 2
        yield from _oddeven_merge_sort(lo, mid)
        yield from _oddeven_merge_sort(mid + 1, hi)
        yield from _oddeven_merge(lo, hi, 1)


_SORT_TOPK = tuple(_oddeven_merge_sort(0, PEER_TOPK - 1))


def _cmpx(w, i, j):
    a, b = w[i], w[j]
    if b is None:
        return
    if a is None:
        w[i], w[j] = b, None
        return
    w[i], w[j] = jnp.maximum(a, b), jnp.minimum(a, b)


def _top_values(w):
    k = PEER_TOPK
    w = list(w)
    for i, j in _SORT_TOPK:
        _cmpx(w, i, j)
    shift = SUBLANES // 2
    while shift >= 1:
        y = [None if v is None else pltpu.roll(v, shift, 0) for v in w]
        z = []
        for r in range(k):
            a, b = w[r], y[k - 1 - r]
            z.append(b if a is None else a if b is None else jnp.maximum(a, b))
        stride = k // 2
        while stride >= 1:
            for i in range(k):
                if i & stride == 0:
                    _cmpx(z, i, i + stride)
            stride //= 2
        w = z
        shift //= 2
    return w


def _thresh_kernel(s1_ref, s2_ref, g2_ref, gm_ref, g1_ref):
    k = PEER_TOPK
    nk, lanes = s1_ref.shape[2], s1_ref.shape[3]
    nslot = nk // SUBLANES
    assert nslot == k and k == 2 * SUBLANES
    sub = lax.broadcasted_iota(jnp.int32, (SUBLANES, lanes), 0)

    def pack(vals):
        out = vals[0]
        for j in range(1, SUBLANES):
            out = jnp.where(sub == j, vals[j], out)
        return out

    nheads = s1_ref.shape[1]

    def head(it, carry):
        sb = it // nheads
        h = it - sb * nheads
        w1 = [s1_ref[sb, h, r * SUBLANES:(r + 1) * SUBLANES, :] for r in range(nslot)]
        w2 = [s2_ref[sb, h, r * SUBLANES:(r + 1) * SUBLANES, :] for r in range(nslot)]
        a = _top_values(w1)
        b = _top_values(w2)
        b_lo, b_hi, a_hi = pack(b[:SUBLANES]), pack(b[SUBLANES:]), pack(a[SUBLANES:])
        cands = ([a[0] + b_lo, a[0] + b_hi] + [a[i] + b_lo for i in range(1, SUBLANES)] + [a_hi + b[0]])
        best = _top_values(cands + [None] * (k - len(cands)))
        tau = best[k - 1]
        z = jnp.ones_like(tau)
        for r in range(1, k):
            z = z + jnp.exp(best[r] - best[0])
        inv_z = 1.0 / z
        eb = [jnp.exp(b[j] - b[0]) for j in range(k)]
        gamma = []
        for i in range(k):
            t = jnp.full_like(tau, jnp.inf)
            for j in range(k // (i + 1)):
                t = jnp.where(a[i] + b[j] >= tau, eb[j], t)
            gamma.append(t)
        for r in range(nslot):
            gm = jnp.full_like(tau, jnp.inf)
            for i in range(k - 1, -1, -1):
                gm = jnp.where(w1[r] >= a[i], gamma[i], gm)
            rows = pl.ds(r * SUBLANES, SUBLANES)
            gm_ref[sb, h, rows, :] = gm
            g1_ref[sb, h, rows, :] = jnp.exp(w1[r] - a[0]) * (0.5 * inv_z)
            g2_ref[sb, h, rows, :] = jnp.exp(w2[r] - b[0])
        return carry

    lax.fori_loop(0, s1_ref.shape[0] * nheads, head, 0)


def _thresholds(s1, s2):
    nsub, heads, nk, lanes = s1.shape
    per_step = 4 if nsub % 4 == 0 else 1
    spec = pl.BlockSpec((per_step, heads, nk, lanes), lambda i: (i, 0, 0, 0))
    out = jax.ShapeDtypeStruct(s1.shape, F32)
    return pl.pallas_call(
        _thresh_kernel,
        grid=(nsub // per_step,),
        in_specs=[spec, spec],
        out_specs=[spec, spec, spec],
        out_shape=[out, out, out],
        compiler_params=_params("parallel"),
        name="peer_thresholds",
    )(s1, s2)


GATE_ROWS = 128
GELU_C0 = math.sqrt(2.0 / math.pi)
GELU_C1 = 0.044715 * GELU_C0
PEER_CHUNK = 512
PEER_TOKENS = 1024


def _gate_columns(at_ref, wt_ref, j0, tls, g2_ref, gm_ref, g1_ref):
    _, heads, nk, lanes = g2_ref.shape

    def tile(j, tl, r0):
        gate = jnp.zeros((GATE_ROWS, lanes), BF16)
        for h in range(heads):
            gm = gm_ref[tl, h, j0 + j:j0 + j + 1, :]
            g1 = jnp.broadcast_to(g1_ref[tl, h, j0 + j:j0 + j + 1, :], (GATE_ROWS, lanes)).astype(BF16)
            g2 = g2_ref[tl, h, r0:r0 + GATE_ROWS, :]
            gate = gate + g1 * jnp.where(g2 >= gm, g2, 0.0).astype(BF16)
        rows = slice(j * nk + r0, j * nk + r0 + GATE_ROWS)
        cols = slice(tl * lanes, (tl + 1) * lanes)
        x = at_ref[rows, cols].astype(BF16)
        t = jnp.tanh(x * (GELU_C0 + GELU_C1 * (x * x)))
        wt_ref[rows, cols] = (x + x * t) * gate

    for tl in tls:
        for j in range(at_ref.shape[0] // nk):
            for r0 in range(0, nk, GATE_ROWS):
                tile(j, tl, r0)


def _peer_kernel(xnt_ref, u0a_ref, u0b_ref, una_ref, unb_ref, va_ref, vb_ref, g2_ref, gm_ref, g1_ref, out_ref,
                 at_a, at_b, wt_a, wt_b, acc, xnt):
    s = pl.program_id(1)
    per = at_a.shape[0] // g2_ref.shape[2]
    tb = xnt_ref.shape[1]
    lanes = g2_ref.shape[3]
    tables = (g2_ref, gm_ref, g1_ref)
    cw = min(MXU_COLS, tb)
    columns = [(slice(c, c + cw), range(c // lanes, (c + cw) // lanes)) for c in range(0, tb, cw)]

    def act(u_ref, at, cs):
        at[:, cs] = jnp.dot(u_ref[...], xnt[:, cs], preferred_element_type=F32)

    def mix(v_ref, wt, cs):
        acc[:, cs] += jnp.dot(v_ref[...], wt[:, cs], preferred_element_type=F32)

    @pl.when(s == 0)
    def _():
        xnt[...] = xnt_ref[...]
        for cs, _ in columns:
            act(u0a_ref, at_a, cs)
            act(u0b_ref, at_b, cs)
        acc[...] = jnp.zeros_like(acc)

    for at, wt, j0, v_ref, un_ref in ((at_a, wt_a, 0, va_ref, una_ref), (at_b, wt_b, per, vb_ref, unb_ref)):
        for cs, tls in columns:
            _gate_columns(at, wt, j0, tls, *tables)
            mix(v_ref, wt, cs)
            act(un_ref, at, cs)

    @pl.when(s == pl.num_programs(1) - 1)
    def _():
        out_ref[...] = acc[...].T


def _peer(xnt, u, vt, g2, gm, g1):
    d, n = xnt.shape
    ne = u.shape[0]
    tb = PEER_TOKENS if n % PEER_TOKENS == 0 else _token_block(n)
    nsub = tb // LANES
    _, heads, nk, _ = g2.shape
    ec = PEER_CHUNK
    assert ne % (2 * ec) == 0 and ec % nk == 0 and ne == nk * nk and nk % GATE_ROWS == 0
    nc = ne // ec
    per = ec // nk
    keyed2 = pl.BlockSpec((nsub, heads, nk, LANES), lambda i, s: (i, 0, 0, 0))
    keyed1 = pl.BlockSpec((nsub, heads, 2 * per, LANES), lambda i, s: (i, 0, s, 0))
    return pl.pallas_call(
        _peer_kernel,
        grid=(n // tb, nc // 2),
        in_specs=[
            pl.BlockSpec((d, tb), lambda i, s: (0, i)),
            pl.BlockSpec((ec, d), lambda i, s: (0, 0)),
            pl.BlockSpec((ec, d), lambda i, s: (1, 0)),
            pl.BlockSpec((ec, d), lambda i, s: (jnp.minimum(2 * s + 2, nc - 2), 0)),
            pl.BlockSpec((ec, d), lambda i, s: (jnp.minimum(2 * s + 3, nc - 1), 0)),
            pl.BlockSpec((d, ec), lambda i, s: (0, 2 * s)),
            pl.BlockSpec((d, ec), lambda i, s: (0, 2 * s + 1)),
            keyed2, keyed1, keyed1,
        ],
        out_specs=pl.BlockSpec((tb, d), lambda i, s: (i, 0)),
        out_shape=jax.ShapeDtypeStruct((n, d), F32),
        scratch_shapes=[
            pltpu.VMEM((ec, tb), F32),
            pltpu.VMEM((ec, tb), F32),
            pltpu.VMEM((ec, tb), BF16),
            pltpu.VMEM((ec, tb), BF16),
            pltpu.VMEM((d, tb), F32),
            pltpu.VMEM((d, tb), BF16),
        ],
        compiler_params=_params("parallel", "arbitrary"),
        name="peer_mix",
    )(xnt, u, u, u, u, vt, vt, g2, gm, g1)


def _final_kernel(x1_ref, pe_ref, p_ref, g_ref, wg_ref, wp_ref, y_ref):
    x2 = x1_ref[...] + pe_ref[...]
    xn = _rms(x2, g_ref[...]).astype(BF16)
    gate = jax.nn.sigmoid(jnp.dot(xn, wg_ref[...], preferred_element_type=F32))
    y_ref[...] = x2 + gate * jnp.dot(p_ref[...].astype(BF16), wp_ref[...], preferred_element_type=F32)


def _final(x1, pe, p, g, wg, wp, row0):
    n, pd = p.shape
    d = x1.shape[1]
    tb = _token_block(n, row0)
    first = row0 // tb
    src = lambda i: (first + i, 0)
    row = lambda i: (i, 0)
    const = lambda i: (0, 0)
    return pl.pallas_call(
        _final_kernel,
        grid=(n // tb,),
        in_specs=[
            pl.BlockSpec((tb, d), src),
            pl.BlockSpec((tb, d), src),
            pl.BlockSpec((tb, pd), row),
            pl.BlockSpec((1, d), const),
            pl.BlockSpec(wg.shape, const),
            pl.BlockSpec(wp.shape, const),
        ],
        out_specs=pl.BlockSpec((tb, d), row),
        out_shape=jax.ShapeDtypeStruct((n, d), F32),
        compiler_params=_params("parallel"),
        name="ple_epilogue",
    )(x1, pe, p, g, wg, wp)


def _layer(xp, xs, pp, ps, past_k, past_v, lp):
    b, s, d = xp.shape
    bd, l, _ = xs.shape
    n_p, n_s = b * s, bd * l
    kvw = N_KV_HEADS * HEAD_DIM
    sgw = d // 2
    gd = sgw // SGU_GROUPS
    xp2, xs2 = xp.reshape(n_p, d), xs.reshape(n_s, d)

    q, k, v, u, vn, ga, gb = _inproj(
        xp2, xs2, lp['attn_norm_g'][None], lp['w_in'].astype(BF16),
        jnp.tile(lp['q_norm_g'], N_HEADS)[None], jnp.tile(lp['k_norm_g'], N_KV_HEADS)[None],
        lp['sgu_norm_g'][None], lp['sgu_norm_b'][None])

    sgu_w, sgu_b = lp['sgu_w'], lp['sgu_b']
    bias_p = jnp.repeat(sgu_b.T, gd, axis=1)
    a_p, m_p = _prompt_mix(lp['attn_sinks'], q, k, v, vn, u, sgu_w, bias_p, b, s)
    wexp = jnp.repeat(jnp.transpose(sgu_w[:, :l, :l], (2, 1, 0)), gd, axis=2)
    a_s, m_s = _sample_mix(lp['attn_sinks'], q, k, v, past_k.reshape(bd, -1, kvw),
                           past_v.reshape(bd, -1, kvw), vn, u, wexp, bias_p[:l], n_p)

    keys = lp['peer_sub_keys'].reshape(2 * PEER_HEADS, PEER_N_KEYS, -1).astype(BF16)
    x1, xn1, s1, s2 = _merge(xp2, xs2, a_p, a_s, m_p, m_s, ga, gb, lp['w_branch_a'].astype(BF16),
                             lp['w_branch_b'].astype(BF16), lp['w_out'].astype(BF16), lp['ffn_norm_g'][None],
                             lp['peer_w_q'].astype(BF16), keys)
    g2, gm, g1 = _thresholds(s1, s2)
    pe = _peer(xn1, lp['peer_u'].astype(BF16), lp['peer_v'].astype(BF16).T, g2, gm, g1)
    ple = (lp['ple_norm_g'][None], lp['w_ple_gate'].astype(BF16), lp['w_ple'].astype(BF16))
    y_p = _final(x1, pe, pp.reshape(n_p, -1), *ple, 0)
    y_s = _final(x1, pe, ps.reshape(n_s, -1), *ple, n_p)

    wp = min(WINDOW, s)
    tail = lambda t, rows: jnp.stack([t[(bi + 1) * s - rows:(bi + 1) * s] for bi in range(b)])
    heads = lambda t: t.reshape(t.shape[0], t.shape[1], N_KV_HEADS, HEAD_DIM)
    return (y_p.reshape(b, s, d), y_s.reshape(bd, l, d),
            heads(tail(k, wp)), heads(tail(v, wp)),
            heads(k[n_p:].reshape(bd, l, kvw)), heads(v[n_p:].reshape(bd, l, kvw)),
            tail(vn, CHUNK), vn[n_p:].reshape(bd, l, sgw))


def kernel(x_prompt, x_sample, cache_k, cache_v, p_prompt, p_sample, attn_norm_g, w_in, q_norm_g, k_norm_g, attn_sinks, sgu_norm_g, sgu_norm_b, sgu_w, sgu_b, w_branch_a, w_branch_b, w_out, ffn_norm_g, peer_w_q, peer_sub_keys, peer_u, peer_v, ple_norm_g, w_ple, w_ple_gate):
    depth = w_in.shape[0]
    hp, hs = x_prompt, x_sample
    outs = [[] for _ in range(6)]
    for i in range(depth):
        lp = dict(attn_norm_g=attn_norm_g[i], w_in=w_in[i], q_norm_g=q_norm_g[i], k_norm_g=k_norm_g[i],
                  attn_sinks=attn_sinks[i], sgu_norm_g=sgu_norm_g[i], sgu_norm_b=sgu_norm_b[i],
                  sgu_w=sgu_w[i], sgu_b=sgu_b[i], w_branch_a=w_branch_a[i], w_branch_b=w_branch_b[i],
                  w_out=w_out[i], ffn_norm_g=ffn_norm_g[i], peer_w_q=peer_w_q[i],
                  peer_sub_keys=peer_sub_keys[i], peer_u=peer_u[i], peer_v=peer_v[i],
                  ple_norm_g=ple_norm_g[i], w_ple=w_ple[i], w_ple_gate=w_ple_gate[i])
        res = _layer(hp, hs, p_prompt[i], p_sample[i], cache_k[i], cache_v[i], lp)
        hp, hs = res[0], res[1]
        for lst, t in zip(outs, res[2:]):
            lst.append(t)
    return (hp, hs) + tuple(jnp.stack(o) for o in outs)
```

```python
import functools
import math

import jax
import jax.numpy as jnp
from jax import lax
from jax.experimental import pallas as pl
from jax.experimental.pallas import tpu as pltpu

F32 = jnp.float32
BF16 = jnp.bfloat16

N_HEADS = 8
N_KV_HEADS = 2
HEAD_DIM = 64
Q_GROUP = N_HEADS // N_KV_HEADS
WINDOW = 128
CHUNK = 128
SGU_GROUPS = 4
PEER_HEADS = 8
PEER_N_KEYS = 128
PEER_TOPK = 16
EPS = 1e-6
NEG_INF = -1e30
ALIBI_SLOPES = tuple(2.0 ** (-8.0 * h / N_HEADS) for h in range(1, N_HEADS + 1))

LANES = 128
SUBLANES = 8
MXU_COLS = 256
VMEM_LIMIT = 56 * 1024 * 1024


def _params(*semantics):
    return pltpu.CompilerParams(dimension_semantics=semantics, vmem_limit_bytes=VMEM_LIMIT)


def _token_block(*counts):
    for tb in (512, 256, 128):
        if all(n % tb == 0 for n in counts):
            return tb
    raise ValueError(f"token counts {counts} must be multiples of 128")


def _rms(x, g):
    return x * lax.rsqrt(jnp.mean(x * x, axis=-1, keepdims=True) + EPS) * g


def _group_rms(t, ones_blk, g):
    t2 = t * t
    hi = t2.astype(BF16)
    lo = (t2 - hi.astype(F32)).astype(BF16)
    ss = (jnp.dot(hi, ones_blk, preferred_element_type=F32)
          + jnp.dot(lo, ones_blk, preferred_element_type=F32))
    return t * lax.rsqrt(ss * (1.0 / HEAD_DIM) + EPS) * g


def _inproj_kernel(xp_ref, xs_ref, g_ref, w_ref, qg_ref, kg_ref, lg_ref, lb_ref, bq_ref, bk_ref,
                   q_ref, k_ref, v_ref, u_ref, vn_ref, ga_ref, gb_ref, *, prompt_blocks):
    x = jnp.where(pl.program_id(0) < prompt_blocks, xp_ref[...], xs_ref[...])
    xn = _rms(x, g_ref[...])
    z = jnp.dot(xn.astype(BF16), w_ref[...], preferred_element_type=F32)
    att = N_HEADS * HEAD_DIM
    kvw = N_KV_HEADS * HEAD_DIM
    sgw = (z.shape[1] - att - 2 * kvw) // 6
    o = 0
    q = z[:, o:o + att]; o += att
    k = z[:, o:o + kvw]; o += kvw
    v = z[:, o:o + kvw]; o += kvw
    su = z[:, o:o + sgw]; o += sgw
    sv = z[:, o:o + sgw]; o += sgw
    g_a = z[:, o:o + 2 * sgw]; o += 2 * sgw
    g_b = z[:, o:o + 2 * sgw]
    qn = _group_rms(q, bq_ref[...], qg_ref[...])
    q_ref[...] = (qn * (HEAD_DIM ** -0.5)).astype(BF16)
    k_ref[...] = _group_rms(k, bk_ref[...], kg_ref[...])
    v_ref[...] = v
    u_ref[...] = jax.nn.gelu(su).astype(BF16)
    gv = jax.nn.gelu(sv)
    mu = jnp.mean(gv, axis=-1, keepdims=True)
    gc = gv - mu
    vn_ref[...] = gc * lax.rsqrt(jnp.mean(gc * gc, axis=-1, keepdims=True) + EPS) * lg_ref[...] + lb_ref[...]
    ga_ref[...] = jax.nn.sigmoid(g_a).astype(BF16)
    gb_ref[...] = jax.nn.sigmoid(g_b).astype(BF16)


def _inproj(xp, xs, g, w_in, qg, kg, lg, lb):
    (n_p, d), n_s = xp.shape, xs.shape[0]
    n = n_p + n_s
    tb = _token_block(n_p, n_s)
    nbp = n_p // tb
    att = N_HEADS * HEAD_DIM
    kvw = N_KV_HEADS * HEAD_DIM
    sgw = d // 2
    hid = jnp.arange(att) // HEAD_DIM
    bq = (hid[:, None] == hid[None, :]).astype(BF16)
    bk = bq[:kvw, :kvw]
    const = lambda i: (0, 0)
    row = lambda i: (i, 0)
    outs = [
        jax.ShapeDtypeStruct((n, att), BF16),
        jax.ShapeDtypeStruct((n, kvw), F32),
        jax.ShapeDtypeStruct((n, kvw), F32),
        jax.ShapeDtypeStruct((n, sgw), BF16),
        jax.ShapeDtypeStruct((n, sgw), F32),
        jax.ShapeDtypeStruct((n, d), BF16),
        jax.ShapeDtypeStruct((n, d), BF16),
    ]
    return pl.pallas_call(
        functools.partial(_inproj_kernel, prompt_blocks=nbp),
        grid=(n // tb,),
        in_specs=[
            pl.BlockSpec((tb, d), lambda i: (jnp.minimum(i, nbp - 1), 0)),
            pl.BlockSpec((tb, d), lambda i: (jnp.maximum(i - nbp, 0), 0)),
            pl.BlockSpec((1, d), const),
            pl.BlockSpec(w_in.shape, const),
            pl.BlockSpec((1, att), const),
            pl.BlockSpec((1, kvw), const),
            pl.BlockSpec((1, sgw), const),
            pl.BlockSpec((1, sgw), const),
            pl.BlockSpec((att, att), const),
            pl.BlockSpec((kvw, kvw), const),
        ],
        out_specs=[pl.BlockSpec((tb, s.shape[1]), row) for s in outs],
        out_shape=outs,
        compiler_params=_params("parallel"),
        name="inproj",
    )(xp, xs, g, w_in, qg, kg, lg, lb, bq, bk)


def _sink_softmax(s, sink):
    mx = jnp.maximum(jnp.max(s, axis=-1, keepdims=True), sink)
    p = jnp.exp(s - mx)
    den = jnp.sum(p, axis=-1, keepdims=True) + jnp.exp(sink - mx)
    return p / den


def _prompt_kernel(sinks_ref, q_ref, kc_ref, kp_ref, vc_ref, vp_ref, vn_ref, u_ref, w_ref, bias_ref,
                   a_ref, m_ref):
    i = pl.program_id(1)
    tq = q_ref.shape[0]
    nblk = tq // WINDOW
    q = q_ref[...]
    kc = kc_ref[...].astype(BF16)
    vc = vc_ref[...].astype(BF16)
    kp = kp_ref[...].astype(BF16)
    vp = vp_ref[...].astype(BF16)
    row = lax.broadcasted_iota(jnp.int32, (WINDOW, 2 * WINDOW), 0)
    col = lax.broadcasted_iota(jnp.int32, (WINDOW, 2 * WINDOW), 1)
    dist = row - col + WINDOW
    in_window = (dist >= 0) & (dist < WINDOW)
    distf = dist.astype(F32)
    for jq in range(nblk):
        r0 = jq * WINDOW
        if jq == 0:
            kprev, vprev = kp, vp
            valid = in_window & (col >= jnp.where(i > 0, 0, WINDOW))
        else:
            kprev, vprev = kc[r0 - WINDOW:r0], vc[r0 - WINDOW:r0]
            valid = in_window
        kcat = jnp.concatenate([kprev, kc[r0:r0 + WINDOW]], axis=0)
        vcat = jnp.concatenate([vprev, vc[r0:r0 + WINDOW]], axis=0)
        for g in range(N_KV_HEADS):
            heads = range(g * Q_GROUP, (g + 1) * Q_GROUP)
            qg = jnp.concatenate([q[r0:r0 + WINDOW, h * HEAD_DIM:(h + 1) * HEAD_DIM] for h in heads], axis=0)
            s_all = lax.dot_general(qg, kcat[:, g * HEAD_DIM:(g + 1) * HEAD_DIM],
                                    (((1,), (1,)), ((), ())), preferred_element_type=F32)
            probs = []
            for hl, h in enumerate(heads):
                s = s_all[hl * WINDOW:(hl + 1) * WINDOW] - ALIBI_SLOPES[h] * distf
                s = jnp.where(valid, s, NEG_INF)
                probs.append(_sink_softmax(s, sinks_ref[h]).astype(BF16))
            o_all = jnp.dot(jnp.concatenate(probs, axis=0), vcat[:, g * HEAD_DIM:(g + 1) * HEAD_DIM],
                            preferred_element_type=F32)
            for hl, h in enumerate(heads):
                a_ref[r0:r0 + WINDOW, h * HEAD_DIM:(h + 1) * HEAD_DIM] = (
                    o_all[hl * WINDOW:(hl + 1) * WINDOW].astype(BF16))
    tr = lax.broadcasted_iota(jnp.int32, (CHUNK, CHUNK), 0)
    tc = lax.broadcasted_iota(jnp.int32, (CHUNK, CHUNK), 1)
    gd = vn_ref.shape[1] // SGU_GROUPS
    wm = [jnp.where(tr >= tc, w_ref[g], 0.0).astype(BF16) for g in range(SGU_GROUPS)]
    for c in range(tq // CHUNK):
        r0 = c * CHUNK
        vnc = vn_ref[r0:r0 + CHUNK, :].astype(BF16)
        for g in range(SGU_GROUPS):
            s = jnp.dot(wm[g], vnc[:, g * gd:(g + 1) * gd], preferred_element_type=F32)
            s = s + bias_ref[:, g * gd:(g + 1) * gd]
            m_ref[r0:r0 + CHUNK, g * gd:(g + 1) * gd] = (
                u_ref[r0:r0 + CHUNK, g * gd:(g + 1) * gd].astype(F32) * s).astype(BF16)


def _prompt_mix(sinks, q, k, v, vn, u, sgu_w, sgu_bias, b, s):
    n, att = b * s, q.shape[1]
    kvw = k.shape[1]
    sgw = vn.shape[1]
    tq = 512 if s % 512 == 0 else WINDOW
    assert s % tq == 0 and tq % WINDOW == 0 and WINDOW == CHUNK
    r = tq // WINDOW
    nq = s // tq
    cur = lambda bi, i: (bi * nq + i, 0)
    prev = lambda bi, i: (jnp.maximum((bi * nq + i) * r - 1, 0), 0)
    outs = [jax.ShapeDtypeStruct((n, att), BF16), jax.ShapeDtypeStruct((n, sgw), BF16)]
    return pl.pallas_call(
        _prompt_kernel,
        grid=(b, nq),
        in_specs=[
            pl.BlockSpec(memory_space=pltpu.SMEM),
            pl.BlockSpec((tq, att), cur),
            pl.BlockSpec((tq, kvw), cur),
            pl.BlockSpec((WINDOW, kvw), prev),
            pl.BlockSpec((tq, kvw), cur),
            pl.BlockSpec((WINDOW, kvw), prev),
            pl.BlockSpec((tq, sgw), cur),
            pl.BlockSpec((tq, sgw), cur),
            pl.BlockSpec(sgu_w.shape, lambda bi, i: (0, 0, 0)),
            pl.BlockSpec(sgu_bias.shape, lambda bi, i: (0, 0)),
        ],
        out_specs=[pl.BlockSpec((tq, att), cur), pl.BlockSpec((tq, sgw), cur)],
        out_shape=outs,
        compiler_params=_params("parallel", "parallel"),
        name="prompt_mix",
    )(sinks, q, k, k, v, v, vn, u, sgu_w, sgu_bias)


def _sample_kernel(sinks_ref, q_ref, kn_ref, vn_new_ref, ck_ref, cv_ref, vn_ref, u_ref, wexp_ref, bias_ref,
                   a_ref, m_ref):
    bb, w, _ = ck_ref.shape
    l = q_ref.shape[0] // bb
    per_seq = lambda ref: ref[...].astype(F32).reshape(bb, l, ref.shape[1])
    q = per_seq(q_ref)
    kcat = jnp.concatenate([ck_ref[...], per_seq(kn_ref)], axis=1).astype(BF16)
    vcat = jnp.concatenate([cv_ref[...], per_seq(vn_new_ref)], axis=1).astype(BF16)
    rows = Q_GROUP * l
    t = lax.broadcasted_iota(jnp.int32, (rows, w + l), 0) % l
    key = lax.broadcasted_iota(jnp.int32, (rows, w + l), 1)
    dist = t - (key - w)
    valid = (dist >= 0) & (dist < WINDOW)
    distf = dist.astype(F32)
    hl_of_row = lax.broadcasted_iota(jnp.int32, (rows, 1), 0) // l
    for g in range(N_KV_HEADS):
        heads = range(g * Q_GROUP, (g + 1) * Q_GROUP)
        qg = jnp.concatenate([q[:, :, h * HEAD_DIM:(h + 1) * HEAD_DIM] for h in heads], axis=1)
        s = jnp.einsum('bqd,bkd->bqk', qg.astype(BF16), kcat[:, :, g * HEAD_DIM:(g + 1) * HEAD_DIM],
                       preferred_element_type=F32)
        slope = jnp.zeros((rows, 1), F32)
        sink = jnp.zeros((rows, 1), F32)
        for hl, h in enumerate(heads):
            slope = jnp.where(hl_of_row == hl, ALIBI_SLOPES[h], slope)
            sink = jnp.where(hl_of_row == hl, sinks_ref[h], sink)
        s = jnp.where(valid[None], s - (slope * distf)[None], NEG_INF)
        p = _sink_softmax(s, sink[None]).astype(BF16)
        o = jnp.einsum('bqk,bkd->bqd', p, vcat[:, :, g * HEAD_DIM:(g + 1) * HEAD_DIM],
                       preferred_element_type=F32)
        for hl, h in enumerate(heads):
            a_ref[:, h * HEAD_DIM:(h + 1) * HEAD_DIM] = (
                o[:, hl * l:(hl + 1) * l, :].reshape(bb * l, HEAD_DIM).astype(BF16))
    vn = per_seq(vn_ref)
    tt = lax.broadcasted_iota(jnp.int32, (l, vn.shape[2]), 0)
    s = jnp.broadcast_to(bias_ref[...][None], vn.shape)
    for sp in range(l):
        wm = jnp.where(tt >= sp, wexp_ref[sp], 0.0)
        s = s + wm[None] * vn[:, sp:sp + 1, :]
    m_ref[...] = (per_seq(u_ref) * s).reshape(bb * l, vn.shape[2]).astype(BF16)


def _sample_mix(sinks, q, k, v, cache_k, cache_v, vn, u, wexp, bias, n_p):
    b, w, kvw = cache_k.shape
    n, att = q.shape
    l = (n - n_p) // b
    sgw = vn.shape[1]
    bb = 16 if b % 16 == 0 else b
    assert n_p % (bb * l) == 0
    first = n_p // (bb * l)
    tok = lambda width: pl.BlockSpec((bb * l, width), lambda i: (first + i, 0))
    own = lambda width: pl.BlockSpec((bb * l, width), lambda i: (i, 0))
    past = pl.BlockSpec((bb, w, kvw), lambda i: (i, 0, 0))
    outs = [jax.ShapeDtypeStruct((n - n_p, att), BF16), jax.ShapeDtypeStruct((n - n_p, sgw), BF16)]
    return pl.pallas_call(
        _sample_kernel,
        grid=(b // bb,),
        in_specs=[
            pl.BlockSpec(memory_space=pltpu.SMEM),
            tok(att), tok(kvw), tok(kvw), past, past, tok(sgw), tok(sgw),
            pl.BlockSpec(wexp.shape, lambda i: (0, 0, 0)),
            pl.BlockSpec(bias.shape, lambda i: (0, 0)),
        ],
        out_specs=[own(att), own(sgw)],
        out_shape=outs,
        compiler_params=_params("parallel"),
        name="sample_mix",
    )(sinks, q, k, v, cache_k, cache_v, vn, u, wexp, bias)


def _merge_kernel(x_ref, a_ref, m_ref, ga_ref, gb_ref, wa_ref, wb_ref, wo_ref, fg_ref, wq_ref, keys_ref,
                  x1_ref, xnt_ref, s1_ref, s2_ref):
    ha = jnp.dot(a_ref[...], wa_ref[...], preferred_element_type=F32)
    hb = jnp.dot(m_ref[...], wb_ref[...], preferred_element_type=F32)
    h = ga_ref[...].astype(F32) * ha + gb_ref[...].astype(F32) * hb
    x1 = x_ref[...] + jnp.dot(h.astype(BF16), wo_ref[...], preferred_element_type=F32)
    x1_ref[...] = x1
    xn32 = _rms(x1, fg_ref[...])
    xnt_ref[...] = xn32.T.astype(BF16)
    xn = xn32.astype(BF16)
    qp = jnp.dot(xn, wq_ref[...], preferred_element_type=F32).astype(BF16)
    half = keys_ref.shape[2]
    nsub = s1_ref.shape[0]
    for hc in range(keys_ref.shape[0]):
        st = lax.dot_general(keys_ref[hc], qp[:, hc * half:(hc + 1) * half],
                             (((1,), (1,)), ((), ())), preferred_element_type=F32)
        dst = s1_ref if hc % 2 == 0 else s2_ref
        for tl in range(nsub):
            dst[tl, hc // 2] = st[:, tl * LANES:(tl + 1) * LANES]


def _merge(xp, xs, ap, a_s, mp, ms, ga, gb, wa, wb, wo, fg, wq, keys):
    (n_p, d), n_s = xp.shape, xs.shape[0]
    n = n_p + n_s
    tb = _token_block(n_p, n_s)
    nbp = n_p // tb
    nsub = tb // LANES
    hc, nk, half = keys.shape
    row = lambda i: (i, 0)
    const2 = lambda i: (0, 0)
    heads = hc // 2
    outs = [
        jax.ShapeDtypeStruct((n, d), F32),
        jax.ShapeDtypeStruct((d, n), BF16),
        jax.ShapeDtypeStruct((n // LANES, heads, nk, LANES), F32),
        jax.ShapeDtypeStruct((n // LANES, heads, nk, LANES), F32),
    ]

    def body(xp_ref, xs_ref, ap_ref, as_ref, mp_ref, ms_ref, *rest):
        side = lambda p_ref, s_ref: _Value(jnp.where(pl.program_id(0) < nbp, p_ref[...], s_ref[...]))
        _merge_kernel(side(xp_ref, xs_ref), side(ap_ref, as_ref), side(mp_ref, ms_ref), *rest)

    prompt = lambda width: pl.BlockSpec((tb, width), lambda i: (jnp.minimum(i, nbp - 1), 0))
    sample = lambda width: pl.BlockSpec((tb, width), lambda i: (jnp.maximum(i - nbp, 0), 0))
    return pl.pallas_call(
        body,
        grid=(n // tb,),
        in_specs=[
            prompt(d), sample(d),
            prompt(ap.shape[1]), sample(ap.shape[1]),
            prompt(mp.shape[1]), sample(mp.shape[1]),
            pl.BlockSpec((tb, d), row),
            pl.BlockSpec((tb, d), row),
            pl.BlockSpec(wa.shape, const2),
            pl.BlockSpec(wb.shape, const2),
            pl.BlockSpec(wo.shape, const2),
            pl.BlockSpec((1, d), const2),
            pl.BlockSpec(wq.shape, const2),
            pl.BlockSpec(keys.shape, lambda i: (0, 0, 0)),
        ],
        out_specs=[
            pl.BlockSpec((tb, d), row),
            pl.BlockSpec((d, tb), lambda i: (0, i)),
            pl.BlockSpec((nsub, heads, nk, LANES), lambda i: (i, 0, 0, 0)),
            pl.BlockSpec((nsub, heads, nk, LANES), lambda i: (i, 0, 0, 0)),
        ],
        out_shape=outs,
        compiler_params=_params("parallel"),
        name="merge",
    )(xp, xs, ap, a_s, mp, ms, ga, gb, wa, wb, wo, fg, wq, keys)


class _Value:
    def __init__(self, value):
        self._value = value

    def __getitem__(self, idx):
        return self._value[idx]


def _oddeven_merge(lo, hi, r):
    step = r * 2
    if step < hi - lo:
        yield from _oddeven_merge(lo, hi, step)
        yield from _oddeven_merge(lo + r, hi, step)
        yield from [(i, i + r) for i in range(lo + r, hi - r, step)]
    else:
        yield (lo, lo + r)


def _oddeven_merge_sort(lo, hi):
    if hi - lo >= 1:
        mid = lo + (hi - lo) // 2
        yield from _oddeven_merge_sort(lo, mid)
        yield from _oddeven_merge_sort(mid + 1, hi)
        yield from _oddeven_merge(lo, hi, 1)


_SORT_TOPK = tuple(_oddeven_merge_sort(0, PEER_TOPK - 1))


def _cmpx(w, i, j):
    a, b = w[i], w[j]
    if b is None:
        return
    if a is None:
        w[i], w[j] = b, None
        return
    w[i], w[j] = jnp.maximum(a, b), jnp.minimum(a, b)


def _top_values(w):
    k = PEER_TOPK
    w = list(w)
    for i, j in _SORT_TOPK:
        _cmpx(w, i, j)
    shift = SUBLANES // 2
    while shift >= 1:
        y = [None if v is None else pltpu.roll(v, shift, 0) for v in w]
        z = []
        for r in range(k):
            a, b = w[r], y[k - 1 - r]
            z.append(b if a is None else a if b is None else jnp.maximum(a, b))
        stride = k // 2
        while stride >= 1:
            for i in range(k):
                if i & stride == 0:
                    _cmpx(z, i, i + stride)
            stride //= 2
        w = z
        shift //= 2
    return w


def _thresh_kernel(s1_ref, s2_ref, g2_ref, gm_ref, g1_ref):
    k = PEER_TOPK
    nk, lanes = s1_ref.shape[2], s1_ref.shape[3]
    nslot = nk // SUBLANES
    assert nslot == k and k == 2 * SUBLANES
    sub = lax.broadcasted_iota(jnp.int32, (SUBLANES, lanes), 0)

    def pack(vals):
        out = vals[0]
        for j in range(1, SUBLANES):
            out = jnp.where(sub == j, vals[j], out)
        return out

    def head(h, carry):
        w1 = [s1_ref[0, h, r * SUBLANES:(r + 1) * SUBLANES, :] for r in range(nslot)]
        w2 = [s2_ref[0, h, r * SUBLANES:(r + 1) * SUBLANES, :] for r in range(nslot)]
        a = _top_values(w1)
        b = _top_values(w2)
        b_lo, b_hi, a_hi = pack(b[:SUBLANES]), pack(b[SUBLANES:]), pack(a[SUBLANES:])
        cands = ([a[0] + b_lo, a[0] + b_hi] + [a[i] + b_lo for i in range(1, SUBLANES)] + [a_hi + b[0]])
        best = _top_values(cands + [None] * (k - len(cands)))
        tau = best[k - 1]
        z = jnp.ones_like(tau)
        for r in range(1, k):
            z = z + jnp.exp(best[r] - best[0])
        inv_z = 1.0 / z
        eb = [jnp.exp(b[j] - b[0]) for j in range(k)]
        gamma = []
        for i in range(k):
            t = jnp.full_like(tau, jnp.inf)
            for j in range(k // (i + 1)):
                t = jnp.where(a[i] + b[j] >= tau, eb[j], t)
            gamma.append(t)
        for r in range(nslot):
            gm = jnp.full_like(tau, jnp.inf)
            for i in range(k - 1, -1, -1):
                gm = jnp.where(w1[r] >= a[i], gamma[i], gm)
            rows = pl.ds(r * SUBLANES, SUBLANES)
            gm_ref[0, h, rows, :] = gm
            g1_ref[0, h, rows, :] = jnp.exp(w1[r] - a[0]) * (0.5 * inv_z)
            g2_ref[0, h, rows, :] = jnp.exp(w2[r] - b[0])
        return carry

    lax.fori_loop(0, s1_ref.shape[1], head, 0)


def _thresholds(s1, s2):
    nsub, heads, nk, lanes = s1.shape
    spec = pl.BlockSpec((1, heads, nk, lanes), lambda i: (i, 0, 0, 0))
    out = jax.ShapeDtypeStruct(s1.shape, F32)
    return pl.pallas_call(
        _thresh_kernel,
        grid=(nsub,),
        in_specs=[spec, spec],
        out_specs=[spec, spec, spec],
        out_shape=[out, out, out],
        compiler_params=_params("parallel"),
        name="peer_thresholds",
    )(s1, s2)


GATE_ROWS = 128
GELU_C0 = math.sqrt(2.0 / math.pi)
GELU_C1 = 0.044715 * GELU_C0
PEER_CHUNK = 512
PEER_TOKENS = 1024


def _gate_columns(at_ref, wt_ref, j0, tls, g2_ref, gm_ref, g1_ref):
    _, heads, nk, lanes = g2_ref.shape

    def tile(j, tl, r0):
        gate = jnp.zeros((GATE_ROWS, lanes), BF16)
        for h in range(heads):
            gm = gm_ref[tl, h, j0 + j:j0 + j + 1, :]
            g1 = jnp.broadcast_to(g1_ref[tl, h, j0 + j:j0 + j + 1, :], (GATE_ROWS, lanes)).astype(BF16)
            g2 = g2_ref[tl, h, r0:r0 + GATE_ROWS, :]
            gate = gate + g1 * jnp.where(g2 >= gm, g2, 0.0).astype(BF16)
        rows = slice(j * nk + r0, j * nk + r0 + GATE_ROWS)
        cols = slice(tl * lanes, (tl + 1) * lanes)
        x = at_ref[rows, cols].astype(BF16)
        t = jnp.tanh(x * (GELU_C0 + GELU_C1 * (x * x)))
        wt_ref[rows, cols] = (x + x * t) * gate

    for tl in tls:
        for j in range(at_ref.shape[0] // nk):
            for r0 in range(0, nk, GATE_ROWS):
                tile(j, tl, r0)


def _peer_kernel(xnt_ref, u0a_ref, u0b_ref, una_ref, unb_ref, va_ref, vb_ref, g2_ref, gm_ref, g1_ref, out_ref,
                 at_a, at_b, wt_a, wt_b, acc, xnt):
    s = pl.program_id(1)
    per = at_a.shape[0] // g2_ref.shape[2]
    tb = xnt_ref.shape[1]
    lanes = g2_ref.shape[3]
    tables = (g2_ref, gm_ref, g1_ref)
    cw = min(MXU_COLS, tb)
    columns = [(slice(c, c + cw), range(c // lanes, (c + cw) // lanes)) for c in range(0, tb, cw)]

    def act(u_ref, at, cs):
        at[:, cs] = jnp.dot(u_ref[...], xnt[:, cs], preferred_element_type=F32)

    def mix(v_ref, wt, cs):
        acc[:, cs] += jnp.dot(v_ref[...], wt[:, cs], preferred_element_type=F32)

    first = (s == 0) & (pl.program_id(0) == 0)

    @pl.when(first | (s == pl.num_programs(1) - 1))
    def _():
        xnt[...] = xnt_ref[...]

    @pl.when(first)
    def _():
        for cs, _ in columns:
            act(u0a_ref, at_a, cs)
            act(u0b_ref, at_b, cs)

    @pl.when(s == 0)
    def _():
        acc[...] = jnp.zeros_like(acc)

    for at, wt, j0, v_ref, un_ref in ((at_a, wt_a, 0, va_ref, una_ref), (at_b, wt_b, per, vb_ref, unb_ref)):
        for cs, tls in columns:
            _gate_columns(at, wt, j0, tls, *tables)
            mix(v_ref, wt, cs)
            act(un_ref, at, cs)

    @pl.when(s == pl.num_programs(1) - 1)
    def _():
        out_ref[...] = acc[...].T


def _peer(xnt, u, vt, g2, gm, g1):
    d, n = xnt.shape
    ne = u.shape[0]
    tb = PEER_TOKENS if n % PEER_TOKENS == 0 else _token_block(n)
    nsub = tb // LANES
    _, heads, nk, _ = g2.shape
    ec = PEER_CHUNK
    assert ne % (2 * ec) == 0 and ec % nk == 0 and ne == nk * nk and nk % GATE_ROWS == 0
    nc = ne // ec
    per = ec // nk
    keyed2 = pl.BlockSpec((nsub, heads, nk, LANES), lambda i, s: (i, 0, 0, 0))
    keyed1 = pl.BlockSpec((nsub, heads, 2 * per, LANES), lambda i, s: (i, 0, s, 0))
    steps, nb = nc // 2, n // tb
    ahead = lambda i, s: jnp.minimum(i + (s == steps - 1).astype(jnp.int32), nb - 1)
    return pl.pallas_call(
        _peer_kernel,
        grid=(nb, steps),
        in_specs=[
            pl.BlockSpec((d, tb), lambda i, s: (0, ahead(i, s))),
            pl.BlockSpec((ec, d), lambda i, s: (0, 0)),
            pl.BlockSpec((ec, d), lambda i, s: (1, 0)),
            pl.BlockSpec((ec, d), lambda i, s: ((2 * s + 2) % nc, 0)),
            pl.BlockSpec((ec, d), lambda i, s: ((2 * s + 3) % nc, 0)),
            pl.BlockSpec((d, ec), lambda i, s: (0, 2 * s)),
            pl.BlockSpec((d, ec), lambda i, s: (0, 2 * s + 1)),
            keyed2, keyed1, keyed1,
        ],
        out_specs=pl.BlockSpec((tb, d), lambda i, s: (i, 0)),
        out_shape=jax.ShapeDtypeStruct((n, d), F32),
        scratch_shapes=[
            pltpu.VMEM((ec, tb), F32),
            pltpu.VMEM((ec, tb), F32),
            pltpu.VMEM((ec, tb), BF16),
            pltpu.VMEM((ec, tb), BF16),
            pltpu.VMEM((d, tb), F32),
            pltpu.VMEM((d, tb), BF16),
        ],
        compiler_params=_params("arbitrary", "arbitrary"),
        name="peer_mix",
    )(xnt, u, u, u, u, vt, vt, g2, gm, g1)


def _final_kernel(x1_ref, pe_ref, p_ref, g_ref, wg_ref, wp_ref, y_ref):
    x2 = x1_ref[...] + pe_ref[...]
    xn = _rms(x2, g_ref[...]).astype(BF16)
    gate = jax.nn.sigmoid(jnp.dot(xn, wg_ref[...], preferred_element_type=F32))
    y_ref[...] = x2 + gate * jnp.dot(p_ref[...].astype(BF16), wp_ref[...], preferred_element_type=F32)


def _final(x1, pe, p, g, wg, wp, row0):
    n, pd = p.shape
    d = x1.shape[1]
    tb = _token_block(n, row0)
    first = row0 // tb
    src = lambda i: (first + i, 0)
    row = lambda i: (i, 0)
    const = lambda i: (0, 0)
    return pl.pallas_call(
        _final_kernel,
        grid=(n // tb,),
        in_specs=[
            pl.BlockSpec((tb, d), src),
            pl.BlockSpec((tb, d), src),
            pl.BlockSpec((tb, pd), row),
            pl.BlockSpec((1, d), const),
            pl.BlockSpec(wg.shape, const),
            pl.BlockSpec(wp.shape, const),
        ],
        out_specs=pl.BlockSpec((tb, d), row),
        out_shape=jax.ShapeDtypeStruct((n, d), F32),
        compiler_params=_params("parallel"),
        name="ple_epilogue",
    )(x1, pe, p, g, wg, wp)


def _layer(xp, xs, pp, ps, past_k, past_v, lp):
    b, s, d = xp.shape
    bd, l, _ = xs.shape
    n_p, n_s = b * s, bd * l
    kvw = N_KV_HEADS * HEAD_DIM
    sgw = d // 2
    gd = sgw // SGU_GROUPS
    xp2, xs2 = xp.reshape(n_p, d), xs.reshape(n_s, d)

    q, k, v, u, vn, ga, gb = _inproj(
        xp2, xs2, lp['attn_norm_g'][None], lp['w_in'].astype(BF16),
        jnp.tile(lp['q_norm_g'], N_HEADS)[None], jnp.tile(lp['k_norm_g'], N_KV_HEADS)[None],
        lp['sgu_norm_g'][None], lp['sgu_norm_b'][None])

    sgu_w, sgu_b = lp['sgu_w'], lp['sgu_b']
    bias_p = jnp.repeat(sgu_b.T, gd, axis=1)
    a_p, m_p = _prompt_mix(lp['attn_sinks'], q, k, v, vn, u, sgu_w, bias_p, b, s)
    wexp = jnp.repeat(jnp.transpose(sgu_w[:, :l, :l], (2, 1, 0)), gd, axis=2)
    a_s, m_s = _sample_mix(lp['attn_sinks'], q, k, v, past_k.reshape(bd, -1, kvw),
                           past_v.reshape(bd, -1, kvw), vn, u, wexp, bias_p[:l], n_p)

    keys = lp['peer_sub_keys'].reshape(2 * PEER_HEADS, PEER_N_KEYS, -1).astype(BF16)
    x1, xn1, s1, s2 = _merge(xp2, xs2, a_p, a_s, m_p, m_s, ga, gb, lp['w_branch_a'].astype(BF16),
                             lp['w_branch_b'].astype(BF16), lp['w_out'].astype(BF16), lp['ffn_norm_g'][None],
                             lp['peer_w_q'].astype(BF16), keys)
    g2, gm, g1 = _thresholds(s1, s2)
    pe = _peer(xn1, lp['peer_u'].astype(BF16), lp['peer_v'].astype(BF16).T, g2, gm, g1)
    ple = (lp['ple_norm_g'][None], lp['w_ple_gate'].astype(BF16), lp['w_ple'].astype(BF16))
    y_p = _final(x1, pe, pp.reshape(n_p, -1), *ple, 0)
    y_s = _final(x1, pe, ps.reshape(n_s, -1), *ple, n_p)

    wp = min(WINDOW, s)
    tail = lambda t, rows: jnp.stack([t[(bi + 1) * s - rows:(bi + 1) * s] for bi in range(b)])
    heads = lambda t: t.reshape(t.shape[0], t.shape[1], N_KV_HEADS, HEAD_DIM)
    return (y_p.reshape(b, s, d), y_s.reshape(bd, l, d),
            heads(tail(k, wp)), heads(tail(v, wp)),
            heads(k[n_p:].reshape(bd, l, kvw)), heads(v[n_p:].reshape(bd, l, kvw)),
            tail(vn, CHUNK), vn[n_p:].reshape(bd, l, sgw))


def kernel(x_prompt, x_sample, cache_k, cache_v, p_prompt, p_sample, attn_norm_g, w_in, q_norm_g, k_norm_g, attn_sinks, sgu_norm_g, sgu_norm_b, sgu_w, sgu_b, w_branch_a, w_branch_b, w_out, ffn_norm_g, peer_w_q, peer_sub_keys, peer_u, peer_v, ple_norm_g, w_ple, w_ple_gate):
    depth = w_in.shape[0]
    hp, hs = x_prompt, x_sample
    outs = [[] for _ in range(6)]
    for i in range(depth):
        lp = dict(attn_norm_g=attn_norm_g[i], w_in=w_in[i], q_norm_g=q_norm_g[i], k_norm_g=k_norm_g[i],
                  attn_sinks=attn_sinks[i], sgu_norm_g=sgu_norm_g[i], sgu_norm_b=sgu_norm_b[i],
                  sgu_w=sgu_w[i], sgu_b=sgu_b[i], w_branch_a=w_branch_a[i], w_branch_b=w_branch_b[i],
                  w_out=w_out[i], ffn_norm_g=ffn_norm_g[i], peer_w_q=peer_w_q[i],
                  peer_sub_keys=peer_sub_keys[i], peer_u=peer_u[i], peer_v=peer_v[i],
                  ple_norm_g=ple_norm_g[i], w_ple=w_ple[i], w_ple_gate=w_ple_gate[i])
        res = _layer(hp, hs, p_prompt[i], p_sample[i], cache_k[i], cache_v[i], lp)
        hp, hs = res[0], res[1]
        for lst, t in zip(outs, res[2:]):
            lst.append(t)
    return (hp, hs) + tuple(jnp.stack(o) for o in outs)
```

```python
import functools
import math

import jax
import jax.numpy as jnp
from jax import lax
from jax.experimental import pallas as pl
from jax.experimental.pallas import tpu as pltpu

F32 = jnp.float32
BF16 = jnp.bfloat16

N_HEADS = 8
N_KV_HEADS = 2
HEAD_DIM = 64
Q_GROUP = N_HEADS // N_KV_HEADS
WINDOW = 128
CHUNK = 128
SGU_GROUPS = 4
PEER_HEADS = 8
PEER_N_KEYS = 128
PEER_TOPK = 16
EPS = 1e-6
NEG_INF = -1e30
ALIBI_SLOPES = tuple(2.0 ** (-8.0 * h / N_HEADS) for h in range(1, N_HEADS + 1))

LANES = 128
SUBLANES = 8
MXU_COLS = 256
VMEM_LIMIT = 56 * 1024 * 1024


def _params(*semantics):
    return pltpu.CompilerParams(dimension_semantics=semantics, vmem_limit_bytes=VMEM_LIMIT)


def _token_block(*counts):
    for tb in (512, 256, 128):
        if all(n % tb == 0 for n in counts):
            return tb
    raise ValueError(f"token counts {counts} must be multiples of 128")


def _rms(x, g):
    return x * lax.rsqrt(jnp.mean(x * x, axis=-1, keepdims=True) + EPS) * g


def _group_rms(t, ones_blk, g):
    t2 = t * t
    hi = t2.astype(BF16)
    lo = (t2 - hi.astype(F32)).astype(BF16)
    ss = (jnp.dot(hi, ones_blk, preferred_element_type=F32)
          + jnp.dot(lo, ones_blk, preferred_element_type=F32))
    return t * lax.rsqrt(ss * (1.0 / HEAD_DIM) + EPS) * g


def _inproj_kernel(xp_ref, xs_ref, g_ref, w_ref, qg_ref, kg_ref, lg_ref, lb_ref, bq_ref, bk_ref,
                   q_ref, k_ref, v_ref, u_ref, vn_ref, ga_ref, gb_ref, *, prompt_blocks):
    x = jnp.where(pl.program_id(0) < prompt_blocks, xp_ref[...], xs_ref[...])
    xn = _rms(x, g_ref[...])
    z = jnp.dot(xn.astype(BF16), w_ref[...], preferred_element_type=F32)
    att = N_HEADS * HEAD_DIM
    kvw = N_KV_HEADS * HEAD_DIM
    sgw = (z.shape[1] - att - 2 * kvw) // 6
    o = 0
    q = z[:, o:o + att]; o += att
    k = z[:, o:o + kvw]; o += kvw
    v = z[:, o:o + kvw]; o += kvw
    su = z[:, o:o + sgw]; o += sgw
    sv = z[:, o:o + sgw]; o += sgw
    g_a = z[:, o:o + 2 * sgw]; o += 2 * sgw
    g_b = z[:, o:o + 2 * sgw]
    qn = _group_rms(q, bq_ref[...], qg_ref[...])
    q_ref[...] = (qn * (HEAD_DIM ** -0.5)).astype(BF16)
    k_ref[...] = _group_rms(k, bk_ref[...], kg_ref[...])
    v_ref[...] = v
    u_ref[...] = jax.nn.gelu(su).astype(BF16)
    gv = jax.nn.gelu(sv)
    mu = jnp.mean(gv, axis=-1, keepdims=True)
    gc = gv - mu
    vn_ref[...] = gc * lax.rsqrt(jnp.mean(gc * gc, axis=-1, keepdims=True) + EPS) * lg_ref[...] + lb_ref[...]
    ga_ref[...] = jax.nn.sigmoid(g_a).astype(BF16)
    gb_ref[...] = jax.nn.sigmoid(g_b).astype(BF16)


def _inproj(xp, xs, g, w_in, qg, kg, lg, lb):
    (n_p, d), n_s = xp.shape, xs.shape[0]
    n = n_p + n_s
    tb = _token_block(n_p, n_s)
    nbp = n_p // tb
    att = N_HEADS * HEAD_DIM
    kvw = N_KV_HEADS * HEAD_DIM
    sgw = d // 2
    hid = jnp.arange(att) // HEAD_DIM
    bq = (hid[:, None] == hid[None, :]).astype(BF16)
    bk = bq[:kvw, :kvw]
    const = lambda i: (0, 0)
    row = lambda i: (i, 0)
    outs = [
        jax.ShapeDtypeStruct((n, att), BF16),
        jax.ShapeDtypeStruct((n, kvw), F32),
        jax.ShapeDtypeStruct((n, kvw), F32),
        jax.ShapeDtypeStruct((n, sgw), BF16),
        jax.ShapeDtypeStruct((n, sgw), F32),
        jax.ShapeDtypeStruct((n, d), BF16),
        jax.ShapeDtypeStruct((n, d), BF16),
    ]
    return pl.pallas_call(
        functools.partial(_inproj_kernel, prompt_blocks=nbp),
        grid=(n // tb,),
        in_specs=[
            pl.BlockSpec((tb, d), lambda i: (jnp.minimum(i, nbp - 1), 0)),
            pl.BlockSpec((tb, d), lambda i: (jnp.maximum(i - nbp, 0), 0)),
            pl.BlockSpec((1, d), const),
            pl.BlockSpec(w_in.shape, const),
            pl.BlockSpec((1, att), const),
            pl.BlockSpec((1, kvw), const),
            pl.BlockSpec((1, sgw), const),
            pl.BlockSpec((1, sgw), const),
            pl.BlockSpec((att, att), const),
            pl.BlockSpec((kvw, kvw), const),
        ],
        out_specs=[pl.BlockSpec((tb, s.shape[1]), row) for s in outs],
        out_shape=outs,
        compiler_params=_params("parallel"),
        name="inproj",
    )(xp, xs, g, w_in, qg, kg, lg, lb, bq, bk)


def _sink_softmax(s, sink):
    mx = jnp.maximum(jnp.max(s, axis=-1, keepdims=True), sink)
    p = jnp.exp(s - mx)
    den = jnp.sum(p, axis=-1, keepdims=True) + jnp.exp(sink - mx)
    return p / den


def _prompt_kernel(sinks_ref, q_ref, kc_ref, kp_ref, vc_ref, vp_ref, vn_ref, u_ref, w_ref, bias_ref,
                   a_ref, m_ref):
    i = pl.program_id(1)
    tq = q_ref.shape[0]
    nblk = tq // WINDOW
    q = q_ref[...]
    kc = kc_ref[...].astype(BF16)
    vc = vc_ref[...].astype(BF16)
    kp = kp_ref[...].astype(BF16)
    vp = vp_ref[...].astype(BF16)
    row = lax.broadcasted_iota(jnp.int32, (WINDOW, 2 * WINDOW), 0)
    col = lax.broadcasted_iota(jnp.int32, (WINDOW, 2 * WINDOW), 1)
    dist = row - col + WINDOW
    in_window = (dist >= 0) & (dist < WINDOW)
    distf = dist.astype(F32)
    for jq in range(nblk):
        r0 = jq * WINDOW
        if jq == 0:
            kprev, vprev = kp, vp
            valid = in_window & (col >= jnp.where(i > 0, 0, WINDOW))
        else:
            kprev, vprev = kc[r0 - WINDOW:r0], vc[r0 - WINDOW:r0]
            valid = in_window
        kcat = jnp.concatenate([kprev, kc[r0:r0 + WINDOW]], axis=0)
        vcat = jnp.concatenate([vprev, vc[r0:r0 + WINDOW]], axis=0)
        for g in range(N_KV_HEADS):
            heads = range(g * Q_GROUP, (g + 1) * Q_GROUP)
            qg = jnp.concatenate([q[r0:r0 + WINDOW, h * HEAD_DIM:(h + 1) * HEAD_DIM] for h in heads], axis=0)
            s_all = lax.dot_general(qg, kcat[:, g * HEAD_DIM:(g + 1) * HEAD_DIM],
                                    (((1,), (1,)), ((), ())), preferred_element_type=F32)
            probs = []
            for hl, h in enumerate(heads):
                s = s_all[hl * WINDOW:(hl + 1) * WINDOW] - ALIBI_SLOPES[h] * distf
                s = jnp.where(valid, s, NEG_INF)
                probs.append(_sink_softmax(s, sinks_ref[h]).astype(BF16))
            o_all = jnp.dot(jnp.concatenate(probs, axis=0), vcat[:, g * HEAD_DIM:(g + 1) * HEAD_DIM],
                            preferred_element_type=F32)
            for hl, h in enumerate(heads):
                a_ref[r0:r0 + WINDOW, h * HEAD_DIM:(h + 1) * HEAD_DIM] = (
                    o_all[hl * WINDOW:(hl + 1) * WINDOW].astype(BF16))
    tr = lax.broadcasted_iota(jnp.int32, (CHUNK, CHUNK), 0)
    tc = lax.broadcasted_iota(jnp.int32, (CHUNK, CHUNK), 1)
    gd = vn_ref.shape[1] // SGU_GROUPS
    wm = [jnp.where(tr >= tc, w_ref[g], 0.0).astype(BF16) for g in range(SGU_GROUPS)]
    for c in range(tq // CHUNK):
        r0 = c * CHUNK
        vnc = vn_ref[r0:r0 + CHUNK, :].astype(BF16)
        for g in range(SGU_GROUPS):
            s = jnp.dot(wm[g], vnc[:, g * gd:(g + 1) * gd], preferred_element_type=F32)
            s = s + bias_ref[:, g * gd:(g + 1) * gd]
            m_ref[r0:r0 + CHUNK, g * gd:(g + 1) * gd] = (
                u_ref[r0:r0 + CHUNK, g * gd:(g + 1) * gd].astype(F32) * s).astype(BF16)


def _prompt_mix(sinks, q, k, v, vn, u, sgu_w, sgu_bias, b, s):
    n, att = b * s, q.shape[1]
    kvw = k.shape[1]
    sgw = vn.shape[1]
    tq = 512 if s % 512 == 0 else WINDOW
    assert s % tq == 0 and tq % WINDOW == 0 and WINDOW == CHUNK
    r = tq // WINDOW
    nq = s // tq
    cur = lambda bi, i: (bi * nq + i, 0)
    prev = lambda bi, i: (jnp.maximum((bi * nq + i) * r - 1, 0), 0)
    outs = [jax.ShapeDtypeStruct((n, att), BF16), jax.ShapeDtypeStruct((n, sgw), BF16)]
    return pl.pallas_call(
        _prompt_kernel,
        grid=(b, nq),
        in_specs=[
            pl.BlockSpec(memory_space=pltpu.SMEM),
            pl.BlockSpec((tq, att), cur),
            pl.BlockSpec((tq, kvw), cur),
            pl.BlockSpec((WINDOW, kvw), prev),
            pl.BlockSpec((tq, kvw), cur),
            pl.BlockSpec((WINDOW, kvw), prev),
            pl.BlockSpec((tq, sgw), cur),
            pl.BlockSpec((tq, sgw), cur),
            pl.BlockSpec(sgu_w.shape, lambda bi, i: (0, 0, 0)),
            pl.BlockSpec(sgu_bias.shape, lambda bi, i: (0, 0)),
        ],
        out_specs=[pl.BlockSpec((tq, att), cur), pl.BlockSpec((tq, sgw), cur)],
        out_shape=outs,
        compiler_params=_params("parallel", "parallel"),
        name="prompt_mix",
    )(sinks, q, k, k, v, v, vn, u, sgu_w, sgu_bias)


def _sample_kernel(sinks_ref, q_ref, kn_ref, vn_new_ref, ck_ref, cv_ref, vn_ref, u_ref, wexp_ref, bias_ref,
                   a_ref, m_ref):
    bb, w, _ = ck_ref.shape
    l = q_ref.shape[0] // bb
    per_seq = lambda ref: ref[...].astype(F32).reshape(bb, l, ref.shape[1])
    q = per_seq(q_ref)
    kcat = jnp.concatenate([ck_ref[...], per_seq(kn_ref)], axis=1).astype(BF16)
    vcat = jnp.concatenate([cv_ref[...], per_seq(vn_new_ref)], axis=1).astype(BF16)
    rows = Q_GROUP * l
    t = lax.broadcasted_iota(jnp.int32, (rows, w + l), 0) % l
    key = lax.broadcasted_iota(jnp.int32, (rows, w + l), 1)
    dist = t - (key - w)
    valid = (dist >= 0) & (dist < WINDOW)
    distf = dist.astype(F32)
    hl_of_row = lax.broadcasted_iota(jnp.int32, (rows, 1), 0) // l
    for g in range(N_KV_HEADS):
        heads = range(g * Q_GROUP, (g + 1) * Q_GROUP)
        qg = jnp.concatenate([q[:, :, h * HEAD_DIM:(h + 1) * HEAD_DIM] for h in heads], axis=1)
        s = jnp.einsum('bqd,bkd->bqk', qg.astype(BF16), kcat[:, :, g * HEAD_DIM:(g + 1) * HEAD_DIM],
                       preferred_element_type=F32)
        slope = jnp.zeros((rows, 1), F32)
        sink = jnp.zeros((rows, 1), F32)
        for hl, h in enumerate(heads):
            slope = jnp.where(hl_of_row == hl, ALIBI_SLOPES[h], slope)
            sink = jnp.where(hl_of_row == hl, sinks_ref[h], sink)
        s = jnp.where(valid[None], s - (slope * distf)[None], NEG_INF)
        p = _sink_softmax(s, sink[None]).astype(BF16)
        o = jnp.einsum('bqk,bkd->bqd', p, vcat[:, :, g * HEAD_DIM:(g + 1) * HEAD_DIM],
                       preferred_element_type=F32)
        for hl, h in enumerate(heads):
            a_ref[:, h * HEAD_DIM:(h + 1) * HEAD_DIM] = (
                o[:, hl * l:(hl + 1) * l, :].reshape(bb * l, HEAD_DIM).astype(BF16))
    vn = per_seq(vn_ref)
    tt = lax.broadcasted_iota(jnp.int32, (l, vn.shape[2]), 0)
    s = jnp.broadcast_to(bias_ref[...][None], vn.shape)
    for sp in range(l):
        wm = jnp.where(tt >= sp, wexp_ref[sp], 0.0)
        s = s + wm[None] * vn[:, sp:sp + 1, :]
    m_ref[...] = (per_seq(u_ref) * s).reshape(bb * l, vn.shape[2]).astype(BF16)


def _sample_mix(sinks, q, k, v, cache_k, cache_v, vn, u, wexp, bias, n_p):
    b, w, kvw = cache_k.shape
    n, att = q.shape
    l = (n - n_p) // b
    sgw = vn.shape[1]
    bb = 16 if b % 16 == 0 else b
    assert n_p % (bb * l) == 0
    first = n_p // (bb * l)
    tok = lambda width: pl.BlockSpec((bb * l, width), lambda i: (first + i, 0))
    own = lambda width: pl.BlockSpec((bb * l, width), lambda i: (i, 0))
    past = pl.BlockSpec((bb, w, kvw), lambda i: (i, 0, 0))
    outs = [jax.ShapeDtypeStruct((n - n_p, att), BF16), jax.ShapeDtypeStruct((n - n_p, sgw), BF16)]
    return pl.pallas_call(
        _sample_kernel,
        grid=(b // bb,),
        in_specs=[
            pl.BlockSpec(memory_space=pltpu.SMEM),
            tok(att), tok(kvw), tok(kvw), past, past, tok(sgw), tok(sgw),
            pl.BlockSpec(wexp.shape, lambda i: (0, 0, 0)),
            pl.BlockSpec(bias.shape, lambda i: (0, 0)),
        ],
        out_specs=[own(att), own(sgw)],
        out_shape=outs,
        compiler_params=_params("parallel"),
        name="sample_mix",
    )(sinks, q, k, v, cache_k, cache_v, vn, u, wexp, bias)


def _merge_kernel(x_ref, a_ref, m_ref, ga_ref, gb_ref, wa_ref, wb_ref, wo_ref, fg_ref, wq_ref, keys_ref,
                  x1_ref, xnt_ref, s1_ref, s2_ref):
    ha = jnp.dot(a_ref[...], wa_ref[...], preferred_element_type=F32)
    hb = jnp.dot(m_ref[...], wb_ref[...], preferred_element_type=F32)
    h = ga_ref[...].astype(F32) * ha + gb_ref[...].astype(F32) * hb
    x1 = x_ref[...] + jnp.dot(h.astype(BF16), wo_ref[...], preferred_element_type=F32)
    x1_ref[...] = x1
    xn32 = _rms(x1, fg_ref[...])
    xnt_ref[...] = xn32.T.astype(BF16)
    xn = xn32.astype(BF16)
    qp = jnp.dot(xn, wq_ref[...], preferred_element_type=F32).astype(BF16)
    half = keys_ref.shape[2]
    nsub = s1_ref.shape[0]
    for hc in range(keys_ref.shape[0]):
        st = lax.dot_general(keys_ref[hc], qp[:, hc * half:(hc + 1) * half],
                             (((1,), (1,)), ((), ())), preferred_element_type=F32)
        dst = s1_ref if hc % 2 == 0 else s2_ref
        for tl in range(nsub):
            dst[tl, hc // 2] = st[:, tl * LANES:(tl + 1) * LANES]


def _merge(xp, xs, ap, a_s, mp, ms, ga, gb, wa, wb, wo, fg, wq, keys):
    (n_p, d), n_s = xp.shape, xs.shape[0]
    n = n_p + n_s
    tb = _token_block(n_p, n_s)
    nbp = n_p // tb
    nsub = tb // LANES
    hc, nk, half = keys.shape
    row = lambda i: (i, 0)
    const2 = lambda i: (0, 0)
    heads = hc // 2
    outs = [
        jax.ShapeDtypeStruct((n, d), F32),
        jax.ShapeDtypeStruct((d, n), BF16),
        jax.ShapeDtypeStruct((n // LANES, heads, nk, LANES), F32),
        jax.ShapeDtypeStruct((n // LANES, heads, nk, LANES), F32),
    ]

    def body(xp_ref, xs_ref, ap_ref, as_ref, mp_ref, ms_ref, *rest):
        side = lambda p_ref, s_ref: _Value(jnp.where(pl.program_id(0) < nbp, p_ref[...], s_ref[...]))
        _merge_kernel(side(xp_ref, xs_ref), side(ap_ref, as_ref), side(mp_ref, ms_ref), *rest)

    prompt = lambda width: pl.BlockSpec((tb, width), lambda i: (jnp.minimum(i, nbp - 1), 0))
    sample = lambda width: pl.BlockSpec((tb, width), lambda i: (jnp.maximum(i - nbp, 0), 0))
    return pl.pallas_call(
        body,
        grid=(n // tb,),
        in_specs=[
            prompt(d), sample(d),
            prompt(ap.shape[1]), sample(ap.shape[1]),
            prompt(mp.shape[1]), sample(mp.shape[1]),
            pl.BlockSpec((tb, d), row),
            pl.BlockSpec((tb, d), row),
            pl.BlockSpec(wa.shape, const2),
            pl.BlockSpec(wb.shape, const2),
            pl.BlockSpec(wo.shape, const2),
            pl.BlockSpec((1, d), const2),
            pl.BlockSpec(wq.shape, const2),
            pl.BlockSpec(keys.shape, lambda i: (0, 0, 0)),
        ],
        out_specs=[
            pl.BlockSpec((tb, d), row),
            pl.BlockSpec((d, tb), lambda i: (0, i)),
            pl.BlockSpec((nsub, heads, nk, LANES), lambda i: (i, 0, 0, 0)),
            pl.BlockSpec((nsub, heads, nk, LANES), lambda i: (i, 0, 0, 0)),
        ],
        out_shape=outs,
        compiler_params=_params("parallel"),
        name="merge",
    )(xp, xs, ap, a_s, mp, ms, ga, gb, wa, wb, wo, fg, wq, keys)


class _Value:
    def __init__(self, value):
        self._value = value

    def __getitem__(self, idx):
        return self._value[idx]


def _oddeven_merge(lo, hi, r):
    step = r * 2
    if step < hi - lo:
        yield from _oddeven_merge(lo, hi, step)
        yield from _oddeven_merge(lo + r, hi, step)
        yield from [(i, i + r) for i in range(lo + r, hi - r, step)]
    else:
        yield (lo, lo + r)


def _oddeven_merge_sort(lo, hi):
    if hi - lo >= 1:
        mid = lo + (hi - lo) // 2
        yield from _oddeven_merge_sort(lo, mid)
        yield from _oddeven_merge_sort(mid + 1, hi)
        yield from _oddeven_merge(lo, hi, 1)


_SORT_TOPK = tuple(_oddeven_merge_sort(0, PEER_TOPK - 1))


def _cmpx(w, i, j):
    a, b = w[i], w[j]
    if b is None:
        return
    if a is None:
        w[i], w[j] = b, None
        return
    w[i], w[j] = jnp.maximum(a, b), jnp.minimum(a, b)


def _top_values(w):
    k = PEER_TOPK
    w = list(w)
    for i, j in _SORT_TOPK:
        _cmpx(w, i, j)
    shift = SUBLANES // 2
    while shift >= 1:
        y = [None if v is None else pltpu.roll(v, shift, 0) for v in w]
        z = []
        for r in range(k):
            a, b = w[r], y[k - 1 - r]
            z.append(b if a is None else a if b is None else jnp.maximum(a, b))
        stride = k // 2
        while stride >= 1:
            for i in range(k):
                if i & stride == 0:
                    _cmpx(z, i, i + stride)
            stride //= 2
        w = z
        shift //= 2
    return w


def _thresh_kernel(s1_ref, s2_ref, g2_ref, gm_ref, g1_ref):
    k = PEER_TOPK
    nk, lanes = s1_ref.shape[2], s1_ref.shape[3]
    nslot = nk // SUBLANES
    assert nslot == k and k == 2 * SUBLANES
    sub = lax.broadcasted_iota(jnp.int32, (SUBLANES, lanes), 0)

    def pack(vals):
        out = vals[0]
        for j in range(1, SUBLANES):
            out = jnp.where(sub == j, vals[j], out)
        return out

    def head(h, carry):
        w1 = [s1_ref[0, h, r * SUBLANES:(r + 1) * SUBLANES, :] for r in range(nslot)]
        w2 = [s2_ref[0, h, r * SUBLANES:(r + 1) * SUBLANES, :] for r in range(nslot)]
        a = _top_values(w1)
        b = _top_values(w2)
        b_lo, b_hi, a_hi = pack(b[:SUBLANES]), pack(b[SUBLANES:]), pack(a[SUBLANES:])
        cands = ([a[0] + b_lo, a[0] + b_hi] + [a[i] + b_lo for i in range(1, SUBLANES)] + [a_hi + b[0]])
        best = _top_values(cands + [None] * (k - len(cands)))
        tau = best[k - 1]
        z = jnp.ones_like(tau)
        for r in range(1, k):
            z = z + jnp.exp(best[r] - best[0])
        inv_z = 1.0 / z
        eb = [jnp.exp(b[j] - b[0]) for j in range(k)]
        gamma = []
        for i in range(k):
            t = jnp.full_like(tau, jnp.inf)
            for j in range(k // (i + 1)):
                t = jnp.where(a[i] + b[j] >= tau, eb[j], t)
            gamma.append(t)
        for r in range(nslot):
            gm = jnp.full_like(tau, jnp.inf)
            for i in range(k - 1, -1, -1):
                gm = jnp.where(w1[r] >= a[i], gamma[i], gm)
            rows = pl.ds(r * SUBLANES, SUBLANES)
            gm_ref[0, h, rows, :] = gm
            g1_ref[0, h, rows, :] = jnp.exp(w1[r] - a[0]) * (0.5 * inv_z)
            g2_ref[0, h, rows, :] = jnp.exp(w2[r] - b[0])
        return carry

    lax.fori_loop(0, s1_ref.shape[1], head, 0)


def _thresholds(s1, s2):
    nsub, heads, nk, lanes = s1.shape
    spec = pl.BlockSpec((1, heads, nk, lanes), lambda i: (i, 0, 0, 0))
    out = jax.ShapeDtypeStruct(s1.shape, F32)
    return pl.pallas_call(
        _thresh_kernel,
        grid=(nsub,),
        in_specs=[spec, spec],
        out_specs=[spec, spec, spec],
        out_shape=[out, out, out],
        compiler_params=_params("parallel"),
        name="peer_thresholds",
    )(s1, s2)


GATE_ROWS = 128
GELU_C0 = math.sqrt(2.0 / math.pi)
GELU_C1 = 0.044715 * GELU_C0
PEER_CHUNK = 512
PEER_TOKENS = 1024


def _gate_columns(at_ref, wt_ref, j0, tls, g2_ref, gm_ref, g1_ref):
    _, heads, nk, lanes = g2_ref.shape

    def tile(j, tl, r0):
        gate = jnp.zeros((GATE_ROWS, lanes), BF16)
        for h in range(heads):
            gm = gm_ref[tl, h, j0 + j:j0 + j + 1, :]
            g1 = jnp.broadcast_to(g1_ref[tl, h, j0 + j:j0 + j + 1, :], (GATE_ROWS, lanes)).astype(BF16)
            g2 = g2_ref[tl, h, r0:r0 + GATE_ROWS, :]
            gate = gate + g1 * jnp.where(g2 >= gm, g2, 0.0).astype(BF16)
        rows = slice(j * nk + r0, j * nk + r0 + GATE_ROWS)
        cols = slice(tl * lanes, (tl + 1) * lanes)
        x = at_ref[rows, cols].astype(BF16)
        t = jnp.tanh(x * (GELU_C0 + GELU_C1 * (x * x)))
        wt_ref[rows, cols] = (x + x * t) * gate

    for tl in tls:
        for j in range(at_ref.shape[0] // nk):
            for r0 in range(0, nk, GATE_ROWS):
                tile(j, tl, r0)


def _peer_kernel(xnt_ref, u0a_ref, u0b_ref, una_ref, unb_ref, va_ref, vb_ref, g2_ref, gm_ref, g1_ref, x1_ref, out_ref,
                 at_a, at_b, wt_a, wt_b, acc, xnt):
    s = pl.program_id(1)
    per = at_a.shape[0] // g2_ref.shape[2]
    tb = xnt_ref.shape[1]
    lanes = g2_ref.shape[3]
    tables = (g2_ref, gm_ref, g1_ref)
    cw = min(MXU_COLS, tb)
    columns = [(slice(c, c + cw), range(c // lanes, (c + cw) // lanes)) for c in range(0, tb, cw)]

    def act(u_ref, at, cs):
        at[:, cs] = jnp.dot(u_ref[...], xnt[:, cs], preferred_element_type=F32)

    def mix(v_ref, wt, cs):
        acc[:, cs] += jnp.dot(v_ref[...], wt[:, cs], preferred_element_type=F32)

    first = (s == 0) & (pl.program_id(0) == 0)

    @pl.when(first | (s == pl.num_programs(1) - 1))
    def _():
        xnt[...] = xnt_ref[...]

    @pl.when(first)
    def _():
        for cs, _ in columns:
            act(u0a_ref, at_a, cs)
            act(u0b_ref, at_b, cs)

    @pl.when(s == 0)
    def _():
        acc[...] = jnp.zeros_like(acc)

    for at, wt, j0, v_ref, un_ref in ((at_a, wt_a, 0, va_ref, una_ref), (at_b, wt_b, per, vb_ref, unb_ref)):
        for cs, tls in columns:
            _gate_columns(at, wt, j0, tls, *tables)
            mix(v_ref, wt, cs)
            act(un_ref, at, cs)

    @pl.when(s == pl.num_programs(1) - 1)
    def _():
        out_ref[...] = x1_ref[...] + acc[...].T


def _peer(xnt, u, vt, g2, gm, g1, x1):
    d, n = xnt.shape
    ne = u.shape[0]
    tb = PEER_TOKENS if n % PEER_TOKENS == 0 else _token_block(n)
    nsub = tb // LANES
    _, heads, nk, _ = g2.shape
    ec = PEER_CHUNK
    assert ne % (2 * ec) == 0 and ec % nk == 0 and ne == nk * nk and nk % GATE_ROWS == 0
    nc = ne // ec
    per = ec // nk
    keyed2 = pl.BlockSpec((nsub, heads, nk, LANES), lambda i, s: (i, 0, 0, 0))
    keyed1 = pl.BlockSpec((nsub, heads, 2 * per, LANES), lambda i, s: (i, 0, s, 0))
    steps, nb = nc // 2, n // tb
    ahead = lambda i, s: jnp.minimum(i + (s == steps - 1).astype(jnp.int32), nb - 1)
    return pl.pallas_call(
        _peer_kernel,
        grid=(nb, steps),
        in_specs=[
            pl.BlockSpec((d, tb), lambda i, s: (0, ahead(i, s))),
            pl.BlockSpec((ec, d), lambda i, s: (0, 0)),
            pl.BlockSpec((ec, d), lambda i, s: (1, 0)),
            pl.BlockSpec((ec, d), lambda i, s: ((2 * s + 2) % nc, 0)),
            pl.BlockSpec((ec, d), lambda i, s: ((2 * s + 3) % nc, 0)),
            pl.BlockSpec((d, ec), lambda i, s: (0, 2 * s)),
            pl.BlockSpec((d, ec), lambda i, s: (0, 2 * s + 1)),
            keyed2, keyed1, keyed1,
            pl.BlockSpec((tb, d), lambda i, s: (i, 0)),
        ],
        out_specs=pl.BlockSpec((tb, d), lambda i, s: (i, 0)),
        out_shape=jax.ShapeDtypeStruct((n, d), F32),
        scratch_shapes=[
            pltpu.VMEM((ec, tb), F32),
            pltpu.VMEM((ec, tb), F32),
            pltpu.VMEM((ec, tb), BF16),
            pltpu.VMEM((ec, tb), BF16),
            pltpu.VMEM((d, tb), F32),
            pltpu.VMEM((d, tb), BF16),
        ],
        compiler_params=_params("arbitrary", "arbitrary"),
        name="peer_mix",
    )(xnt, u, u, u, u, vt, vt, g2, gm, g1, x1)


def _final_kernel(x2_ref, p_ref, g_ref, wg_ref, wp_ref, y_ref):
    x2 = x2_ref[...]
    xn = _rms(x2, g_ref[...]).astype(BF16)
    gate = jax.nn.sigmoid(jnp.dot(xn, wg_ref[...], preferred_element_type=F32))
    y_ref[...] = x2 + gate * jnp.dot(p_ref[...].astype(BF16), wp_ref[...], preferred_element_type=F32)


def _final(x2, p, g, wg, wp, row0):
    n, pd = p.shape
    d = x2.shape[1]
    tb = _token_block(n, row0)
    first = row0 // tb
    src = lambda i: (first + i, 0)
    row = lambda i: (i, 0)
    const = lambda i: (0, 0)
    return pl.pallas_call(
        _final_kernel,
        grid=(n // tb,),
        in_specs=[
            pl.BlockSpec((tb, d), src),
            pl.BlockSpec((tb, pd), row),
            pl.BlockSpec((1, d), const),
            pl.BlockSpec(wg.shape, const),
            pl.BlockSpec(wp.shape, const),
        ],
        out_specs=pl.BlockSpec((tb, d), row),
        out_shape=jax.ShapeDtypeStruct((n, d), F32),
        compiler_params=_params("parallel"),
        name="ple_epilogue",
    )(x2, p, g, wg, wp)


def _layer(xp, xs, pp, ps, past_k, past_v, lp):
    b, s, d = xp.shape
    bd, l, _ = xs.shape
    n_p, n_s = b * s, bd * l
    kvw = N_KV_HEADS * HEAD_DIM
    sgw = d // 2
    gd = sgw // SGU_GROUPS
    xp2, xs2 = xp.reshape(n_p, d), xs.reshape(n_s, d)

    q, k, v, u, vn, ga, gb = _inproj(
        xp2, xs2, lp['attn_norm_g'][None], lp['w_in'].astype(BF16),
        jnp.tile(lp['q_norm_g'], N_HEADS)[None], jnp.tile(lp['k_norm_g'], N_KV_HEADS)[None],
        lp['sgu_norm_g'][None], lp['sgu_norm_b'][None])

    sgu_w, sgu_b = lp['sgu_w'], lp['sgu_b']
    bias_p = jnp.repeat(sgu_b.T, gd, axis=1)
    a_p, m_p = _prompt_mix(lp['attn_sinks'], q, k, v, vn, u, sgu_w, bias_p, b, s)
    wexp = jnp.repeat(jnp.transpose(sgu_w[:, :l, :l], (2, 1, 0)), gd, axis=2)
    a_s, m_s = _sample_mix(lp['attn_sinks'], q, k, v, past_k.reshape(bd, -1, kvw),
                           past_v.reshape(bd, -1, kvw), vn, u, wexp, bias_p[:l], n_p)

    keys = lp['peer_sub_keys'].reshape(2 * PEER_HEADS, PEER_N_KEYS, -1).astype(BF16)
    x1, xn1, s1, s2 = _merge(xp2, xs2, a_p, a_s, m_p, m_s, ga, gb, lp['w_branch_a'].astype(BF16),
                             lp['w_branch_b'].astype(BF16), lp['w_out'].astype(BF16), lp['ffn_norm_g'][None],
                             lp['peer_w_q'].astype(BF16), keys)
    g2, gm, g1 = _thresholds(s1, s2)
    x2 = _peer(xn1, lp['peer_u'].astype(BF16), lp['peer_v'].astype(BF16).T, g2, gm, g1, x1)
    ple = (lp['ple_norm_g'][None], lp['w_ple_gate'].astype(BF16), lp['w_ple'].astype(BF16))
    y_p = _final(x2, pp.reshape(n_p, -1), *ple, 0)
    y_s = _final(x2, ps.reshape(n_s, -1), *ple, n_p)

    wp = min(WINDOW, s)
    tail = lambda t, rows: jnp.stack([t[(bi + 1) * s - rows:(bi + 1) * s] for bi in range(b)])
    heads = lambda t: t.reshape(t.shape[0], t.shape[1], N_KV_HEADS, HEAD_DIM)
    return (y_p.reshape(b, s, d), y_s.reshape(bd, l, d),
            heads(tail(k, wp)), heads(tail(v, wp)),
            heads(k[n_p:].reshape(bd, l, kvw)), heads(v[n_p:].reshape(bd, l, kvw)),
            tail(vn, CHUNK), vn[n_p:].reshape(bd, l, sgw))


def kernel(x_prompt, x_sample, cache_k, cache_v, p_prompt, p_sample, attn_norm_g, w_in, q_norm_g, k_norm_g, attn_sinks, sgu_norm_g, sgu_norm_b, sgu_w, sgu_b, w_branch_a, w_branch_b, w_out, ffn_norm_g, peer_w_q, peer_sub_keys, peer_u, peer_v, ple_norm_g, w_ple, w_ple_gate):
    depth = w_in.shape[0]
    hp, hs = x_prompt, x_sample
    outs = [[] for _ in range(6)]
    for i in range(depth):
        lp = dict(attn_norm_g=attn_norm_g[i], w_in=w_in[i], q_norm_g=q_norm_g[i], k_norm_g=k_norm_g[i],
                  attn_sinks=attn_sinks[i], sgu_norm_g=sgu_norm_g[i], sgu_norm_b=sgu_norm_b[i],
                  sgu_w=sgu_w[i], sgu_b=sgu_b[i], w_branch_a=w_branch_a[i], w_branch_b=w_branch_b[i],
                  w_out=w_out[i], ffn_norm_g=ffn_norm_g[i], peer_w_q=peer_w_q[i],
                  peer_sub_keys=peer_sub_keys[i], peer_u=peer_u[i], peer_v=peer_v[i],
                  ple_norm_g=ple_norm_g[i], w_ple=w_ple[i], w_ple_gate=w_ple_gate[i])
        res = _layer(hp, hs, p_prompt[i], p_sample[i], cache_k[i], cache_v[i], lp)
        hp, hs = res[0], res[1]
        for lst, t in zip(outs, res[2:]):
            lst.append(t)
    return (hp, hs) + tuple(jnp.stack(o) for o in outs)
```

```python
import functools
import math

import jax
import jax.numpy as jnp
from jax import lax
from jax.experimental import pallas as pl
from jax.experimental.pallas import tpu as pltpu

F32 = jnp.float32
BF16 = jnp.bfloat16

N_HEADS = 8
N_KV_HEADS = 2
HEAD_DIM = 64
Q_GROUP = N_HEADS // N_KV_HEADS
WINDOW = 128
CHUNK = 128
SGU_GROUPS = 4
PEER_HEADS = 8
PEER_N_KEYS = 128
PEER_TOPK = 16
EPS = 1e-6
NEG_INF = -1e30
ALIBI_SLOPES = tuple(2.0 ** (-8.0 * h / N_HEADS) for h in range(1, N_HEADS + 1))

LANES = 128
SUBLANES = 8
MXU_COLS = 256
VMEM_LIMIT = 56 * 1024 * 1024


def _params(*semantics):
    return pltpu.CompilerParams(dimension_semantics=semantics, vmem_limit_bytes=VMEM_LIMIT)


def _token_block(*counts):
    for tb in (512, 256, 128):
        if all(n % tb == 0 for n in counts):
            return tb
    raise ValueError(f"token counts {counts} must be multiples of 128")


def _rms(x, g):
    return x * lax.rsqrt(jnp.mean(x * x, axis=-1, keepdims=True) + EPS) * g


def _group_rms(t, ones_blk, g):
    t2 = t * t
    hi = t2.astype(BF16)
    lo = (t2 - hi.astype(F32)).astype(BF16)
    ss = (jnp.dot(hi, ones_blk, preferred_element_type=F32)
          + jnp.dot(lo, ones_blk, preferred_element_type=F32))
    return t * lax.rsqrt(ss * (1.0 / HEAD_DIM) + EPS) * g


def _inproj_kernel(xp_ref, xs_ref, g_ref, w_ref, qg_ref, kg_ref, lg_ref, lb_ref, bq_ref, bk_ref,
                   q_ref, k_ref, v_ref, u_ref, vn_ref, ga_ref, gb_ref, *, prompt_blocks):
    x = jnp.where(pl.program_id(0) < prompt_blocks, xp_ref[...], xs_ref[...])
    xn = _rms(x, g_ref[...])
    z = jnp.dot(xn.astype(BF16), w_ref[...], preferred_element_type=F32)
    att = N_HEADS * HEAD_DIM
    kvw = N_KV_HEADS * HEAD_DIM
    sgw = (z.shape[1] - att - 2 * kvw) // 6
    o = 0
    q = z[:, o:o + att]; o += att
    k = z[:, o:o + kvw]; o += kvw
    v = z[:, o:o + kvw]; o += kvw
    su = z[:, o:o + sgw]; o += sgw
    sv = z[:, o:o + sgw]; o += sgw
    g_a = z[:, o:o + 2 * sgw]; o += 2 * sgw
    g_b = z[:, o:o + 2 * sgw]
    qn = _group_rms(q, bq_ref[...], qg_ref[...])
    q_ref[...] = (qn * (HEAD_DIM ** -0.5)).astype(BF16)
    k_ref[...] = _group_rms(k, bk_ref[...], kg_ref[...])
    v_ref[...] = v
    u_ref[...] = jax.nn.gelu(su).astype(BF16)
    gv = jax.nn.gelu(sv)
    mu = jnp.mean(gv, axis=-1, keepdims=True)
    gc = gv - mu
    vn_ref[...] = gc * lax.rsqrt(jnp.mean(gc * gc, axis=-1, keepdims=True) + EPS) * lg_ref[...] + lb_ref[...]
    ga_ref[...] = jax.nn.sigmoid(g_a).astype(BF16)
    gb_ref[...] = jax.nn.sigmoid(g_b).astype(BF16)


def _inproj(xp, xs, g, w_in, qg, kg, lg, lb):
    (n_p, d), n_s = xp.shape, xs.shape[0]
    n = n_p + n_s
    tb = _token_block(n_p, n_s)
    nbp = n_p // tb
    att = N_HEADS * HEAD_DIM
    kvw = N_KV_HEADS * HEAD_DIM
    sgw = d // 2
    hid = jnp.arange(att) // HEAD_DIM
    bq = (hid[:, None] == hid[None, :]).astype(BF16)
    bk = bq[:kvw, :kvw]
    const = lambda i: (0, 0)
    row = lambda i: (i, 0)
    outs = [
        jax.ShapeDtypeStruct((n, att), BF16),
        jax.ShapeDtypeStruct((n, kvw), F32),
        jax.ShapeDtypeStruct((n, kvw), F32),
        jax.ShapeDtypeStruct((n, sgw), BF16),
        jax.ShapeDtypeStruct((n, sgw), F32),
        jax.ShapeDtypeStruct((n, d), BF16),
        jax.ShapeDtypeStruct((n, d), BF16),
    ]
    return pl.pallas_call(
        functools.partial(_inproj_kernel, prompt_blocks=nbp),
        grid=(n // tb,),
        in_specs=[
            pl.BlockSpec((tb, d), lambda i: (jnp.minimum(i, nbp - 1), 0)),
            pl.BlockSpec((tb, d), lambda i: (jnp.maximum(i - nbp, 0), 0)),
            pl.BlockSpec((1, d), const),
            pl.BlockSpec(w_in.shape, const),
            pl.BlockSpec((1, att), const),
            pl.BlockSpec((1, kvw), const),
            pl.BlockSpec((1, sgw), const),
            pl.BlockSpec((1, sgw), const),
            pl.BlockSpec((att, att), const),
            pl.BlockSpec((kvw, kvw), const),
        ],
        out_specs=[pl.BlockSpec((tb, s.shape[1]), row) for s in outs],
        out_shape=outs,
        compiler_params=_params("parallel"),
        name="inproj",
    )(xp, xs, g, w_in, qg, kg, lg, lb, bq, bk)


def _sink_softmax(s, sink):
    mx = jnp.maximum(jnp.max(s, axis=-1, keepdims=True), sink)
    p = jnp.exp(s - mx)
    den = jnp.sum(p, axis=-1, keepdims=True) + jnp.exp(sink - mx)
    return p / den


def _prompt_kernel(sinks_ref, q_ref, kc_ref, kp_ref, vc_ref, vp_ref, vn_ref, u_ref, w_ref, bias_ref,
                   a_ref, m_ref):
    i = pl.program_id(1)
    tq = q_ref.shape[0]
    nblk = tq // WINDOW
    q = q_ref[...]
    kc = kc_ref[...].astype(BF16)
    vc = vc_ref[...].astype(BF16)
    kp = kp_ref[...].astype(BF16)
    vp = vp_ref[...].astype(BF16)
    row = lax.broadcasted_iota(jnp.int32, (WINDOW, 2 * WINDOW), 0)
    col = lax.broadcasted_iota(jnp.int32, (WINDOW, 2 * WINDOW), 1)
    dist = row - col + WINDOW
    in_window = (dist >= 0) & (dist < WINDOW)
    distf = dist.astype(F32)
    for jq in range(nblk):
        r0 = jq * WINDOW
        if jq == 0:
            kprev, vprev = kp, vp
            valid = in_window & (col >= jnp.where(i > 0, 0, WINDOW))
        else:
            kprev, vprev = kc[r0 - WINDOW:r0], vc[r0 - WINDOW:r0]
            valid = in_window
        kcat = jnp.concatenate([kprev, kc[r0:r0 + WINDOW]], axis=0)
        vcat = jnp.concatenate([vprev, vc[r0:r0 + WINDOW]], axis=0)
        for g in range(N_KV_HEADS):
            heads = range(g * Q_GROUP, (g + 1) * Q_GROUP)
            qg = jnp.concatenate([q[r0:r0 + WINDOW, h * HEAD_DIM:(h + 1) * HEAD_DIM] for h in heads], axis=0)
            s_all = lax.dot_general(qg, kcat[:, g * HEAD_DIM:(g + 1) * HEAD_DIM],
                                    (((1,), (1,)), ((), ())), preferred_element_type=F32)
            probs = []
            for hl, h in enumerate(heads):
                s = s_all[hl * WINDOW:(hl + 1) * WINDOW] - ALIBI_SLOPES[h] * distf
                s = jnp.where(valid, s, NEG_INF)
                probs.append(_sink_softmax(s, sinks_ref[h]).astype(BF16))
            o_all = jnp.dot(jnp.concatenate(probs, axis=0), vcat[:, g * HEAD_DIM:(g + 1) * HEAD_DIM],
                            preferred_element_type=F32)
            for hl, h in enumerate(heads):
                a_ref[r0:r0 + WINDOW, h * HEAD_DIM:(h + 1) * HEAD_DIM] = (
                    o_all[hl * WINDOW:(hl + 1) * WINDOW].astype(BF16))
    tr = lax.broadcasted_iota(jnp.int32, (CHUNK, CHUNK), 0)
    tc = lax.broadcasted_iota(jnp.int32, (CHUNK, CHUNK), 1)
    gd = vn_ref.shape[1] // SGU_GROUPS
    wm = [jnp.where(tr >= tc, w_ref[g], 0.0).astype(BF16) for g in range(SGU_GROUPS)]
    for c in range(tq // CHUNK):
        r0 = c * CHUNK
        vnc = vn_ref[r0:r0 + CHUNK, :].astype(BF16)
        for g in range(SGU_GROUPS):
            s = jnp.dot(wm[g], vnc[:, g * gd:(g + 1) * gd], preferred_element_type=F32)
            s = s + bias_ref[:, g * gd:(g + 1) * gd]
            m_ref[r0:r0 + CHUNK, g * gd:(g + 1) * gd] = (
                u_ref[r0:r0 + CHUNK, g * gd:(g + 1) * gd].astype(F32) * s).astype(BF16)


def _prompt_mix(sinks, q, k, v, vn, u, sgu_w, sgu_bias, b, s):
    n, att = b * s, q.shape[1]
    kvw = k.shape[1]
    sgw = vn.shape[1]
    tq = 512 if s % 512 == 0 else WINDOW
    assert s % tq == 0 and tq % WINDOW == 0 and WINDOW == CHUNK
    r = tq // WINDOW
    nq = s // tq
    cur = lambda bi, i: (bi * nq + i, 0)
    prev = lambda bi, i: (jnp.maximum((bi * nq + i) * r - 1, 0), 0)
    outs = [jax.ShapeDtypeStruct((n, att), BF16), jax.ShapeDtypeStruct((n, sgw), BF16)]
    return pl.pallas_call(
        _prompt_kernel,
        grid=(b, nq),
        in_specs=[
            pl.BlockSpec(memory_space=pltpu.SMEM),
            pl.BlockSpec((tq, att), cur),
            pl.BlockSpec((tq, kvw), cur),
            pl.BlockSpec((WINDOW, kvw), prev),
            pl.BlockSpec((tq, kvw), cur),
            pl.BlockSpec((WINDOW, kvw), prev),
            pl.BlockSpec((tq, sgw), cur),
            pl.BlockSpec((tq, sgw), cur),
            pl.BlockSpec(sgu_w.shape, lambda bi, i: (0, 0, 0)),
            pl.BlockSpec(sgu_bias.shape, lambda bi, i: (0, 0)),
        ],
        out_specs=[pl.BlockSpec((tq, att), cur), pl.BlockSpec((tq, sgw), cur)],
        out_shape=outs,
        compiler_params=_params("parallel", "parallel"),
        name="prompt_mix",
    )(sinks, q, k, k, v, v, vn, u, sgu_w, sgu_bias)


def _sample_kernel(sinks_ref, q_ref, kn_ref, vn_new_ref, ck_ref, cv_ref, vn_ref, u_ref, wexp_ref, bias_ref,
                   a_ref, m_ref):
    bb, w, _ = ck_ref.shape
    l = q_ref.shape[0] // bb
    per_seq = lambda ref: ref[...].astype(F32).reshape(bb, l, ref.shape[1])
    q = per_seq(q_ref)
    kcat = jnp.concatenate([ck_ref[...], per_seq(kn_ref)], axis=1).astype(BF16)
    vcat = jnp.concatenate([cv_ref[...], per_seq(vn_new_ref)], axis=1).astype(BF16)
    rows = Q_GROUP * l
    t = lax.broadcasted_iota(jnp.int32, (rows, w + l), 0) % l
    key = lax.broadcasted_iota(jnp.int32, (rows, w + l), 1)
    dist = t - (key - w)
    valid = (dist >= 0) & (dist < WINDOW)
    distf = dist.astype(F32)
    hl_of_row = lax.broadcasted_iota(jnp.int32, (rows, 1), 0) // l
    for g in range(N_KV_HEADS):
        heads = range(g * Q_GROUP, (g + 1) * Q_GROUP)
        qg = jnp.concatenate([q[:, :, h * HEAD_DIM:(h + 1) * HEAD_DIM] for h in heads], axis=1)
        s = jnp.einsum('bqd,bkd->bqk', qg.astype(BF16), kcat[:, :, g * HEAD_DIM:(g + 1) * HEAD_DIM],
                       preferred_element_type=F32)
        slope = jnp.zeros((rows, 1), F32)
        sink = jnp.zeros((rows, 1), F32)
        for hl, h in enumerate(heads):
            slope = jnp.where(hl_of_row == hl, ALIBI_SLOPES[h], slope)
            sink = jnp.where(hl_of_row == hl, sinks_ref[h], sink)
        s = jnp.where(valid[None], s - (slope * distf)[None], NEG_INF)
        p = _sink_softmax(s, sink[None]).astype(BF16)
        o = jnp.einsum('bqk,bkd->bqd', p, vcat[:, :, g * HEAD_DIM:(g + 1) * HEAD_DIM],
                       preferred_element_type=F32)
        for hl, h in enumerate(heads):
            a_ref[:, h * HEAD_DIM:(h + 1) * HEAD_DIM] = (
                o[:, hl * l:(hl + 1) * l, :].reshape(bb * l, HEAD_DIM).astype(BF16))
    vn = per_seq(vn_ref)
    tt = lax.broadcasted_iota(jnp.int32, (l, vn.shape[2]), 0)
    s = jnp.broadcast_to(bias_ref[...][None], vn.shape)
    for sp in range(l):
        wm = jnp.where(tt >= sp, wexp_ref[sp], 0.0)
        s = s + wm[None] * vn[:, sp:sp + 1, :]
    m_ref[...] = (per_seq(u_ref) * s).reshape(bb * l, vn.shape[2]).astype(BF16)


def _sample_mix(sinks, q, k, v, cache_k, cache_v, vn, u, wexp, bias, n_p):
    b, w, kvw = cache_k.shape
    n, att = q.shape
    l = (n - n_p) // b
    sgw = vn.shape[1]
    bb = 16 if b % 16 == 0 else b
    assert n_p % (bb * l) == 0
    first = n_p // (bb * l)
    tok = lambda width: pl.BlockSpec((bb * l, width), lambda i: (first + i, 0))
    own = lambda width: pl.BlockSpec((bb * l, width), lambda i: (i, 0))
    past = pl.BlockSpec((bb, w, kvw), lambda i: (i, 0, 0))
    outs = [jax.ShapeDtypeStruct((n - n_p, att), BF16), jax.ShapeDtypeStruct((n - n_p, sgw), BF16)]
    return pl.pallas_call(
        _sample_kernel,
        grid=(b // bb,),
        in_specs=[
            pl.BlockSpec(memory_space=pltpu.SMEM),
            tok(att), tok(kvw), tok(kvw), past, past, tok(sgw), tok(sgw),
            pl.BlockSpec(wexp.shape, lambda i: (0, 0, 0)),
            pl.BlockSpec(bias.shape, lambda i: (0, 0)),
        ],
        out_specs=[own(att), own(sgw)],
        out_shape=outs,
        compiler_params=_params("parallel"),
        name="sample_mix",
    )(sinks, q, k, v, cache_k, cache_v, vn, u, wexp, bias)


def _merge_kernel(x_ref, a_ref, m_ref, ga_ref, gb_ref, wa_ref, wb_ref, wo_ref, fg_ref, wq_ref, keys_ref,
                  x1_ref, xnt_ref, s1_ref, s2_ref):
    ha = jnp.dot(a_ref[...], wa_ref[...], preferred_element_type=F32)
    hb = jnp.dot(m_ref[...], wb_ref[...], preferred_element_type=F32)
    h = ga_ref[...].astype(F32) * ha + gb_ref[...].astype(F32) * hb
    x1 = x_ref[...] + jnp.dot(h.astype(BF16), wo_ref[...], preferred_element_type=F32)
    x1_ref[...] = x1
    xn32 = _rms(x1, fg_ref[...])
    xnt_ref[...] = xn32.T.astype(BF16)
    xn = xn32.astype(BF16)
    qp = jnp.dot(xn, wq_ref[...], preferred_element_type=F32).astype(BF16)
    half = keys_ref.shape[2]
    nsub = s1_ref.shape[0]
    for hc in range(keys_ref.shape[0]):
        st = lax.dot_general(keys_ref[hc], qp[:, hc * half:(hc + 1) * half],
                             (((1,), (1,)), ((), ())), preferred_element_type=F32)
        dst = s1_ref if hc % 2 == 0 else s2_ref
        for tl in range(nsub):
            dst[tl, hc // 2] = st[:, tl * LANES:(tl + 1) * LANES]


def _merge(xp, xs, ap, a_s, mp, ms, ga, gb, wa, wb, wo, fg, wq, keys):
    (n_p, d), n_s = xp.shape, xs.shape[0]
    n = n_p + n_s
    tb = _token_block(n_p, n_s)
    nbp = n_p // tb
    nsub = tb // LANES
    hc, nk, half = keys.shape
    row = lambda i: (i, 0)
    const2 = lambda i: (0, 0)
    heads = hc // 2
    outs = [
        jax.ShapeDtypeStruct((n, d), F32),
        jax.ShapeDtypeStruct((d, n), BF16),
        jax.ShapeDtypeStruct((n // LANES, heads, nk, LANES), F32),
        jax.ShapeDtypeStruct((n // LANES, heads, nk, LANES), F32),
    ]

    def body(xp_ref, xs_ref, ap_ref, as_ref, mp_ref, ms_ref, *rest):
        side = lambda p_ref, s_ref: _Value(jnp.where(pl.program_id(0) < nbp, p_ref[...], s_ref[...]))
        _merge_kernel(side(xp_ref, xs_ref), side(ap_ref, as_ref), side(mp_ref, ms_ref), *rest)

    prompt = lambda width: pl.BlockSpec((tb, width), lambda i: (jnp.minimum(i, nbp - 1), 0))
    sample = lambda width: pl.BlockSpec((tb, width), lambda i: (jnp.maximum(i - nbp, 0), 0))
    return pl.pallas_call(
        body,
        grid=(n // tb,),
        in_specs=[
            prompt(d), sample(d),
            prompt(ap.shape[1]), sample(ap.shape[1]),
            prompt(mp.shape[1]), sample(mp.shape[1]),
            pl.BlockSpec((tb, d), row),
            pl.BlockSpec((tb, d), row),
            pl.BlockSpec(wa.shape, const2),
            pl.BlockSpec(wb.shape, const2),
            pl.BlockSpec(wo.shape, const2),
            pl.BlockSpec((1, d), const2),
            pl.BlockSpec(wq.shape, const2),
            pl.BlockSpec(keys.shape, lambda i: (0, 0, 0)),
        ],
        out_specs=[
            pl.BlockSpec((tb, d), row),
            pl.BlockSpec((d, tb), lambda i: (0, i)),
            pl.BlockSpec((nsub, heads, nk, LANES), lambda i: (i, 0, 0, 0)),
            pl.BlockSpec((nsub, heads, nk, LANES), lambda i: (i, 0, 0, 0)),
        ],
        out_shape=outs,
        compiler_params=_params("parallel"),
        name="merge",
    )(xp, xs, ap, a_s, mp, ms, ga, gb, wa, wb, wo, fg, wq, keys)


class _Value:
    def __init__(self, value):
        self._value = value

    def __getitem__(self, idx):
        return self._value[idx]


def _oddeven_merge(lo, hi, r):
    step = r * 2
    if step < hi - lo:
        yield from _oddeven_merge(lo, hi, step)
        yield from _oddeven_merge(lo + r, hi, step)
        yield from [(i, i + r) for i in range(lo + r, hi - r, step)]
    else:
        yield (lo, lo + r)


def _oddeven_merge_sort(lo, hi):
    if hi - lo >= 1:
        mid = lo + (hi - lo) // 2
        yield from _oddeven_merge_sort(lo, mid)
        yield from _oddeven_merge_sort(mid + 1, hi)
        yield from _oddeven_merge(lo, hi, 1)


_SORT_TOPK = tuple(_oddeven_merge_sort(0, PEER_TOPK - 1))


def _cmpx(w, i, j):
    a, b = w[i], w[j]
    if b is None:
        return
    if a is None:
        w[i], w[j] = b, None
        return
    w[i], w[j] = jnp.maximum(a, b), jnp.minimum(a, b)


def _top_values(w):
    k = PEER_TOPK
    w = list(w)
    for i, j in _SORT_TOPK:
        _cmpx(w, i, j)
    shift = SUBLANES // 2
    while shift >= 1:
        y = [None if v is None else pltpu.roll(v, shift, 0) for v in w]
        z = []
        for r in range(k):
            a, b = w[r], y[k - 1 - r]
            z.append(b if a is None else a if b is None else jnp.maximum(a, b))
        stride = k // 2
        while stride >= 1:
            for i in range(k):
                if i & stride == 0:
                    _cmpx(z, i, i + stride)
            stride //= 2
        w = z
        shift //= 2
    return w


def _thresh_kernel(s1_ref, s2_ref, g2_ref, gm_ref, g1_ref):
    k = PEER_TOPK
    nk, lanes = s1_ref.shape[2], s1_ref.shape[3]
    nslot = nk // SUBLANES
    assert nslot == k and k == 2 * SUBLANES
    sub = lax.broadcasted_iota(jnp.int32, (SUBLANES, lanes), 0)

    def pack(vals):
        out = vals[0]
        for j in range(1, SUBLANES):
            out = jnp.where(sub == j, vals[j], out)
        return out

    def head(h, carry):
        w1 = [s1_ref[0, h, r * SUBLANES:(r + 1) * SUBLANES, :] for r in range(nslot)]
        w2 = [s2_ref[0, h, r * SUBLANES:(r + 1) * SUBLANES, :] for r in range(nslot)]
        a = _top_values(w1)
        b = _top_values(w2)
        b_lo, b_hi, a_hi = pack(b[:SUBLANES]), pack(b[SUBLANES:]), pack(a[SUBLANES:])
        cands = ([a[0] + b_lo, a[0] + b_hi] + [a[i] + b_lo for i in range(1, SUBLANES)] + [a_hi + b[0]])
        best = _top_values(cands + [None] * (k - len(cands)))
        tau = best[k - 1]
        z = jnp.ones_like(tau)
        for r in range(1, k):
            z = z + jnp.exp(best[r] - best[0])
        inv_z = 1.0 / z
        eb = [jnp.exp(b[j] - b[0]) for j in range(k)]
        gamma = []
        for i in range(k):
            t = jnp.full_like(tau, jnp.inf)
            for j in range(k // (i + 1)):
                t = jnp.where(a[i] + b[j] >= tau, eb[j], t)
            gamma.append(t)
        for r in range(nslot):
            gm = jnp.full_like(tau, jnp.inf)
            for i in range(k - 1, -1, -1):
                gm = jnp.where(w1[r] >= a[i], gamma[i], gm)
            rows = pl.ds(r * SUBLANES, SUBLANES)
            gm_ref[0, h, rows, :] = gm
            g1_ref[0, h, rows, :] = jnp.exp(w1[r] - a[0]) * (0.5 * inv_z)
            g2_ref[0, h, rows, :] = jnp.exp(w2[r] - b[0])
        return carry

    lax.fori_loop(0, s1_ref.shape[1], head, 0)


def _thresholds(s1, s2):
    nsub, heads, nk, lanes = s1.shape
    spec = pl.BlockSpec((1, heads, nk, lanes), lambda i: (i, 0, 0, 0))
    out = jax.ShapeDtypeStruct(s1.shape, F32)
    return pl.pallas_call(
        _thresh_kernel,
        grid=(nsub,),
        in_specs=[spec, spec],
        out_specs=[spec, spec, spec],
        out_shape=[out, out, out],
        compiler_params=_params("parallel"),
        name="peer_thresholds",
    )(s1, s2)


GATE_ROWS = 128
GELU_C0 = math.sqrt(2.0 / math.pi)
GELU_C1 = 0.044715 * GELU_C0
PEER_CHUNK = 512
PEER_TOKENS = 1024


def _gate_columns(at_ref, wt_ref, j0, tls, g2_ref, gm_ref, g1_ref):
    _, heads, nk, lanes = g2_ref.shape

    def tile(j, tl, r0):
        gate = jnp.zeros((GATE_ROWS, lanes), BF16)
        for h in range(heads):
            gm = gm_ref[tl, h, j0 + j:j0 + j + 1, :]
            g1 = jnp.broadcast_to(g1_ref[tl, h, j0 + j:j0 + j + 1, :], (GATE_ROWS, lanes)).astype(BF16)
            g2 = g2_ref[tl, h, r0:r0 + GATE_ROWS, :]
            gate = gate + g1 * jnp.where(g2 >= gm, g2, 0.0).astype(BF16)
        rows = slice(j * nk + r0, j * nk + r0 + GATE_ROWS)
        cols = slice(tl * lanes, (tl + 1) * lanes)
        x = at_ref[rows, cols].astype(BF16)
        t = jnp.tanh(x * (GELU_C0 + GELU_C1 * (x * x)))
        wt_ref[rows, cols] = (x + x * t) * gate

    for tl in tls:
        for j in range(at_ref.shape[0] // nk):
            for r0 in range(0, nk, GATE_ROWS):
                tile(j, tl, r0)


def _peer_kernel(xnt_ref, u0a_ref, u0b_ref, una_ref, unb_ref, va_ref, vb_ref, g2_ref, gm_ref, g1_ref, x1_ref, out_ref,
                 at_a, at_b, wt_a, wt_b, acc, xnt):
    s = pl.program_id(1)
    per = at_a.shape[0] // g2_ref.shape[2]
    tb = xnt_ref.shape[1]
    lanes = g2_ref.shape[3]
    tables = (g2_ref, gm_ref, g1_ref)
    cw = min(MXU_COLS, tb)
    columns = [(slice(c, c + cw), range(c // lanes, (c + cw) // lanes)) for c in range(0, tb, cw)]

    def act(u_ref, at, cs):
        at[:, cs] = jnp.dot(u_ref[...], xnt[:, cs], preferred_element_type=F32)

    def mix(v_ref, wt, cs):
        acc[:, cs] += jnp.dot(v_ref[...], wt[:, cs], preferred_element_type=F32)

    first = (s == 0) & (pl.program_id(0) == 0)

    @pl.when(first | (s == pl.num_programs(1) - 1))
    def _():
        xnt[...] = xnt_ref[...]

    @pl.when(first)
    def _():
        for cs, _ in columns:
            act(u0a_ref, at_a, cs)
            act(u0b_ref, at_b, cs)

    @pl.when(s == 0)
    def _():
        acc[...] = jnp.zeros_like(acc)

    for at, wt, j0, v_ref, un_ref in ((at_a, wt_a, 0, va_ref, una_ref), (at_b, wt_b, per, vb_ref, unb_ref)):
        for cs, tls in columns:
            _gate_columns(at, wt, j0, tls, *tables)
            mix(v_ref, wt, cs)
            act(un_ref, at, cs)

    @pl.when(s == pl.num_programs(1) - 1)
    def _():
        out_ref[...] = x1_ref[...] + acc[...].T


def _peer(xnt, u, vt, g2, gm, g1, x1):
    d, n = xnt.shape
    ne = u.shape[0]
    tb = PEER_TOKENS if n % PEER_TOKENS == 0 else _token_block(n)
    nsub = tb // LANES
    _, heads, nk, _ = g2.shape
    ec = PEER_CHUNK
    assert ne % (2 * ec) == 0 and ec % nk == 0 and ne == nk * nk and nk % GATE_ROWS == 0
    nc = ne // ec
    per = ec // nk
    keyed2 = pl.BlockSpec((nsub, heads, nk, LANES), lambda i, s: (i, 0, 0, 0))
    keyed1 = pl.BlockSpec((nsub, heads, 2 * per, LANES), lambda i, s: (i, 0, s, 0))
    steps, nb = nc // 2, n // tb
    ahead = lambda i, s: jnp.minimum(i + (s == steps - 1).astype(jnp.int32), nb - 1)
    return pl.pallas_call(
        _peer_kernel,
        grid=(nb, steps),
        in_specs=[
            pl.BlockSpec((d, tb), lambda i, s: (0, ahead(i, s))),
            pl.BlockSpec((ec, d), lambda i, s: (0, 0)),
            pl.BlockSpec((ec, d), lambda i, s: (1, 0)),
            pl.BlockSpec((ec, d), lambda i, s: ((2 * s + 2) % nc, 0)),
            pl.BlockSpec((ec, d), lambda i, s: ((2 * s + 3) % nc, 0)),
            pl.BlockSpec((d, ec), lambda i, s: (0, 2 * s)),
            pl.BlockSpec((d, ec), lambda i, s: (0, 2 * s + 1)),
            keyed2, keyed1, keyed1,
            pl.BlockSpec((tb, d), lambda i, s: (i, 0)),
        ],
        out_specs=pl.BlockSpec((tb, d), lambda i, s: (i, 0)),
        out_shape=jax.ShapeDtypeStruct((n, d), F32),
        scratch_shapes=[
            pltpu.VMEM((ec, tb), F32),
            pltpu.VMEM((ec, tb), F32),
            pltpu.VMEM((ec, tb), BF16),
            pltpu.VMEM((ec, tb), BF16),
            pltpu.VMEM((d, tb), F32),
            pltpu.VMEM((d, tb), BF16),
        ],
        compiler_params=_params("arbitrary", "arbitrary"),
        name="peer_mix",
    )(xnt, u, u, u, u, vt, vt, g2, gm, g1, x1)


def _final_kernel(x2_ref, p_ref, g_ref, wg_ref, wp_ref, y_ref):
    x2 = x2_ref[...]
    xn = _rms(x2, g_ref[...]).astype(BF16)
    gate = jax.nn.sigmoid(jnp.dot(xn, wg_ref[...], preferred_element_type=F32))
    y_ref[...] = x2 + gate * jnp.dot(p_ref[...].astype(BF16), wp_ref[...], preferred_element_type=F32)


def _final(x2, p, g, wg, wp, row0):
    n, pd = p.shape
    d = x2.shape[1]
    tb = PEER_TOKENS if n % PEER_TOKENS == 0 and row0 % PEER_TOKENS == 0 else _token_block(n, row0)
    first = row0 // tb
    src = lambda i: (first + i, 0)
    row = lambda i: (i, 0)
    const = lambda i: (0, 0)
    return pl.pallas_call(
        _final_kernel,
        grid=(n // tb,),
        in_specs=[
            pl.BlockSpec((tb, d), src),
            pl.BlockSpec((tb, pd), row),
            pl.BlockSpec((1, d), const),
            pl.BlockSpec(wg.shape, const),
            pl.BlockSpec(wp.shape, const),
        ],
        out_specs=pl.BlockSpec((tb, d), row),
        out_shape=jax.ShapeDtypeStruct((n, d), F32),
        compiler_params=_params("parallel"),
        name="ple_epilogue",
    )(x2, p, g, wg, wp)


def _layer(xp, xs, pp, ps, past_k, past_v, lp):
    b, s, d = xp.shape
    bd, l, _ = xs.shape
    n_p, n_s = b * s, bd * l
    kvw = N_KV_HEADS * HEAD_DIM
    sgw = d // 2
    gd = sgw // SGU_GROUPS
    xp2, xs2 = xp.reshape(n_p, d), xs.reshape(n_s, d)

    q, k, v, u, vn, ga, gb = _inproj(
        xp2, xs2, lp['attn_norm_g'][None], lp['w_in'].astype(BF16),
        jnp.tile(lp['q_norm_g'], N_HEADS)[None], jnp.tile(lp['k_norm_g'], N_KV_HEADS)[None],
        lp['sgu_norm_g'][None], lp['sgu_norm_b'][None])

    sgu_w, sgu_b = lp['sgu_w'], lp['sgu_b']
    bias_p = jnp.repeat(sgu_b.T, gd, axis=1)
    a_p, m_p = _prompt_mix(lp['attn_sinks'], q, k, v, vn, u, sgu_w, bias_p, b, s)
    wexp = jnp.repeat(jnp.transpose(sgu_w[:, :l, :l], (2, 1, 0)), gd, axis=2)
    a_s, m_s = _sample_mix(lp['attn_sinks'], q, k, v, past_k.reshape(bd, -1, kvw),
                           past_v.reshape(bd, -1, kvw), vn, u, wexp, bias_p[:l], n_p)

    keys = lp['peer_sub_keys'].reshape(2 * PEER_HEADS, PEER_N_KEYS, -1).astype(BF16)
    x1, xn1, s1, s2 = _merge(xp2, xs2, a_p, a_s, m_p, m_s, ga, gb, lp['w_branch_a'].astype(BF16),
                             lp['w_branch_b'].astype(BF16), lp['w_out'].astype(BF16), lp['ffn_norm_g'][None],
                             lp['peer_w_q'].astype(BF16), keys)
    g2, gm, g1 = _thresholds(s1, s2)
    x2 = _peer(xn1, lp['peer_u'].astype(BF16), lp['peer_v'].astype(BF16).T, g2, gm, g1, x1)
    ple = (lp['ple_norm_g'][None], lp['w_ple_gate'].astype(BF16), lp['w_ple'].astype(BF16))
    y_p = _final(x2, pp.reshape(n_p, -1), *ple, 0)
    y_s = _final(x2, ps.reshape(n_s, -1), *ple, n_p)

    wp = min(WINDOW, s)
    tail = lambda t, rows: jnp.stack([t[(bi + 1) * s - rows:(bi + 1) * s] for bi in range(b)])
    heads = lambda t: t.reshape(t.shape[0], t.shape[1], N_KV_HEADS, HEAD_DIM)
    return (y_p.reshape(b, s, d), y_s.reshape(bd, l, d),
            heads(tail(k, wp)), heads(tail(v, wp)),
            heads(k[n_p:].reshape(bd, l, kvw)), heads(v[n_p:].reshape(bd, l, kvw)),
            tail(vn, CHUNK), vn[n_p:].reshape(bd, l, sgw))


def kernel(x_prompt, x_sample, cache_k, cache_v, p_prompt, p_sample, attn_norm_g, w_in, q_norm_g, k_norm_g, attn_sinks, sgu_norm_g, sgu_norm_b, sgu_w, sgu_b, w_branch_a, w_branch_b, w_out, ffn_norm_g, peer_w_q, peer_sub_keys, peer_u, peer_v, ple_norm_g, w_ple, w_ple_gate):
    depth = w_in.shape[0]
    hp, hs = x_prompt, x_sample
    outs = [[] for _ in range(6)]
    for i in range(depth):
        lp = dict(attn_norm_g=attn_norm_g[i], w_in=w_in[i], q_norm_g=q_norm_g[i], k_norm_g=k_norm_g[i],
                  attn_sinks=attn_sinks[i], sgu_norm_g=sgu_norm_g[i], sgu_norm_b=sgu_norm_b[i],
                  sgu_w=sgu_w[i], sgu_b=sgu_b[i], w_branch_a=w_branch_a[i], w_branch_b=w_branch_b[i],
                  w_out=w_out[i], ffn_norm_g=ffn_norm_g[i], peer_w_q=peer_w_q[i],
                  peer_sub_keys=peer_sub_keys[i], peer_u=peer_u[i], peer_v=peer_v[i],
                  ple_norm_g=ple_norm_g[i], w_ple=w_ple[i], w_ple_gate=w_ple_gate[i])
        res = _layer(hp, hs, p_prompt[i], p_sample[i], cache_k[i], cache_v[i], lp)
        hp, hs = res[0], res[1]
        for lst, t in zip(outs, res[2:]):
            lst.append(t)
    return (hp, hs) + tuple(jnp.stack(o) for o in outs)
```
